```python
import math
import jax, jax.numpy as jnp
from jax import lax
import numpy as np

D_MODEL = 1024
BATCH = 8
SEQ = 4096
DEPTH = 1

N_ATTN_HEADS = 4
QK_DIM = 64
V_DIM = 2 * QK_DIM
ATTN_WIDTH = N_ATTN_HEADS * V_DIM
Q_COLS = N_ATTN_HEADS * 2 * QK_DIM
K_COLS = N_ATTN_HEADS * 2 * QK_DIM
V_COLS = N_ATTN_HEADS * V_DIM
CONV_WIDTH = D_MODEL // 2
CONV_GROUPS = 8
CONV_GROUP_DIM = CONV_WIDTH // CONV_GROUPS
CONV_K = 3
MIX_WIDTH = ATTN_WIDTH + CONV_WIDTH
IN_COLS = Q_COLS + K_COLS + V_COLS + 3 * CONV_WIDTH
D_FF = 2816
FFN_RESIDUAL_WEIGHT = 0.5
Q_BLOCK = 128
NORM_EPS = 1e-6

kernel_name = "hymba_diffattn_shortconv_macaron"


def rms_norm(x, g, eps=NORM_EPS):
    xf = x.astype(jnp.float32)
    y = xf * lax.rsqrt(jnp.mean(xf * xf, axis=-1, keepdims=True) + eps)
    return (y * g.astype(jnp.float32)).astype(x.dtype)


def swiglu_ffn(h, w_gate, w_up, w_down):
    return (jax.nn.silu(h @ w_gate) * (h @ w_up)) @ w_down


def alibi_slopes(n_heads):
    idx = jnp.arange(1, n_heads + 1, dtype=jnp.float32)
    return jnp.exp2(-8.0 * idx / n_heads)


def diff_attention(q, k, v, lam, slopes):
    b, s, h, _, dk = q.shape
    nb = s // Q_BLOCK
    scale = dk ** -0.5
    kf = k.astype(jnp.float32)
    pos_k = jnp.arange(s)
    q_blocks = (q.astype(jnp.float32) * scale).reshape(b, nb, Q_BLOCK, h, 2, dk).swapaxes(0, 1)

    def one_block(args):
        q_blk, start = args
        pos_q = start + jnp.arange(Q_BLOCK)
        dist = (pos_q[:, None] - pos_k[None, :]).astype(jnp.float32)
        scores = jnp.einsum('bqhmd,bkhmd->bhmqk', q_blk, kf)
        scores = scores - slopes[None, :, None, None, None] * dist
        scores = jnp.where(dist >= 0, scores, -jnp.inf)
        p = jax.nn.softmax(scores, axis=-1)
        w = p[:, :, 0] - lam * p[:, :, 1]
        return jnp.einsum('bhqk,bkhd->bqhd', w.astype(v.dtype), v)

    out = lax.map(one_block, (q_blocks, jnp.arange(nb) * Q_BLOCK))
    return out.swapaxes(0, 1).reshape(b, s, h, v.shape[-1])


def short_conv(u, w):
    c = u.shape[-1]
    return lax.conv_general_dilated(
        u, w[:, None, :].astype(u.dtype), window_strides=(1,),
        padding=[(CONV_K - 1, 0)], dimension_numbers=('NWC', 'WIO', 'NWC'),
        feature_group_count=c)


def setup_inputs(seed: int = 0) -> dict:
    key = jax.random.key(seed)
    ks = jax.random.split(key, 24)
    f32 = jnp.float32

    def nrm(k, shape, scale):
        return jax.random.normal(k, shape, f32) * scale

    def gain(k, shape):
        return 1.0 + 0.02 * jax.random.normal(k, shape, f32)

    L, D, F = DEPTH, D_MODEL, D_FF
    return {
        "x": jax.random.normal(ks[0], (BATCH, SEQ, D), f32),
        "ffn1_norm": gain(ks[1], (L, D)),
        "ffn1_w_gate": nrm(ks[2], (L, D, F), D ** -0.5),
        "ffn1_w_up": nrm(ks[3], (L, D, F), D ** -0.5),
        "ffn1_w_down": nrm(ks[4], (L, F, D), F ** -0.5),
        "mix_norm": gain(ks[5], (L, D)),
        "w_in": nrm(ks[6], (L, D, IN_COLS), D ** -0.5),
        "q_norm": gain(ks[7], (L, QK_DIM)),
        "k_norm": gain(ks[8], (L, QK_DIM)),
        "lambda_q1": nrm(ks[9], (L, QK_DIM), 0.1),
        "lambda_k1": nrm(ks[10], (L, QK_DIM), 0.1),
        "lambda_q2": nrm(ks[11], (L, QK_DIM), 0.1),
        "lambda_k2": nrm(ks[12], (L, QK_DIM), 0.1),
        "attn_subln": gain(ks[13], (L, V_DIM)),
        "conv_w": nrm(ks[14], (L, CONV_K, CONV_WIDTH), CONV_K ** -0.5),
        "conv_norm": gain(ks[15], (L, CONV_WIDTH)),
        "w_out": nrm(ks[16], (L, MIX_WIDTH, D), MIX_WIDTH ** -0.5),
        "ffn2_norm": gain(ks[17], (L, D)),
        "ffn2_w_gate": nrm(ks[18], (L, D, F), D ** -0.5),
        "ffn2_w_up": nrm(ks[19], (L, D, F), D ** -0.5),
        "ffn2_w_down": nrm(ks[20], (L, F, D), F ** -0.5),
        "final_norm": gain(ks[21], (L, D)),
    }


def reference(x, ffn1_norm, ffn1_w_gate, ffn1_w_up, ffn1_w_down, mix_norm, w_in,
              q_norm, k_norm, lambda_q1, lambda_k1, lambda_q2, lambda_k2, attn_subln,
              conv_w, conv_norm, w_out, ffn2_norm, ffn2_w_gate, ffn2_w_up, ffn2_w_down,
              final_norm):
    b, s, _ = x.shape
    slopes = alibi_slopes(N_ATTN_HEADS)
    splits = np.cumsum([Q_COLS, K_COLS, V_COLS, CONV_WIDTH, CONV_WIDTH]).tolist()
    for i in range(DEPTH):
        lam_init = 0.8 - 0.6 * math.exp(-0.3 * i)
        h = rms_norm(x, ffn1_norm[i])
        x = x + FFN_RESIDUAL_WEIGHT * swiglu_ffn(h, ffn1_w_gate[i], ffn1_w_up[i], ffn1_w_down[i])

        h = rms_norm(x, mix_norm[i])
        proj = h @ w_in[i]
        q, k, v, gate_b, gate_c, hc = jnp.split(proj, splits, axis=-1)

        q = rms_norm(q.reshape(b, s, N_ATTN_HEADS, 2, QK_DIM), q_norm[i])
        k = rms_norm(k.reshape(b, s, N_ATTN_HEADS, 2, QK_DIM), k_norm[i])
        v = v.reshape(b, s, N_ATTN_HEADS, V_DIM)
        lq1 = lambda_q1[i].astype(jnp.float32)
        lk1 = lambda_k1[i].astype(jnp.float32)
        lq2 = lambda_q2[i].astype(jnp.float32)
        lk2 = lambda_k2[i].astype(jnp.float32)
        lam = jnp.exp(jnp.sum(lq1 * lk1)) - jnp.exp(jnp.sum(lq2 * lk2)) + lam_init
        a = diff_attention(q, k, v, lam, slopes)
        a = (rms_norm(a, attn_subln[i]) * (1.0 - lam_init)).reshape(b, s, ATTN_WIDTH)

        c = gate_b * short_conv(gate_c * hc, conv_w[i])
        c = rms_norm(c.reshape(b, s, CONV_GROUPS, CONV_GROUP_DIM),
                     conv_norm[i].reshape(CONV_GROUPS, CONV_GROUP_DIM)).reshape(b, s, CONV_WIDTH)

        mixed = jnp.concatenate([a, c], axis=-1)
        x = x + mixed @ w_out[i]

        h = rms_norm(x, ffn2_norm[i])
        x = x + FFN_RESIDUAL_WEIGHT * swiglu_ffn(h, ffn2_w_gate[i], ffn2_w_up[i], ffn2_w_down[i])

        x = rms_norm(x, final_norm[i])
    return x
```

```python
import functools
import math

import jax
import jax.numpy as jnp
from jax import lax
from jax.experimental import pallas as pl
from jax.experimental.pallas import tpu as pltpu

F32 = jnp.float32
BF16 = jnp.bfloat16

N_HEADS = 4
QK_DIM = 64
V_DIM = 128
HEAD_COLS = 2 * QK_DIM
ATTN_W = N_HEADS * V_DIM
CONV_W = 512
CONV_GROUP = 64
CONV_K = 3
NORM_EPS = 1e-6
FFN_RES_W = 0.5
LAM_INIT = 0.8 - 0.6 * math.exp(-0.3 * 0)
NEG_BIG = -1e30

FFN_CHUNK = 256
TOKEN_TILE = 512
ATT_BLOCK = 256
CARRY_ROWS = 8
VMEM_LIMIT = 60000 * 1024


def _rms(x, gain):
    ms = jnp.mean(x * x, axis=-1, keepdims=True)
    return x * lax.rsqrt(ms + NORM_EPS) * gain


def _group_rms_scale(t, gmat_ref, group):
    ss = jnp.dot((t * t).astype(BF16), gmat_ref[...], preferred_element_type=F32)
    return lax.rsqrt(ss * (1.0 / group) + NORM_EPS)


def _swiglu(h, wg_ref, wu_ref, wd_ref, a_buf):
    d_ff = wg_ref.shape[1]
    for c in range(d_ff // FFN_CHUNK):
        cols = slice(c * FFN_CHUNK, (c + 1) * FFN_CHUNK)
        g = jnp.dot(h, wg_ref[:, cols], preferred_element_type=F32)
        u = jnp.dot(h, wu_ref[:, cols], preferred_element_type=F32)
        a_buf[:, cols] = (g * jax.nn.sigmoid(g) * u).astype(BF16)
    return jnp.dot(a_buf[...], wd_ref[...], preferred_element_type=F32)


def _front_kernel(x_ref, g1_ref, wg_ref, wu_ref, wd_ref, g2_ref, win_ref,
                  gmat_ref, qg_ref, kg_ref, cw_ref, cg_ref,
                  x1_ref, qt_ref, k_ref, vt_ref, c_ref,
                  a_buf, u_buf):
    tm = x_ref.shape[1]
    x = x_ref[0]
    h = _rms(x, g1_ref[...]).astype(BF16)
    x1 = x + FFN_RES_W * _swiglu(h, wg_ref, wu_ref, wd_ref, a_buf)
    x1_ref[0] = x1

    h2 = _rms(x1, g2_ref[...]).astype(BF16)

    def proj(j):
        return jnp.dot(h2, win_ref[:, j * 512:(j + 1) * 512], preferred_element_type=F32)

    q = proj(0)
    q = q * _group_rms_scale(q, gmat_ref, QK_DIM) * qg_ref[...]
    qt_ref[0] = q.T.astype(BF16)
    k = proj(1)
    k = k * _group_rms_scale(k, gmat_ref, QK_DIM) * kg_ref[...]
    k_ref[0] = k.astype(BF16)
    vt_ref[0] = proj(2).T.astype(BF16)

    gate_b = proj(3)
    u = proj(4) * proj(5)

    @pl.when(pl.program_id(1) == 0)
    def _():
        u_buf[0:CARRY_ROWS, :] = jnp.zeros((CARRY_ROWS, CONV_W), F32)

    u_buf[CARRY_ROWS:CARRY_ROWS + tm, :] = u
    y = (cw_ref[0:1, :] * u_buf[CARRY_ROWS - 2:CARRY_ROWS - 2 + tm, :]
         + cw_ref[1:2, :] * u_buf[CARRY_ROWS - 1:CARRY_ROWS - 1 + tm, :]
         + cw_ref[2:3, :] * u)
    u_buf[0:CARRY_ROWS, :] = u_buf[tm:tm + CARRY_ROWS, :]
    c = gate_b * y
    c = c * _group_rms_scale(c, gmat_ref, CONV_GROUP) * cg_ref[...]
    c_ref[0] = c.astype(BF16)


def _attn_kernel(slopes_ref, lam_ref, qt_ref, k_ref, vt_ref, sg_ref, o_ref,
                 rhs_buf, bias_buf, dbias_buf, acc_buf):
    blk = ATT_BLOCK
    seq = k_ref.shape[1]
    nq = seq // blk
    slope = slopes_ref[pl.program_id(1)]
    lam = lam_ref[0]

    row = lax.broadcasted_iota(jnp.int32, (blk, 2 * blk), 0)
    col = lax.broadcasted_iota(jnp.int32, (blk, 2 * blk), 1)
    col = jnp.where(col >= blk, col - blk, col)
    bias = slope * row.astype(F32)
    bias_buf[...] = bias
    dbias_buf[...] = jnp.where(row <= col, bias, NEG_BIG)

    rhs_buf[...] = jnp.zeros(rhs_buf.shape, BF16)

    def q_block(qi, carry):
        q0 = pl.multiple_of(qi * blk, blk)
        rhs_buf[0:QK_DIM, 0:blk] = qt_ref[0, 0:QK_DIM, pl.ds(q0, blk)]
        rhs_buf[QK_DIM:HEAD_COLS, blk:2 * blk] = qt_ref[0, QK_DIM:HEAD_COLS, pl.ds(q0, blk)]
        acc_buf[...] = jnp.zeros(acc_buf.shape, F32)

        def step(kb, m, l, bias_ref):
            k0 = pl.multiple_of(kb * blk, blk)
            s = jnp.dot(k_ref[0, pl.ds(k0, blk), :], rhs_buf[...], preferred_element_type=F32)
            t = s + bias_ref[...]
            shift = slope * k0.astype(F32)
            m_new = jnp.maximum(m, jnp.max(t, axis=0, keepdims=True) + shift)
            alpha = jnp.exp(m - m_new)
            p = jnp.exp(t - (m_new - shift))
            l_new = alpha * l + jnp.sum(p, axis=0, keepdims=True)
            pv = jnp.dot(vt_ref[0, :, pl.ds(k0, blk)], p.astype(BF16), preferred_element_type=F32)
            acc_buf[...] = acc_buf[...] * alpha + pv
            return m_new, l_new

        m0 = jnp.full((1, 2 * blk), NEG_BIG, F32)
        l0 = jnp.zeros((1, 2 * blk), F32)
        m, l = lax.fori_loop(0, qi, lambda kb, ml: step(kb, ml[0], ml[1], bias_buf), (m0, l0))
        m, l = step(qi, m, l, dbias_buf)

        o = acc_buf[...] / l
        a = o[:, 0:blk] - lam * o[:, blk:2 * blk]
        ms = jnp.mean(a * a, axis=0, keepdims=True)
        a = a * lax.rsqrt(ms + NORM_EPS)
        o_ref[0, pl.ds(q0, blk), :] = (a.T * sg_ref[...]).astype(BF16)
        return carry

    lax.fori_loop(0, nq, q_block, 0)


def _back_kernel(x1_ref, a_ref, c_ref, wo_ref, g3_ref, wg_ref, wu_ref, wd_ref, gf_ref,
                 out_ref, a_buf):
    x2 = (x1_ref[0]
          + jnp.dot(a_ref[0], wo_ref[0:ATTN_W, :], preferred_element_type=F32)
          + jnp.dot(c_ref[0], wo_ref[ATTN_W:ATTN_W + CONV_W, :], preferred_element_type=F32))
    h = _rms(x2, g3_ref[...]).astype(BF16)
    x3 = x2 + FFN_RES_W * _swiglu(h, wg_ref, wu_ref, wd_ref, a_buf)
    out_ref[0] = _rms(x3, gf_ref[...])


def _resident(shape):
    return pl.BlockSpec(shape, lambda *_: (0,) * len(shape), pipeline_mode=pl.Buffered(1))


def kernel(x, ffn1_norm, ffn1_w_gate, ffn1_w_up, ffn1_w_down, mix_norm, w_in, q_norm, k_norm,
           lambda_q1, lambda_k1, lambda_q2, lambda_k2, attn_subln, conv_w, conv_norm, w_out,
           ffn2_norm, ffn2_w_gate, ffn2_w_up, ffn2_w_down, final_norm):
    b, s, d = x.shape
    d_ff = ffn1_w_gate.shape[-1]
    tm = TOKEN_TILE
    nt = s // tm
    assert s % tm == 0 and s % ATT_BLOCK == 0 and d_ff % FFN_CHUNK == 0
    assert ffn1_w_gate.shape[0] == 1, "single layer"

    row = lambda v: v.reshape(1, -1).astype(F32)
    bf = lambda w: w[0].astype(BF16)
    n_groups = 512 // QK_DIM
    gid = jnp.arange(512) // QK_DIM
    gmat = (gid[:, None] == gid[None, :]).astype(BF16)
    q_gain = jnp.tile(q_norm[0].astype(F32), n_groups).reshape(1, 512) * (QK_DIM ** -0.5)
    k_gain = jnp.tile(k_norm[0].astype(F32), n_groups).reshape(1, 512)
    slopes = jnp.exp2(-8.0 * jnp.arange(1, N_HEADS + 1, dtype=F32) / N_HEADS)
    lam = (jnp.exp(jnp.sum(lambda_q1[0].astype(F32) * lambda_k1[0].astype(F32)))
           - jnp.exp(jnp.sum(lambda_q2[0].astype(F32) * lambda_k2[0].astype(F32)))
           + LAM_INIT).reshape(1)
    sub_gain = row(attn_subln[0]) * (1.0 - LAM_INIT)

    tok = lambda w: pl.BlockSpec((1, tm, w), lambda bi, ti: (bi, ti, 0))
    tok_t = pl.BlockSpec((1, 512, tm), lambda bi, ti: (bi, 0, ti))
    params = pltpu.CompilerParams(dimension_semantics=("arbitrary", "arbitrary"),
                                  vmem_limit_bytes=VMEM_LIMIT)

    x1, q_t, k_n, v_t, c_n = pl.pallas_call(
        _front_kernel,
        grid=(b, nt),
        in_specs=[tok(d), _resident((1, d)), _resident((d, d_ff)), _resident((d, d_ff)),
                  _resident((d_ff, d)), _resident((1, d)), _resident((d, 6 * 512)),
                  _resident((512, 512)), _resident((1, 512)), _resident((1, 512)),
                  _resident((CONV_K, CONV_W)), _resident((1, CONV_W))],
        out_specs=[tok(d), tok_t, tok(512), tok_t, tok(CONV_W)],
        out_shape=[jax.ShapeDtypeStruct((b, s, d), F32),
                   jax.ShapeDtypeStruct((b, 512, s), BF16),
                   jax.ShapeDtypeStruct((b, s, 512), BF16),
                   jax.ShapeDtypeStruct((b, 512, s), BF16),
                   jax.ShapeDtypeStruct((b, s, CONV_W), BF16)],
        scratch_shapes=[pltpu.VMEM((tm, d_ff), BF16),
                        pltpu.VMEM((tm + CARRY_ROWS, CONV_W), F32)],
        compiler_params=params,
        name="front",
    )(x, row(ffn1_norm[0]), bf(ffn1_w_gate), bf(ffn1_w_up), bf(ffn1_w_down),
      row(mix_norm[0]), bf(w_in), gmat, q_gain, k_gain,
      conv_w[0].astype(F32), row(conv_norm[0]))

    blk = ATT_BLOCK
    attn = pl.pallas_call(
        _attn_kernel,
        grid=(b, N_HEADS),
        in_specs=[pl.BlockSpec(memory_space=pltpu.SMEM),
                  pl.BlockSpec(memory_space=pltpu.SMEM),
                  pl.BlockSpec((1, HEAD_COLS, s), lambda bi, hi: (bi, hi, 0)),
                  pl.BlockSpec((1, s, HEAD_COLS), lambda bi, hi: (bi, 0, hi)),
                  pl.BlockSpec((1, V_DIM, s), lambda bi, hi: (bi, hi, 0)),
                  _resident((1, V_DIM))],
        out_specs=pl.BlockSpec((1, s, V_DIM), lambda bi, hi: (bi, 0, hi)),
        out_shape=jax.ShapeDtypeStruct((b, s, ATTN_W), BF16),
        scratch_shapes=[pltpu.VMEM((HEAD_COLS, 2 * blk), BF16),
                        pltpu.VMEM((blk, 2 * blk), F32),
                        pltpu.VMEM((blk, 2 * blk), F32),
                        pltpu.VMEM((V_DIM, 2 * blk), F32)],
        compiler_params=params,
        name="diff_attn",
    )(slopes, lam, q_t, k_n, v_t, sub_gain)

    out = pl.pallas_call(
        _back_kernel,
        grid=(b, nt),
        in_specs=[tok(d), tok(ATTN_W), tok(CONV_W), _resident((ATTN_W + CONV_W, d)),
                  _resident((1, d)), _resident((d, d_ff)), _resident((d, d_ff)),
                  _resident((d_ff, d)), _resident((1, d))],
        out_specs=tok(d),
        out_shape=jax.ShapeDtypeStruct((b, s, d), F32),
        scratch_shapes=[pltpu.VMEM((tm, d_ff), BF16)],
        compiler_params=params,
        name="back",
    )(x1, attn, c_n, bf(w_out), row(ffn2_norm[0]), bf(ffn2_w_gate), bf(ffn2_w_up),
      bf(ffn2_w_down), row(final_norm[0]))
    return out
```

```python
import functools
import math

import jax
import jax.numpy as jnp
from jax import lax
from jax.experimental import pallas as pl
from jax.experimental.pallas import tpu as pltpu

F32 = jnp.float32
BF16 = jnp.bfloat16

N_HEADS = 4
QK_DIM = 64
V_DIM = 128
HEAD_COLS = 2 * QK_DIM
K_AUG = 2 * HEAD_COLS
N_BIAS_COLS = 3
V_PAD = 16
V_AUG = V_DIM + V_PAD
ATTN_W = N_HEADS * V_DIM
CONV_W = 512
CONV_GROUP = 64
CONV_K = 3
NORM_EPS = 1e-6
FFN_RES_W = 0.5
LAM_INIT = 0.8 - 0.6 * math.exp(-0.3 * 0)
NEG_BIG = -1e30

FFN_CHUNK = 256
TOKEN_TILE = 512
ATT_BLOCK = 256
CARRY_ROWS = 8
VMEM_LIMIT = 60000 * 1024


def _rms(x, gain):
    ms = jnp.mean(x * x, axis=-1, keepdims=True)
    return x * lax.rsqrt(ms + NORM_EPS) * gain


def _group_rms_scale(t, gmat_ref, group):
    ss = jnp.dot((t * t).astype(BF16), gmat_ref[...], preferred_element_type=F32)
    return lax.rsqrt(ss * (1.0 / group) + NORM_EPS)


def _swiglu(h, wg_ref, wu_ref, wd_ref, a_buf):
    d_ff = wg_ref.shape[1]
    for c in range(d_ff // FFN_CHUNK):
        cols = slice(c * FFN_CHUNK, (c + 1) * FFN_CHUNK)
        g = jnp.dot(h, wg_ref[:, cols], preferred_element_type=F32)
        u = jnp.dot(h, wu_ref[:, cols], preferred_element_type=F32)
        a_buf[:, cols] = (g * jax.nn.sigmoid(g) * u).astype(BF16)
    return jnp.dot(a_buf[...], wd_ref[...], preferred_element_type=F32)


def _front_kernel(slopes_ref, x_ref, g1_ref, wg_ref, wu_ref, wd_ref, g2_ref, win_ref,
                  gmat_ref, qg_ref, kg_ref, cw_ref, cg_ref,
                  x1_ref, qt_ref, k_ref, vt_ref, c_ref,
                  a_buf, u_buf):
    tm = x_ref.shape[1]
    x = x_ref[0]
    h = _rms(x, g1_ref[...]).astype(BF16)
    x1 = x + FFN_RES_W * _swiglu(h, wg_ref, wu_ref, wd_ref, a_buf)
    x1_ref[0] = x1

    h2 = _rms(x1, g2_ref[...]).astype(BF16)

    def proj(j):
        return jnp.dot(h2, win_ref[:, j * 512:(j + 1) * 512], preferred_element_type=F32)

    q = proj(0)
    q = q * _group_rms_scale(q, gmat_ref, QK_DIM) * qg_ref[...]
    qt_ref[0] = q.T.astype(BF16)
    k = proj(1)
    k = (k * _group_rms_scale(k, gmat_ref, QK_DIM) * kg_ref[...]).astype(BF16)
    vt = proj(2).T.astype(BF16)
    pos = (pl.program_id(1) * tm + lax.broadcasted_iota(jnp.int32, (tm, 128), 0)).astype(F32)
    lane = lax.broadcasted_iota(jnp.int32, (tm, 128), 1)
    ones_rows = jnp.where(lax.broadcasted_iota(jnp.int32, (V_PAD, tm), 0) == 0, 1.0, 0.0)
    for hd in range(N_HEADS):
        b0 = slopes_ref[hd] * pos
        b_hi = b0.astype(BF16).astype(F32)
        b_mid = (b0 - b_hi).astype(BF16).astype(F32)
        b_lo = b0 - b_hi - b_mid
        extra = jnp.where(lane == 0, b_hi, jnp.where(lane == 1, b_mid,
                                                     jnp.where(lane == 2, b_lo, 0.0)))
        k_ref[0, :, hd * K_AUG:hd * K_AUG + HEAD_COLS] = k[:, hd * HEAD_COLS:(hd + 1) * HEAD_COLS]
        k_ref[0, :, hd * K_AUG + HEAD_COLS:(hd + 1) * K_AUG] = extra.astype(BF16)
        vt_ref[0, hd * V_AUG:hd * V_AUG + V_DIM, :] = vt[hd * V_DIM:(hd + 1) * V_DIM, :]
        vt_ref[0, hd * V_AUG + V_DIM:(hd + 1) * V_AUG, :] = ones_rows.astype(BF16)

    gate_b = proj(3)
    u = proj(4) * proj(5)

    @pl.when(pl.program_id(1) == 0)
    def _():
        u_buf[0:CARRY_ROWS, :] = jnp.zeros((CARRY_ROWS, CONV_W), F32)

    u_buf[CARRY_ROWS:CARRY_ROWS + tm, :] = u
    y = (cw_ref[0:1, :] * u_buf[CARRY_ROWS - 2:CARRY_ROWS - 2 + tm, :]
         + cw_ref[1:2, :] * u_buf[CARRY_ROWS - 1:CARRY_ROWS - 1 + tm, :]
         + cw_ref[2:3, :] * u)
    u_buf[0:CARRY_ROWS, :] = u_buf[tm:tm + CARRY_ROWS, :]
    c = gate_b * y
    c = c * _group_rms_scale(c, gmat_ref, CONV_GROUP) * cg_ref[...]
    c_ref[0] = c.astype(BF16)


def _attn_kernel(lam_ref, qt_ref, k_ref, vt_ref, sg_ref, o_ref,
                 rhs_buf, mask_buf, acc_buf, s_buf):
    blk = ATT_BLOCK
    seq = k_ref.shape[1]
    nq = seq // blk
    lam = lam_ref[0]
    heads = range(N_HEADS)

    row = lax.broadcasted_iota(jnp.int32, (blk, 2 * blk), 0)
    col = lax.broadcasted_iota(jnp.int32, (blk, 2 * blk), 1)
    col = jnp.where(col >= blk, col - blk, col)
    mask_buf[...] = jnp.where(row <= col, 0.0, NEG_BIG)

    rhs_buf[...] = jnp.zeros(rhs_buf.shape, BF16)
    for h in heads:
        rhs_buf[h, HEAD_COLS:HEAD_COLS + 16, :] = jnp.where(
            lax.broadcasted_iota(jnp.int32, (16, 2 * blk), 0) < N_BIAS_COLS, 1.0, 0.0).astype(BF16)

    def block_start(kb):
        return kb * blk if isinstance(kb, int) else pl.multiple_of(kb * blk, blk)

    def scores(kb, slot, h):
        s_buf[slot, h] = jnp.dot(
            k_ref[0, pl.ds(block_start(kb), blk), h * K_AUG:(h + 1) * K_AUG],
            rhs_buf[h], preferred_element_type=F32)

    def softmax_pv(kb, slot, h, m, masked):
        k0 = block_start(kb)
        t = s_buf[slot, h]
        if masked:
            t = t + mask_buf[...]
        m_new = jnp.maximum(m, jnp.max(t, axis=0, keepdims=True))
        alpha = jnp.exp2(m - m_new)
        p = jnp.exp2(t - m_new).astype(BF16)
        pv = jnp.dot(vt_ref[0, h * V_AUG:(h + 1) * V_AUG, pl.ds(k0, blk)], p,
                     preferred_element_type=F32)
        acc_buf[h] = acc_buf[h] * alpha + pv
        return m_new

    def full_step(kb, slot, ms):
        out = []
        for h in heads:
            scores(kb + 1, 1 - slot, h)
            out.append(softmax_pv(kb, slot, h, ms[h], masked=False))
        return tuple(out)

    def q_block(qi, n_pairs, odd):
        q0 = pl.multiple_of(qi * blk, blk)
        for h in heads:
            r0 = h * HEAD_COLS
            rhs_buf[h, 0:QK_DIM, 0:blk] = qt_ref[0, r0:r0 + QK_DIM, pl.ds(q0, blk)]
            rhs_buf[h, QK_DIM:HEAD_COLS, blk:2 * blk] = (
                qt_ref[0, r0 + QK_DIM:r0 + HEAD_COLS, pl.ds(q0, blk)])
        acc_buf[...] = jnp.zeros(acc_buf.shape, F32)
        for h in heads:
            scores(0, 0, h)

        def pair(kp, ms):
            return full_step(2 * kp + 1, 1, full_step(2 * kp, 0, ms))

        ms = lax.fori_loop(0, n_pairs, pair, (jnp.full((1, 2 * blk), NEG_BIG, F32),) * N_HEADS)
        if odd:
            ms = full_step(qi - 1, 0, ms)
        for h in heads:
            softmax_pv(qi, int(odd), h, ms[h], masked=True)

        for h in heads:
            o = acc_buf[h, 0:V_DIM, :] / acc_buf[h, V_DIM:V_DIM + 1, :]
            a = o[:, 0:blk] - lam * o[:, blk:2 * blk]
            ms_a = jnp.mean(a * a, axis=0, keepdims=True)
            a = a * lax.rsqrt(ms_a + NORM_EPS)
            o_ref[0, pl.ds(q0, blk), h * V_DIM:(h + 1) * V_DIM] = (a.T * sg_ref[...]).astype(BF16)

    def q_block_pair(j, carry):
        q_block(2 * j, j, odd=False)
        q_block(2 * j + 1, j, odd=True)
        return carry

    lax.fori_loop(0, nq // 2, q_block_pair, 0)


def _back_kernel(x1_ref, a_ref, c_ref, wo_ref, g3_ref, wg_ref, wu_ref, wd_ref, gf_ref,
                 out_ref, a_buf):
    x2 = (x1_ref[0]
          + jnp.dot(a_ref[0], wo_ref[0:ATTN_W, :], preferred_element_type=F32)
          + jnp.dot(c_ref[0], wo_ref[ATTN_W:ATTN_W + CONV_W, :], preferred_element_type=F32))
    h = _rms(x2, g3_ref[...]).astype(BF16)
    x3 = x2 + FFN_RES_W * _swiglu(h, wg_ref, wu_ref, wd_ref, a_buf)
    out_ref[0] = _rms(x3, gf_ref[...])


def _resident(shape):
    return pl.BlockSpec(shape, lambda *_: (0,) * len(shape), pipeline_mode=pl.Buffered(1))


def kernel(x, ffn1_norm, ffn1_w_gate, ffn1_w_up, ffn1_w_down, mix_norm, w_in, q_norm, k_norm,
           lambda_q1, lambda_k1, lambda_q2, lambda_k2, attn_subln, conv_w, conv_norm, w_out,
           ffn2_norm, ffn2_w_gate, ffn2_w_up, ffn2_w_down, final_norm):
    b, s, d = x.shape
    d_ff = ffn1_w_gate.shape[-1]
    tm = TOKEN_TILE
    nt = s // tm
    assert s % tm == 0 and s % ATT_BLOCK == 0 and d_ff % FFN_CHUNK == 0
    assert ffn1_w_gate.shape[0] == 1, "single layer"

    row = lambda v: v.reshape(1, -1).astype(F32)
    bf = lambda w: w[0].astype(BF16)
    n_groups = 512 // QK_DIM
    gid = jnp.arange(512) // QK_DIM
    gmat = (gid[:, None] == gid[None, :]).astype(BF16)
    log2e = math.log2(math.e)
    q_gain = jnp.tile(q_norm[0].astype(F32), n_groups).reshape(1, 512) * (QK_DIM ** -0.5 * log2e)
    k_gain = jnp.tile(k_norm[0].astype(F32), n_groups).reshape(1, 512)
    slopes = jnp.exp2(-8.0 * jnp.arange(1, N_HEADS + 1, dtype=F32) / N_HEADS) * log2e
    lam = (jnp.exp(jnp.sum(lambda_q1[0].astype(F32) * lambda_k1[0].astype(F32)))
           - jnp.exp(jnp.sum(lambda_q2[0].astype(F32) * lambda_k2[0].astype(F32)))
           + LAM_INIT).reshape(1)
    sub_gain = row(attn_subln[0]) * (1.0 - LAM_INIT)

    tok = lambda w: pl.BlockSpec((1, tm, w), lambda bi, ti: (bi, ti, 0))
    tok_t = lambda r: pl.BlockSpec((1, r, tm), lambda bi, ti: (bi, 0, ti))
    smem = pl.BlockSpec(memory_space=pltpu.SMEM)
    params = pltpu.CompilerParams(dimension_semantics=("arbitrary", "arbitrary"),
                                  vmem_limit_bytes=VMEM_LIMIT)

    x1, q_t, k_n, v_t, c_n = pl.pallas_call(
        _front_kernel,
        grid=(b, nt),
        in_specs=[smem, tok(d), _resident((1, d)), _resident((d, d_ff)), _resident((d, d_ff)),
                  _resident((d_ff, d)), _resident((1, d)), _resident((d, 6 * 512)),
                  _resident((512, 512)), _resident((1, 512)), _resident((1, 512)),
                  _resident((CONV_K, CONV_W)), _resident((1, CONV_W))],
        out_specs=[tok(d), tok_t(512), tok(N_HEADS * K_AUG), tok_t(N_HEADS * V_AUG), tok(CONV_W)],
        out_shape=[jax.ShapeDtypeStruct((b, s, d), F32),
                   jax.ShapeDtypeStruct((b, 512, s), BF16),
                   jax.ShapeDtypeStruct((b, s, N_HEADS * K_AUG), BF16),
                   jax.ShapeDtypeStruct((b, N_HEADS * V_AUG, s), BF16),
                   jax.ShapeDtypeStruct((b, s, CONV_W), BF16)],
        scratch_shapes=[pltpu.VMEM((tm, d_ff), BF16),
                        pltpu.VMEM((tm + CARRY_ROWS, CONV_W), F32)],
        compiler_params=params,
        name="front",
    )(slopes, x, row(ffn1_norm[0]), bf(ffn1_w_gate), bf(ffn1_w_up), bf(ffn1_w_down),
      row(mix_norm[0]), bf(w_in), gmat, q_gain, k_gain,
      conv_w[0].astype(F32), row(conv_norm[0]))

    blk = ATT_BLOCK
    attn = pl.pallas_call(
        _attn_kernel,
        grid=(b,),
        in_specs=[smem,
                  pl.BlockSpec((1, 512, s), lambda bi: (bi, 0, 0)),
                  pl.BlockSpec((1, s, N_HEADS * K_AUG), lambda bi: (bi, 0, 0)),
                  pl.BlockSpec((1, N_HEADS * V_AUG, s), lambda bi: (bi, 0, 0)),
                  _resident((1, V_DIM))],
        out_specs=pl.BlockSpec((1, s, ATTN_W), lambda bi: (bi, 0, 0)),
        out_shape=jax.ShapeDtypeStruct((b, s, ATTN_W), BF16),
        scratch_shapes=[pltpu.VMEM((N_HEADS, K_AUG, 2 * blk), BF16),
                        pltpu.VMEM((blk, 2 * blk), F32),
                        pltpu.VMEM((N_HEADS, V_AUG, 2 * blk), F32),
                        pltpu.VMEM((2, N_HEADS, blk, 2 * blk), F32)],
        compiler_params=pltpu.CompilerParams(dimension_semantics=("arbitrary",),
                                             vmem_limit_bytes=VMEM_LIMIT),
        name="diff_attn",
    )(lam, q_t, k_n, v_t, sub_gain)

    out = pl.pallas_call(
        _back_kernel,
        grid=(b, nt),
        in_specs=[tok(d), tok(ATTN_W), tok(CONV_W), _resident((ATTN_W + CONV_W, d)),
                  _resident((1, d)), _resident((d, d_ff)), _resident((d, d_ff)),
                  _resident((d_ff, d)), _resident((1, d))],
        out_specs=tok(d),
        out_shape=jax.ShapeDtypeStruct((b, s, d), F32),
        scratch_shapes=[pltpu.VMEM((tm, d_ff), BF16)],
        compiler_params=params,
        name="back",
    )(x1, attn, c_n, bf(w_out), row(ffn2_norm[0]), bf(ffn2_w_gate), bf(ffn2_w_up),
      bf(ffn2_w_down), row(final_norm[0]))
    return out
```

```python
import math

import jax
import jax.numpy as jnp
from jax import lax
from jax.experimental import pallas as pl
from jax.experimental.pallas import tpu as pltpu

F32 = jnp.float32
BF16 = jnp.bfloat16

N_HEADS = 4
QK_DIM = 64
V_DIM = 128
HEAD_COLS = 2 * QK_DIM
K_AUG = 2 * HEAD_COLS
N_BIAS_COLS = 3
V_PAD = 16
V_AUG = V_DIM + V_PAD
ATTN_W = N_HEADS * V_DIM
CONV_W = 512
CONV_GROUP = 64
GMAT_W = 256
CONV_K = 3
NORM_EPS = 1e-6
FFN_RES_W = 0.5
LAM_INIT = 0.8 - 0.6 * math.exp(-0.3 * 0)
NEG_BIG = -1e30

FFN_CHUNK = 256
TOKEN_TILE = 512
SUB_TILE = 256
NORM_ROWS = 32
ATT_BLOCK = 256
CARRY_ROWS = 8
VMEM_LIMIT = 60000 * 1024


def _rms(x, gain):
    ms = jnp.mean(x * x, axis=-1, keepdims=True)
    return x * lax.rsqrt(ms + NORM_EPS) * gain


def _group_rms_scale(t, gmat_ref, group):
    t2 = (t * t).astype(BF16)
    w = gmat_ref.shape[0]
    ss = jnp.concatenate(
        [jnp.dot(t2[:, c:c + w], gmat_ref[...], preferred_element_type=F32)
         for c in range(0, t.shape[1], w)], axis=1)
    return lax.rsqrt(ss * (1.0 / group) + NORM_EPS)


def _norm_jobs(src_ref, gain_ref, dst_ref, r0, dtype):
    def job(g):
        rows = slice(r0 + g * NORM_ROWS, r0 + (g + 1) * NORM_ROWS)
        dst_ref[rows, :] = _rms(src_ref[rows, :], gain_ref[...]).astype(dtype)
    return [lambda g=g: job(g) for g in range(SUB_TILE // NORM_ROWS)]


def _run(jobs):
    for job in jobs:
        job()


def _gate_up(h_buf, wg_ref, wu_ref, a_buf, r0, side_jobs):
    rows = slice(r0, r0 + SUB_TILE)
    side_jobs = list(side_jobs)
    for c in range(wg_ref.shape[1] // FFN_CHUNK):
        cols = slice(c * FFN_CHUNK, (c + 1) * FFN_CHUNK)
        h = h_buf[rows, :]
        g = jnp.dot(h, wg_ref[:, cols], preferred_element_type=F32)
        u = jnp.dot(h, wu_ref[:, cols], preferred_element_type=F32)
        a_buf[rows, cols] = (g * jax.nn.sigmoid(g) * u).astype(BF16)
        if side_jobs:
            side_jobs.pop(0)()
    _run(side_jobs)


def _front_kernel(slopes_ref, x_ref, g1_ref, wg_ref, wu_ref, wd_ref, g2_ref, win_ref,
                  gmat_ref, qg_ref, kg_ref, cw_ref, cg_ref,
                  x1_ref, qt_ref, k_ref, vt_ref, c_ref,
                  h_buf, a_buf, h2_buf, u_buf, gb_buf):
    tm = x_ref.shape[1]
    x_rows, x1_rows = x_ref.at[0], x1_ref.at[0]
    sub_tiles = list(range(0, tm, SUB_TILE))

    @pl.when(pl.program_id(1) == 0)
    def _():
        u_buf[0:CARRY_ROWS, :] = jnp.zeros((CARRY_ROWS, CONV_W), F32)

    def down(r0):
        rows = slice(r0, r0 + SUB_TILE)
        x1_ref[0, rows, :] = x_ref[0, rows, :] + FFN_RES_W * jnp.dot(
            a_buf[rows, :], wd_ref[...], preferred_element_type=F32)

    def proj(r0, j):
        return jnp.dot(h2_buf[r0:r0 + SUB_TILE, :], win_ref[:, j * 512:(j + 1) * 512],
                       preferred_element_type=F32)

    def q_job(r0):
        q = proj(r0, 0)
        q = q * _group_rms_scale(q, gmat_ref, QK_DIM) * qg_ref[...]
        qt_ref[0, :, r0:r0 + SUB_TILE] = q.T.astype(BF16)

    def kv_job(r0):
        rows = slice(r0, r0 + SUB_TILE)
        k = proj(r0, 1)
        k = (k * _group_rms_scale(k, gmat_ref, QK_DIM) * kg_ref[...]).astype(BF16)
        vt = proj(r0, 2).T.astype(BF16)
        pos = (pl.program_id(1) * tm + r0
               + lax.broadcasted_iota(jnp.int32, (SUB_TILE, 128), 0)).astype(F32)
        lane = lax.broadcasted_iota(jnp.int32, (SUB_TILE, 128), 1)
        ones_rows = jnp.where(lax.broadcasted_iota(jnp.int32, (V_PAD, SUB_TILE), 0) == 0, 1.0, 0.0)
        for hd in range(N_HEADS):
            b0 = slopes_ref[hd] * pos
            b_hi = b0.astype(BF16).astype(F32)
            b_mid = (b0 - b_hi).astype(BF16).astype(F32)
            b_lo = b0 - b_hi - b_mid
            extra = jnp.where(lane == 0, b_hi, jnp.where(lane == 1, b_mid,
                                                         jnp.where(lane == 2, b_lo, 0.0)))
            k_ref[0, rows, hd * K_AUG:hd * K_AUG + HEAD_COLS] = (
                k[:, hd * HEAD_COLS:(hd + 1) * HEAD_COLS])
            k_ref[0, rows, hd * K_AUG + HEAD_COLS:(hd + 1) * K_AUG] = extra.astype(BF16)
            vt_ref[0, hd * V_AUG:hd * V_AUG + V_DIM, rows] = vt[hd * V_DIM:(hd + 1) * V_DIM, :]
            vt_ref[0, hd * V_AUG + V_DIM:(hd + 1) * V_AUG, rows] = ones_rows.astype(BF16)

    def conv_in_job(r0):
        gb_buf[r0:r0 + SUB_TILE, :] = proj(r0, 3)
        u0 = CARRY_ROWS + r0
        u_buf[u0:u0 + SUB_TILE, :] = proj(r0, 4) * proj(r0, 5)

    def conv_out_job(r0):
        u0 = CARRY_ROWS + r0
        y = (cw_ref[0:1, :] * u_buf[u0 - 2:u0 - 2 + SUB_TILE, :]
             + cw_ref[1:2, :] * u_buf[u0 - 1:u0 - 1 + SUB_TILE, :]
             + cw_ref[2:3, :] * u_buf[u0:u0 + SUB_TILE, :])
        c = gb_buf[r0:r0 + SUB_TILE, :] * y
        c = c * _group_rms_scale(c, gmat_ref, CONV_GROUP) * cg_ref[...]
        c_ref[0, r0:r0 + SUB_TILE, :] = c.astype(BF16)

    _run(_norm_jobs(x_rows, g1_ref, h_buf, sub_tiles[0], BF16))
    pending = []
    for i, r0 in enumerate(sub_tiles):
        nxt = sub_tiles[i + 1] if i + 1 < len(sub_tiles) else None
        side = pending + (_norm_jobs(x_rows, g1_ref, h_buf, nxt, BF16) if nxt is not None else [])
        _gate_up(h_buf, wg_ref, wu_ref, a_buf, r0, side)
        down(r0)
        pending = _norm_jobs(x1_rows, g2_ref, h2_buf, r0, BF16)
    _run(pending)
    for i, r0 in enumerate(sub_tiles):
        q_job(r0)
        if i > 0:
            conv_out_job(sub_tiles[i - 1])
        kv_job(r0)
        conv_in_job(r0)
    conv_out_job(sub_tiles[-1])

    u_buf[0:CARRY_ROWS, :] = u_buf[tm:tm + CARRY_ROWS, :]


def _attn_kernel(lam_ref, qt_ref, k_ref, vt_ref, sg_ref, o_ref,
                 rhs_buf, mask_buf, acc_buf, s_buf):
    blk = ATT_BLOCK
    seq = k_ref.shape[1]
    nq = seq // blk
    lam = lam_ref[0]
    heads = range(N_HEADS)

    row = lax.broadcasted_iota(jnp.int32, (blk, 2 * blk), 0)
    col = lax.broadcasted_iota(jnp.int32, (blk, 2 * blk), 1)
    col = jnp.where(col >= blk, col - blk, col)
    mask_buf[...] = jnp.where(row <= col, 0.0, NEG_BIG)

    rhs_buf[...] = jnp.zeros(rhs_buf.shape, BF16)
    for h in heads:
        rhs_buf[h, HEAD_COLS:HEAD_COLS + 16, :] = jnp.where(
            lax.broadcasted_iota(jnp.int32, (16, 2 * blk), 0) < N_BIAS_COLS, 1.0, 0.0).astype(BF16)

    def block_start(kb):
        return kb * blk if isinstance(kb, int) else pl.multiple_of(kb * blk, blk)

    def scores(kb, slot, h):
        s_buf[slot, h] = jnp.dot(
            k_ref[0, pl.ds(block_start(kb), blk), h * K_AUG:(h + 1) * K_AUG],
            rhs_buf[h], preferred_element_type=F32)

    def softmax_pv(kb, slot, h, m, masked):
        k0 = block_start(kb)
        t = s_buf[slot, h]
        if masked:
            t = t + mask_buf[...]
        m_new = jnp.maximum(m, jnp.max(t, axis=0, keepdims=True))
        alpha = jnp.exp2(m - m_new)
        p = jnp.exp2(t - m_new).astype(BF16)
        pv = jnp.dot(vt_ref[0, h * V_AUG:(h + 1) * V_AUG, pl.ds(k0, blk)], p,
                     preferred_element_type=F32)
        acc_buf[h] = acc_buf[h] * alpha + pv
        return m_new

    def full_step(kb, slot, ms):
        out = []
        for h in heads:
            scores(kb + 1, 1 - slot, h)
            out.append(softmax_pv(kb, slot, h, ms[h], masked=False))
        return tuple(out)

    def q_block(qi, n_pairs, odd):
        q0 = pl.multiple_of(qi * blk, blk)
        for h in heads:
            r0 = h * HEAD_COLS
            rhs_buf[h, 0:QK_DIM, 0:blk] = qt_ref[0, r0:r0 + QK_DIM, pl.ds(q0, blk)]
            rhs_buf[h, QK_DIM:HEAD_COLS, blk:2 * blk] = (
                qt_ref[0, r0 + QK_DIM:r0 + HEAD_COLS, pl.ds(q0, blk)])
        acc_buf[...] = jnp.zeros(acc_buf.shape, F32)
        for h in heads:
            scores(0, 0, h)

        def pair(kp, ms):
            return full_step(2 * kp + 1, 1, full_step(2 * kp, 0, ms))

        ms = lax.fori_loop(0, n_pairs, pair, (jnp.full((1, 2 * blk), NEG_BIG, F32),) * N_HEADS)
        if odd:
            ms = full_step(qi - 1, 0, ms)
        for h in heads:
            softmax_pv(qi, int(odd), h, ms[h], masked=True)

        for h in heads:
            o = acc_buf[h, 0:V_DIM, :] / acc_buf[h, V_DIM:V_DIM + 1, :]
            a = o[:, 0:blk] - lam * o[:, blk:2 * blk]
            ms_a = jnp.mean(a * a, axis=0, keepdims=True)
            a = a * lax.rsqrt(ms_a + NORM_EPS)
            o_ref[0, pl.ds(q0, blk), h * V_DIM:(h + 1) * V_DIM] = (a.T * sg_ref[...]).astype(BF16)

    def q_block_pair(j, carry):
        q_block(2 * j, j, odd=False)
        q_block(2 * j + 1, j, odd=True)
        return carry

    lax.fori_loop(0, nq // 2, q_block_pair, 0)


def _back_kernel(x1_ref, a_ref, c_ref, wo_ref, g3_ref, wg_ref, wu_ref, wd_ref, gf_ref,
                 out_ref, x2_buf, h_buf, a_buf):
    tm = x1_ref.shape[1]
    out_rows = out_ref.at[0]
    sub_tiles = list(range(0, tm, SUB_TILE))

    def out_proj(r0):
        rows = slice(r0, r0 + SUB_TILE)
        x2_buf[rows, :] = (
            x1_ref[0, rows, :]
            + jnp.dot(a_ref[0, rows, :], wo_ref[0:ATTN_W, :], preferred_element_type=F32)
            + jnp.dot(c_ref[0, rows, :], wo_ref[ATTN_W:ATTN_W + CONV_W, :],
                      preferred_element_type=F32))

    def down(r0):
        rows = slice(r0, r0 + SUB_TILE)
        out_ref[0, rows, :] = x2_buf[rows, :] + FFN_RES_W * jnp.dot(
            a_buf[rows, :], wd_ref[...], preferred_element_type=F32)

    out_proj(sub_tiles[0])
    _run(_norm_jobs(x2_buf, g3_ref, h_buf, sub_tiles[0], BF16))
    pending = []
    for i, r0 in enumerate(sub_tiles):
        nxt = sub_tiles[i + 1] if i + 1 < len(sub_tiles) else None
        side = list(pending)
        if nxt is not None:
            side = [lambda nxt=nxt: out_proj(nxt)] + side + _norm_jobs(x2_buf, g3_ref, h_buf, nxt, BF16)
        _gate_up(h_buf, wg_ref, wu_ref, a_buf, r0, side)
        down(r0)
        pending = _norm_jobs(out_rows, gf_ref, out_rows, r0, F32)
    _run(pending)


def _resident(shape):
    return pl.BlockSpec(shape, lambda *_: (0,) * len(shape), pipeline_mode=pl.Buffered(1))


def kernel(x, ffn1_norm, ffn1_w_gate, ffn1_w_up, ffn1_w_down, mix_norm, w_in, q_norm, k_norm,
           lambda_q1, lambda_k1, lambda_q2, lambda_k2, attn_subln, conv_w, conv_norm, w_out,
           ffn2_norm, ffn2_w_gate, ffn2_w_up, ffn2_w_down, final_norm):
    b, s, d = x.shape
    d_ff = ffn1_w_gate.shape[-1]
    tm = TOKEN_TILE
    nt = s // tm
    assert s % tm == 0 and tm % SUB_TILE == 0 and d_ff % FFN_CHUNK == 0
    assert s % (2 * ATT_BLOCK) == 0
    assert ffn1_w_gate.shape[0] == 1, "single layer"

    row = lambda v: v.reshape(1, -1).astype(F32)
    bf = lambda w: w[0].astype(BF16)
    n_groups = 512 // QK_DIM
    gid = jnp.arange(GMAT_W) // QK_DIM
    gmat = (gid[:, None] == gid[None, :]).astype(BF16)
    log2e = math.log2(math.e)
    q_gain = jnp.tile(q_norm[0].astype(F32), n_groups).reshape(1, 512) * (QK_DIM ** -0.5 * log2e)
    k_gain = jnp.tile(k_norm[0].astype(F32), n_groups).reshape(1, 512)
    slopes = jnp.exp2(-8.0 * jnp.arange(1, N_HEADS + 1, dtype=F32) / N_HEADS) * log2e
    lam = (jnp.exp(jnp.sum(lambda_q1[0].astype(F32) * lambda_k1[0].astype(F32)))
           - jnp.exp(jnp.sum(lambda_q2[0].astype(F32) * lambda_k2[0].astype(F32)))
           + LAM_INIT).reshape(1)
    sub_gain = row(attn_subln[0]) * (1.0 - LAM_INIT)

    tok = lambda w: pl.BlockSpec((1, tm, w), lambda bi, ti: (bi, ti, 0))
    tok_t = lambda r: pl.BlockSpec((1, r, tm), lambda bi, ti: (bi, 0, ti))
    smem = pl.BlockSpec(memory_space=pltpu.SMEM)
    params = pltpu.CompilerParams(dimension_semantics=("arbitrary", "arbitrary"),
                                  vmem_limit_bytes=VMEM_LIMIT)

    x1, q_t, k_n, v_t, c_n = pl.pallas_call(
        _front_kernel,
        grid=(b, nt),
        in_specs=[smem, tok(d), _resident((1, d)), _resident((d, d_ff)), _resident((d, d_ff)),
                  _resident((d_ff, d)), _resident((1, d)), _resident((d, 6 * 512)),
                  _resident((GMAT_W, GMAT_W)), _resident((1, 512)), _resident((1, 512)),
                  _resident((CONV_K, CONV_W)), _resident((1, CONV_W))],
        out_specs=[tok(d), tok_t(512), tok(N_HEADS * K_AUG), tok_t(N_HEADS * V_AUG), tok(CONV_W)],
        out_shape=[jax.ShapeDtypeStruct((b, s, d), F32),
                   jax.ShapeDtypeStruct((b, 512, s), BF16),
                   jax.ShapeDtypeStruct((b, s, N_HEADS * K_AUG), BF16),
                   jax.ShapeDtypeStruct((b, N_HEADS * V_AUG, s), BF16),
                   jax.ShapeDtypeStruct((b, s, CONV_W), BF16)],
        scratch_shapes=[pltpu.VMEM((tm, d), BF16),
                        pltpu.VMEM((tm, d_ff), BF16),
                        pltpu.VMEM((tm, d), BF16),
                        pltpu.VMEM((tm + CARRY_ROWS, CONV_W), F32),
                        pltpu.VMEM((tm, CONV_W), F32)],
        compiler_params=params,
        name="front",
    )(slopes, x, row(ffn1_norm[0]), bf(ffn1_w_gate), bf(ffn1_w_up), bf(ffn1_w_down),
      row(mix_norm[0]), bf(w_in), gmat, q_gain, k_gain,
      conv_w[0].astype(F32), row(conv_norm[0]))

    blk = ATT_BLOCK
    attn = pl.pallas_call(
        _attn_kernel,
        grid=(b,),
        in_specs=[smem,
                  pl.BlockSpec((1, 512, s), lambda bi: (bi, 0, 0)),
                  pl.BlockSpec((1, s, N_HEADS * K_AUG), lambda bi: (bi, 0, 0)),
                  pl.BlockSpec((1, N_HEADS * V_AUG, s), lambda bi: (bi, 0, 0)),
                  _resident((1, V_DIM))],
        out_specs=pl.BlockSpec((1, s, ATTN_W), lambda bi: (bi, 0, 0)),
        out_shape=jax.ShapeDtypeStruct((b, s, ATTN_W), BF16),
        scratch_shapes=[pltpu.VMEM((N_HEADS, K_AUG, 2 * blk), BF16),
                        pltpu.VMEM((blk, 2 * blk), F32),
                        pltpu.VMEM((N_HEADS, V_AUG, 2 * blk), F32),
                        pltpu.VMEM((2, N_HEADS, blk, 2 * blk), F32)],
        compiler_params=pltpu.CompilerParams(dimension_semantics=("arbitrary",),
                                             vmem_limit_bytes=VMEM_LIMIT),
        name="diff_attn",
    )(lam, q_t, k_n, v_t, sub_gain)

    out = pl.pallas_call(
        _back_kernel,
        grid=(b, nt),
        in_specs=[tok(d), tok(ATTN_W), tok(CONV_W), _resident((ATTN_W + CONV_W, d)),
                  _resident((1, d)), _resident((d, d_ff)), _resident((d, d_ff)),
                  _resident((d_ff, d)), _resident((1, d))],
        out_specs=tok(d),
        out_shape=jax.ShapeDtypeStruct((b, s, d), F32),
        scratch_shapes=[pltpu.VMEM((tm, d), F32),
                        pltpu.VMEM((tm, d), BF16),
                        pltpu.VMEM((tm, d_ff), BF16)],
        compiler_params=params,
        name="back",
    )(x1, attn, c_n, bf(w_out), row(ffn2_norm[0]), bf(ffn2_w_gate), bf(ffn2_w_up),
      bf(ffn2_w_down), row(final_norm[0]))
    return out
```

```python
import math

import jax
import jax.numpy as jnp
from jax import lax
from jax.experimental import pallas as pl
from jax.experimental.pallas import tpu as pltpu

F32 = jnp.float32
BF16 = jnp.bfloat16

N_HEADS = 4
QK_DIM = 64
V_DIM = 128
HEAD_COLS = 2 * QK_DIM
K_AUG = 2 * HEAD_COLS
N_BIAS_COLS = 3
V_PAD = 16
V_AUG = V_DIM + V_PAD
ATTN_W = N_HEADS * V_DIM
CONV_W = 512
CONV_GROUP = 64
GMAT_W = 256
CONV_K = 3
NORM_EPS = 1e-6
FFN_RES_W = 0.5
LAM_INIT = 0.8 - 0.6 * math.exp(-0.3 * 0)
NEG_BIG = -1e30

FFN_CHUNK = 256
TOKEN_TILE = 512
SUB_TILE = 256
NORM_ROWS = 32
ATT_BLOCK = 256
CARRY_ROWS = 8
VMEM_LIMIT = 60000 * 1024


def _rms(x, gain):
    ms = jnp.mean(x * x, axis=-1, keepdims=True)
    return x * lax.rsqrt(ms + NORM_EPS) * gain


def _group_rms_scale(t, gmat_ref, group):
    t2 = (t * t).astype(BF16)
    w = gmat_ref.shape[0]
    ss = jnp.concatenate(
        [jnp.dot(t2[:, c:c + w], gmat_ref[...], preferred_element_type=F32)
         for c in range(0, t.shape[1], w)], axis=1)
    return lax.rsqrt(ss * (1.0 / group) + NORM_EPS)


def _norm_jobs(src_ref, gain_ref, dst_ref, r0, dtype):
    def job(g):
        rows = slice(r0 + g * NORM_ROWS, r0 + (g + 1) * NORM_ROWS)
        dst_ref[rows, :] = _rms(src_ref[rows, :], gain_ref[...]).astype(dtype)
    return [lambda g=g: job(g) for g in range(SUB_TILE // NORM_ROWS)]


def _run(jobs):
    for job in jobs:
        job()


def _gate_up(h_buf, wg_ref, wu_ref, a_buf, r0, side_jobs):
    rows = slice(r0, r0 + SUB_TILE)
    side_jobs = list(side_jobs)
    for c in range(wg_ref.shape[1] // FFN_CHUNK):
        cols = slice(c * FFN_CHUNK, (c + 1) * FFN_CHUNK)
        h = h_buf[rows, :]
        g = jnp.dot(h, wg_ref[:, cols], preferred_element_type=F32)
        u = jnp.dot(h, wu_ref[:, cols], preferred_element_type=F32)
        a_buf[rows, cols] = (g * jax.nn.sigmoid(g) * u).astype(BF16)
        if side_jobs:
            side_jobs.pop(0)()
    _run(side_jobs)


def _front_kernel(slopes_ref, x_ref, g1_ref, wg_ref, wu_ref, wd_ref, g2_ref, win_ref,
                  gmat_ref, qg_ref, kg_ref, cw_ref, cg_ref,
                  x1_ref, qt_ref, k_ref, vt_ref, c_ref,
                  h_buf, a_buf, h2_buf, u_buf, gb_buf):
    tm = x_ref.shape[1]
    x_rows, x1_rows = x_ref.at[0], x1_ref.at[0]
    sub_tiles = list(range(0, tm, SUB_TILE))

    @pl.when(pl.program_id(1) == 0)
    def _():
        u_buf[0:CARRY_ROWS, :] = jnp.zeros((CARRY_ROWS, CONV_W), F32)

    def down(r0):
        rows = slice(r0, r0 + SUB_TILE)
        x1_ref[0, rows, :] = x_ref[0, rows, :] + FFN_RES_W * jnp.dot(
            a_buf[rows, :], wd_ref[...], preferred_element_type=F32)

    def proj(r0, j):
        return jnp.dot(h2_buf[r0:r0 + SUB_TILE, :], win_ref[:, j * 512:(j + 1) * 512],
                       preferred_element_type=F32)

    def q_job(r0):
        q = proj(r0, 0)
        q = q * _group_rms_scale(q, gmat_ref, QK_DIM) * qg_ref[...]
        qt_ref[0, :, r0:r0 + SUB_TILE] = q.T.astype(BF16)

    def kv_job(r0):
        rows = slice(r0, r0 + SUB_TILE)
        k = proj(r0, 1)
        k = (k * _group_rms_scale(k, gmat_ref, QK_DIM) * kg_ref[...]).astype(BF16)
        vt = proj(r0, 2).T.astype(BF16)
        pos = (pl.program_id(1) * tm + r0
               + lax.broadcasted_iota(jnp.int32, (SUB_TILE, 128), 0)).astype(F32)
        lane = lax.broadcasted_iota(jnp.int32, (SUB_TILE, 128), 1)
        ones_rows = jnp.where(lax.broadcasted_iota(jnp.int32, (V_PAD, SUB_TILE), 0) == 0, 1.0, 0.0)
        for hd in range(N_HEADS):
            b0 = slopes_ref[hd] * pos
            b_hi = b0.astype(BF16).astype(F32)
            b_mid = (b0 - b_hi).astype(BF16).astype(F32)
            b_lo = b0 - b_hi - b_mid
            extra = jnp.where(lane == 0, b_hi, jnp.where(lane == 1, b_mid,
                                                         jnp.where(lane == 2, b_lo, 0.0)))
            k_ref[0, rows, hd * K_AUG:hd * K_AUG + HEAD_COLS] = (
                k[:, hd * HEAD_COLS:(hd + 1) * HEAD_COLS])
            k_ref[0, rows, hd * K_AUG + HEAD_COLS:(hd + 1) * K_AUG] = extra.astype(BF16)
            vt_ref[0, hd * V_AUG:hd * V_AUG + V_DIM, rows] = vt[hd * V_DIM:(hd + 1) * V_DIM, :]
            vt_ref[0, hd * V_AUG + V_DIM:(hd + 1) * V_AUG, rows] = ones_rows.astype(BF16)

    def conv_in_job(r0):
        gb_buf[r0:r0 + SUB_TILE, :] = proj(r0, 3)
        u0 = CARRY_ROWS + r0
        u_buf[u0:u0 + SUB_TILE, :] = proj(r0, 4) * proj(r0, 5)

    def conv_out_job(r0):
        u0 = CARRY_ROWS + r0
        y = (cw_ref[0:1, :] * u_buf[u0 - 2:u0 - 2 + SUB_TILE, :]
             + cw_ref[1:2, :] * u_buf[u0 - 1:u0 - 1 + SUB_TILE, :]
             + cw_ref[2:3, :] * u_buf[u0:u0 + SUB_TILE, :])
        c = gb_buf[r0:r0 + SUB_TILE, :] * y
        c = c * _group_rms_scale(c, gmat_ref, CONV_GROUP) * cg_ref[...]
        c_ref[0, r0:r0 + SUB_TILE, :] = c.astype(BF16)

    _run(_norm_jobs(x_rows, g1_ref, h_buf, sub_tiles[0], BF16))
    pending = []
    for i, r0 in enumerate(sub_tiles):
        nxt = sub_tiles[i + 1] if i + 1 < len(sub_tiles) else None
        side = pending + (_norm_jobs(x_rows, g1_ref, h_buf, nxt, BF16) if nxt is not None else [])
        _gate_up(h_buf, wg_ref, wu_ref, a_buf, r0, side)
        down(r0)
        pending = _norm_jobs(x1_rows, g2_ref, h2_buf, r0, BF16)
    _run(pending)
    for i, r0 in enumerate(sub_tiles):
        q_job(r0)
        if i > 0:
            conv_out_job(sub_tiles[i - 1])
        kv_job(r0)
        conv_in_job(r0)
    conv_out_job(sub_tiles[-1])

    u_buf[0:CARRY_ROWS, :] = u_buf[tm:tm + CARRY_ROWS, :]


def _attn_kernel(lam_ref, qt_ref, k_ref, vt_ref, sg_ref, o_ref,
                 rhs_buf, mask_buf, acc_buf, s_buf):
    blk = ATT_BLOCK
    half = 2 * blk
    seq = k_ref.shape[1]
    lam = lam_ref[0]
    heads = range(N_HEADS)
    all_cols, first, second = slice(0, 2 * half), slice(0, half), slice(half, 2 * half)

    row = lax.broadcasted_iota(jnp.int32, (blk, half), 0)
    col = lax.broadcasted_iota(jnp.int32, (blk, half), 1)
    col = jnp.where(col >= blk, col - blk, col)
    mask_buf[...] = jnp.where(row <= col, 0.0, NEG_BIG)

    rhs_buf[...] = jnp.zeros(rhs_buf.shape, BF16)
    for h in heads:
        rhs_buf[h, HEAD_COLS:HEAD_COLS + 16, :] = jnp.where(
            lax.broadcasted_iota(jnp.int32, (16, 2 * half), 0) < N_BIAS_COLS, 1.0, 0.0).astype(BF16)

    def block_start(kb):
        return kb * blk if isinstance(kb, int) else pl.multiple_of(kb * blk, blk)

    def scores(kb, slot, h, cols):
        s = jnp.dot(k_ref[0, pl.ds(block_start(kb), blk), h * K_AUG:(h + 1) * K_AUG],
                    rhs_buf[h, :, cols], preferred_element_type=F32)
        s_buf[slot, h, :, cols] = s
        return jnp.max(s, axis=0, keepdims=True)

    def softmax_pv(kb, slot, h, cols, m, tile_max, masked):
        t = s_buf[slot, h, :, cols]
        if masked:
            t = t + mask_buf[...]
            tile_max = jnp.max(t, axis=0, keepdims=True)
        m_new = jnp.maximum(m, tile_max)
        alpha = jnp.exp2(m - m_new)
        p = jnp.exp2(t - m_new).astype(BF16)
        pv = jnp.dot(vt_ref[0, h * V_AUG:(h + 1) * V_AUG, pl.ds(block_start(kb), blk)], p,
                     preferred_element_type=F32)
        acc_buf[h, :, cols] = acc_buf[h, :, cols] * alpha + pv
        return m_new

    def full_step(kb, slot, carry):
        ms, tms = carry
        new_ms, new_tms = [], []
        for h in heads:
            new_tms.append(scores(kb + 1, 1 - slot, h, all_cols))
            new_ms.append(softmax_pv(kb, slot, h, all_cols, ms[h], tms[h], masked=False))
        return tuple(new_ms), tuple(new_tms)

    def q_block(qb, carry):
        q0 = pl.multiple_of(qb * 2 * blk, 2 * blk)
        for h in heads:
            r0 = h * HEAD_COLS
            for part in range(2):
                qs = pl.ds(q0 + part * blk, blk)
                c0 = part * half
                rhs_buf[h, 0:QK_DIM, c0:c0 + blk] = qt_ref[0, r0:r0 + QK_DIM, qs]
                rhs_buf[h, QK_DIM:HEAD_COLS, c0 + blk:c0 + half] = (
                    qt_ref[0, r0 + QK_DIM:r0 + HEAD_COLS, qs])
        acc_buf[...] = jnp.zeros(acc_buf.shape, F32)
        tms = tuple(scores(0, 0, h, all_cols) for h in heads)
        ms = (jnp.full((1, 2 * half), NEG_BIG, F32),) * N_HEADS

        def pair(kp, c):
            return full_step(2 * kp + 1, 1, full_step(2 * kp, 0, c))

        ms, tms = lax.fori_loop(0, qb, pair, (ms, tms))

        for h in heads:
            scores(2 * qb + 1, 1, h, second)
            softmax_pv(2 * qb, 0, h, first, ms[h][:, first], None, masked=True)
            m_b = softmax_pv(2 * qb, 0, h, second, ms[h][:, second], tms[h][:, second],
                             masked=False)
            softmax_pv(2 * qb + 1, 1, h, second, m_b, None, masked=True)

        for h in heads:
            o = acc_buf[h, 0:V_DIM, :] / acc_buf[h, V_DIM:V_DIM + 1, :]
            for part in range(2):
                c0 = part * half
                a = o[:, c0:c0 + blk] - lam * o[:, c0 + blk:c0 + half]
                ms_a = jnp.mean(a * a, axis=0, keepdims=True)
                a = a * lax.rsqrt(ms_a + NORM_EPS)
                o_ref[0, pl.ds(q0 + part * blk, blk), h * V_DIM:(h + 1) * V_DIM] = (
                    (a.T * sg_ref[...]).astype(BF16))
        return carry

    lax.fori_loop(0, seq // (2 * blk), q_block, 0)


def _back_kernel(x1_ref, a_ref, c_ref, wo_ref, g3_ref, wg_ref, wu_ref, wd_ref, gf_ref,
                 out_ref, x2_buf, h_buf, a_buf):
    tm = x1_ref.shape[1]
    out_rows = out_ref.at[0]
    sub_tiles = list(range(0, tm, SUB_TILE))

    def out_proj(r0):
        rows = slice(r0, r0 + SUB_TILE)
        x2_buf[rows, :] = (
            x1_ref[0, rows, :]
            + jnp.dot(a_ref[0, rows, :], wo_ref[0:ATTN_W, :], preferred_element_type=F32)
            + jnp.dot(c_ref[0, rows, :], wo_ref[ATTN_W:ATTN_W + CONV_W, :],
                      preferred_element_type=F32))

    def down(r0):
        rows = slice(r0, r0 + SUB_TILE)
        out_ref[0, rows, :] = x2_buf[rows, :] + FFN_RES_W * jnp.dot(
            a_buf[rows, :], wd_ref[...], preferred_element_type=F32)

    out_proj(sub_tiles[0])
    _run(_norm_jobs(x2_buf, g3_ref, h_buf, sub_tiles[0], BF16))
    pending = []
    for i, r0 in enumerate(sub_tiles):
        nxt = sub_tiles[i + 1] if i + 1 < len(sub_tiles) else None
        side = list(pending)
        if nxt is not None:
            side = [lambda nxt=nxt: out_proj(nxt)] + side + _norm_jobs(x2_buf, g3_ref, h_buf, nxt, BF16)
        _gate_up(h_buf, wg_ref, wu_ref, a_buf, r0, side)
        down(r0)
        pending = _norm_jobs(out_rows, gf_ref, out_rows, r0, F32)
    _run(pending)


def _resident(shape):
    return pl.BlockSpec(shape, lambda *_: (0,) * len(shape), pipeline_mode=pl.Buffered(1))


def kernel(x, ffn1_norm, ffn1_w_gate, ffn1_w_up, ffn1_w_down, mix_norm, w_in, q_norm, k_norm,
           lambda_q1, lambda_k1, lambda_q2, lambda_k2, attn_subln, conv_w, conv_norm, w_out,
           ffn2_norm, ffn2_w_gate, ffn2_w_up, ffn2_w_down, final_norm):
    b, s, d = x.shape
    d_ff = ffn1_w_gate.shape[-1]
    tm = TOKEN_TILE
    nt = s // tm
    assert s % tm == 0 and tm % SUB_TILE == 0 and d_ff % FFN_CHUNK == 0
    assert s % (2 * ATT_BLOCK) == 0
    assert ffn1_w_gate.shape[0] == 1, "single layer"

    row = lambda v: v.reshape(1, -1).astype(F32)
    bf = lambda w: w[0].astype(BF16)
    n_groups = 512 // QK_DIM
    gid = jnp.arange(GMAT_W) // QK_DIM
    gmat = (gid[:, None] == gid[None, :]).astype(BF16)
    log2e = math.log2(math.e)
    q_gain = jnp.tile(q_norm[0].astype(F32), n_groups).reshape(1, 512) * (QK_DIM ** -0.5 * log2e)
    k_gain = jnp.tile(k_norm[0].astype(F32), n_groups).reshape(1, 512)
    slopes = jnp.exp2(-8.0 * jnp.arange(1, N_HEADS + 1, dtype=F32) / N_HEADS) * log2e
    lam = (jnp.exp(jnp.sum(lambda_q1[0].astype(F32) * lambda_k1[0].astype(F32)))
           - jnp.exp(jnp.sum(lambda_q2[0].astype(F32) * lambda_k2[0].astype(F32)))
           + LAM_INIT).reshape(1)
    sub_gain = row(attn_subln[0]) * (1.0 - LAM_INIT)

    tok = lambda w: pl.BlockSpec((1, tm, w), lambda bi, ti: (bi, ti, 0))
    tok_t = lambda r: pl.BlockSpec((1, r, tm), lambda bi, ti: (bi, 0, ti))
    smem = pl.BlockSpec(memory_space=pltpu.SMEM)
    params = pltpu.CompilerParams(dimension_semantics=("arbitrary", "arbitrary"),
                                  vmem_limit_bytes=VMEM_LIMIT)

    x1, q_t, k_n, v_t, c_n = pl.pallas_call(
        _front_kernel,
        grid=(b, nt),
        in_specs=[smem, tok(d), _resident((1, d)), _resident((d, d_ff)), _resident((d, d_ff)),
                  _resident((d_ff, d)), _resident((1, d)), _resident((d, 6 * 512)),
                  _resident((GMAT_W, GMAT_W)), _resident((1, 512)), _resident((1, 512)),
                  _resident((CONV_K, CONV_W)), _resident((1, CONV_W))],
        out_specs=[tok(d), tok_t(512), tok(N_HEADS * K_AUG), tok_t(N_HEADS * V_AUG), tok(CONV_W)],
        out_shape=[jax.ShapeDtypeStruct((b, s, d), F32),
                   jax.ShapeDtypeStruct((b, 512, s), BF16),
                   jax.ShapeDtypeStruct((b, s, N_HEADS * K_AUG), BF16),
                   jax.ShapeDtypeStruct((b, N_HEADS * V_AUG, s), BF16),
                   jax.ShapeDtypeStruct((b, s, CONV_W), BF16)],
        scratch_shapes=[pltpu.VMEM((tm, d), BF16),
                        pltpu.VMEM((tm, d_ff), BF16),
                        pltpu.VMEM((tm, d), BF16),
                        pltpu.VMEM((tm + CARRY_ROWS, CONV_W), F32),
                        pltpu.VMEM((tm, CONV_W), F32)],
        compiler_params=params,
        name="front",
    )(slopes, x, row(ffn1_norm[0]), bf(ffn1_w_gate), bf(ffn1_w_up), bf(ffn1_w_down),
      row(mix_norm[0]), bf(w_in), gmat, q_gain, k_gain,
      conv_w[0].astype(F32), row(conv_norm[0]))

    blk = ATT_BLOCK
    attn = pl.pallas_call(
        _attn_kernel,
        grid=(b,),
        in_specs=[smem,
                  pl.BlockSpec((1, 512, s), lambda bi: (bi, 0, 0)),
                  pl.BlockSpec((1, s, N_HEADS * K_AUG), lambda bi: (bi, 0, 0)),
                  pl.BlockSpec((1, N_HEADS * V_AUG, s), lambda bi: (bi, 0, 0)),
                  _resident((1, V_DIM))],
        out_specs=pl.BlockSpec((1, s, ATTN_W), lambda bi: (bi, 0, 0)),
        out_shape=jax.ShapeDtypeStruct((b, s, ATTN_W), BF16),
        scratch_shapes=[pltpu.VMEM((N_HEADS, K_AUG, 4 * blk), BF16),
                        pltpu.VMEM((blk, 2 * blk), F32),
                        pltpu.VMEM((N_HEADS, V_AUG, 4 * blk), F32),
                        pltpu.VMEM((2, N_HEADS, blk, 4 * blk), F32)],
        compiler_params=pltpu.CompilerParams(dimension_semantics=("arbitrary",),
                                             vmem_limit_bytes=VMEM_LIMIT),
        name="diff_attn",
    )(lam, q_t, k_n, v_t, sub_gain)

    out = pl.pallas_call(
        _back_kernel,
        grid=(b, nt),
        in_specs=[tok(d), tok(ATTN_W), tok(CONV_W), _resident((ATTN_W + CONV_W, d)),
                  _resident((1, d)), _resident((d, d_ff)), _resident((d, d_ff)),
                  _resident((d_ff, d)), _resident((1, d))],
        out_specs=tok(d),
        out_shape=jax.ShapeDtypeStruct((b, s, d), F32),
        scratch_shapes=[pltpu.VMEM((tm, d), F32),
                        pltpu.VMEM((tm, d), BF16),
                        pltpu.VMEM((tm, d_ff), BF16)],
        compiler_params=params,
        name="back",
    )(x1, attn, c_n, bf(w_out), row(ffn2_norm[0]), bf(ffn2_w_gate), bf(ffn2_w_up),
      bf(ffn2_w_down), row(final_norm[0]))
    return out
```

```python
import math

import jax
import jax.numpy as jnp
from jax import lax
from jax.experimental import pallas as pl
from jax.experimental.pallas import tpu as pltpu

F32 = jnp.float32
BF16 = jnp.bfloat16

N_HEADS = 4
QK_DIM = 64
V_DIM = 128
HEAD_COLS = 2 * QK_DIM
K_AUG = 2 * HEAD_COLS
N_BIAS_COLS = 3
V_PAD = 16
V_AUG = V_DIM + V_PAD
ATTN_W = N_HEADS * V_DIM
CONV_W = 512
CONV_GROUP = 64
MXU_W = 256
GMAT_W = MXU_W
CONV_K = 3
NORM_EPS = 1e-6
FFN_RES_W = 0.5
LAM_INIT = 0.8 - 0.6 * math.exp(-0.3 * 0)
NEG_BIG = -1e30

FFN_CHUNK = 256
TOKEN_TILE = 512
SUB_TILE = 256
NORM_ROWS = 32
ATT_BLOCK = 256
CARRY_ROWS = 8
VMEM_LIMIT = 60000 * 1024


def _rms(x, gain):
    ms = jnp.mean(x * x, axis=-1, keepdims=True)
    return x * lax.rsqrt(ms + NORM_EPS) * gain


def _group_rms_scale(t, gmat_ref, group):
    t2 = (t * t).astype(BF16)
    w = gmat_ref.shape[0]
    ss = jnp.concatenate(
        [jnp.dot(t2[:, c:c + w], gmat_ref[...], preferred_element_type=F32)
         for c in range(0, t.shape[1], w)], axis=1)
    return lax.rsqrt(ss * (1.0 / group) + NORM_EPS)


def _norm_jobs(src_ref, gain_ref, dst_ref, r0, dtype):
    def job(g):
        rows = slice(r0 + g * NORM_ROWS, r0 + (g + 1) * NORM_ROWS)
        dst_ref[rows, :] = _rms(src_ref[rows, :], gain_ref[...]).astype(dtype)
    return [lambda g=g: job(g) for g in range(SUB_TILE // NORM_ROWS)]


def _run(jobs):
    for job in jobs:
        job()


def _gate_up(h_buf, wg_ref, wu_ref, a_buf, r0, side_jobs):
    rows = slice(r0, r0 + SUB_TILE)
    side_jobs = list(side_jobs)
    n_chunks = wg_ref.shape[1] // FFN_CHUNK
    done = 0
    for c in range(n_chunks):
        cols = slice(c * FFN_CHUNK, (c + 1) * FFN_CHUNK)
        h = h_buf[rows, :]
        g = jnp.dot(h, wg_ref[:, cols], preferred_element_type=F32)
        u = jnp.dot(h, wu_ref[:, cols], preferred_element_type=F32)
        a_buf[rows, cols] = (g * jax.nn.sigmoid(g) * u).astype(BF16)
        upto = -(-(c + 1) * len(side_jobs) // n_chunks)
        _run(side_jobs[done:upto])
        done = upto


def _front_kernel(x_ref, bias_ref, g1_ref, wg_ref, wu_ref, wd_ref, g2_ref, win_ref,
                  gmat_ref, qg_ref, kg_ref, cw_ref, cg_ref,
                  x1_ref, qt_ref, k_ref, vt_ref, c_ref,
                  h_buf, a_buf, h2_buf, u_buf, gb_buf):
    tm = x_ref.shape[1]
    x_rows, x1_rows = x_ref.at[0], x1_ref.at[0]
    sub_tiles = list(range(0, tm, SUB_TILE))

    @pl.when(pl.program_id(1) == 0)
    def _():
        u_buf[0:CARRY_ROWS, :] = jnp.zeros((CARRY_ROWS, CONV_W), F32)

    def down(r0):
        rows = slice(r0, r0 + SUB_TILE)
        x1_ref[0, rows, :] = x_ref[0, rows, :] + FFN_RES_W * jnp.dot(
            a_buf[rows, :], wd_ref[...], preferred_element_type=F32)

    def proj(r0, j):
        return jnp.dot(h2_buf[r0:r0 + SUB_TILE, :], win_ref[:, j * 512:(j + 1) * 512],
                       preferred_element_type=F32)

    def q_job(r0):
        q = proj(r0, 0)
        q = q * _group_rms_scale(q, gmat_ref, QK_DIM) * qg_ref[...]
        qt_ref[0, :, r0:r0 + SUB_TILE] = q.T.astype(BF16)

    def k_job(r0):
        rows = slice(r0, r0 + SUB_TILE)
        k = proj(r0, 1)
        k = (k * _group_rms_scale(k, gmat_ref, QK_DIM) * kg_ref[...]).astype(BF16)
        for hd in range(N_HEADS):
            k_ref[0, rows, hd * K_AUG:hd * K_AUG + HEAD_COLS] = (
                k[:, hd * HEAD_COLS:(hd + 1) * HEAD_COLS])
            k_ref[0, rows, hd * K_AUG + HEAD_COLS:(hd + 1) * K_AUG] = (
                bias_ref[rows, hd * HEAD_COLS:(hd + 1) * HEAD_COLS])

    def v_job(r0):
        rows = slice(r0, r0 + SUB_TILE)
        vt = proj(r0, 2).T.astype(BF16)
        ones_rows = jnp.where(lax.broadcasted_iota(jnp.int32, (V_PAD, SUB_TILE), 0) == 0, 1.0, 0.0)
        for hd in range(N_HEADS):
            vt_ref[0, hd * V_AUG:hd * V_AUG + V_DIM, rows] = vt[hd * V_DIM:(hd + 1) * V_DIM, :]
            vt_ref[0, hd * V_AUG + V_DIM:(hd + 1) * V_AUG, rows] = ones_rows.astype(BF16)

    def conv_in_job(r0):
        gb_buf[r0:r0 + SUB_TILE, :] = proj(r0, 3)
        u0 = CARRY_ROWS + r0
        u_buf[u0:u0 + SUB_TILE, :] = proj(r0, 4) * proj(r0, 5)

    def conv_out_job(r0):
        u0 = CARRY_ROWS + r0
        y = (cw_ref[0:1, :] * u_buf[u0 - 2:u0 - 2 + SUB_TILE, :]
             + cw_ref[1:2, :] * u_buf[u0 - 1:u0 - 1 + SUB_TILE, :]
             + cw_ref[2:3, :] * u_buf[u0:u0 + SUB_TILE, :])
        c = gb_buf[r0:r0 + SUB_TILE, :] * y
        c = c * _group_rms_scale(c, gmat_ref, CONV_GROUP) * cg_ref[...]
        c_ref[0, r0:r0 + SUB_TILE, :] = c.astype(BF16)

    def mixer_jobs(r0):
        return [lambda: q_job(r0), lambda: k_job(r0), lambda: v_job(r0),
                lambda: conv_in_job(r0), lambda: conv_out_job(r0)]

    _run(_norm_jobs(x_rows, g1_ref, h_buf, sub_tiles[0], BF16))
    pending = []
    for i, r0 in enumerate(sub_tiles):
        nxt = sub_tiles[i + 1] if i + 1 < len(sub_tiles) else None
        side = (_norm_jobs(x_rows, g1_ref, h_buf, nxt, BF16) if nxt is not None else []) + pending
        _gate_up(h_buf, wg_ref, wu_ref, a_buf, r0, side)
        down(r0)
        pending = _norm_jobs(x1_rows, g2_ref, h2_buf, r0, BF16) + mixer_jobs(r0)
    _run(pending)

    u_buf[0:CARRY_ROWS, :] = u_buf[tm:tm + CARRY_ROWS, :]


def _attn_kernel(lam_ref, qt_ref, k_ref, vt_ref, sg_ref, o_ref,
                 rhs_buf, mask_buf, acc_buf, s_buf):
    blk = ATT_BLOCK
    half = 2 * blk
    seq = k_ref.shape[1]
    lam = lam_ref[0]
    heads = range(N_HEADS)
    all_cols, first, second = slice(0, 2 * half), slice(0, half), slice(half, 2 * half)

    row = lax.broadcasted_iota(jnp.int32, (blk, half), 0)
    col = lax.broadcasted_iota(jnp.int32, (blk, half), 1)
    col = jnp.where(col >= blk, col - blk, col)
    mask_buf[...] = jnp.where(row <= col, 0.0, NEG_BIG)

    rhs_buf[...] = jnp.zeros(rhs_buf.shape, BF16)
    for par in range(2):
        for h in heads:
            rhs_buf[par, h, HEAD_COLS:HEAD_COLS + 16, :] = jnp.where(
                lax.broadcasted_iota(jnp.int32, (16, 2 * half), 0) < N_BIAS_COLS, 1.0, 0.0
            ).astype(BF16)

    def block_start(kb):
        return kb * blk if isinstance(kb, int) else pl.multiple_of(kb * blk, blk)

    def build_rhs(qb, par, h):
        q0 = block_start(2 * qb)
        r0 = h * HEAD_COLS
        for part in range(2):
            qs = pl.ds(q0 + part * blk, blk)
            c0 = part * half
            rhs_buf[par, h, 0:QK_DIM, c0:c0 + blk] = qt_ref[0, r0:r0 + QK_DIM, qs]
            rhs_buf[par, h, QK_DIM:HEAD_COLS, c0 + blk:c0 + half] = (
                qt_ref[0, r0 + QK_DIM:r0 + HEAD_COLS, qs])

    def scores(kb, slot, par, h, cols, want_max=True):
        s = jnp.dot(k_ref[0, pl.ds(block_start(kb), blk), h * K_AUG:(h + 1) * K_AUG],
                    rhs_buf[par, h, :, cols], preferred_element_type=F32)
        s_buf[slot, h, :, cols] = s
        return jnp.max(s, axis=0, keepdims=True) if want_max else None

    def softmax_pv(kb, slot, h, cols, m, tile_max, masked):
        t = s_buf[slot, h, :, cols]
        if masked:
            t = t + mask_buf[...]
            tile_max = jnp.max(t, axis=0, keepdims=True)
        m_new = jnp.maximum(m, tile_max)
        alpha = jnp.exp2(m - m_new)
        p = jnp.exp2(t - m_new).astype(BF16)
        pv = jnp.dot(vt_ref[0, h * V_AUG:(h + 1) * V_AUG, pl.ds(block_start(kb), blk)], p,
                     preferred_element_type=F32)
        acc_buf[h, :, cols] = acc_buf[h, :, cols] * alpha + pv
        return m_new

    def finalize(qb, h):
        inv_l = 1.0 / acc_buf[h, V_DIM:V_DIM + 1, :]
        for part in range(2):
            c0 = part * half
            a = (acc_buf[h, 0:V_DIM, c0:c0 + blk] * inv_l[:, c0:c0 + blk]
                 - acc_buf[h, 0:V_DIM, c0 + blk:c0 + half]
                 * (lam * inv_l[:, c0 + blk:c0 + half]))
            ms_a = jnp.mean(a * a, axis=0, keepdims=True)
            a = a * lax.rsqrt(ms_a + NORM_EPS)
            o_ref[0, pl.ds(block_start(2 * qb + part), blk), h * V_DIM:(h + 1) * V_DIM] = (
                (a.T * sg_ref[...]).astype(BF16))

    def q_block(qb, tms):
        par = qb & 1
        acc_buf[...] = jnp.zeros(acc_buf.shape, F32)
        ms = (jnp.full((1, 2 * half), NEG_BIG, F32),) * N_HEADS

        def full_step(kb, slot, carry):
            ms, tms = carry
            new_ms, new_tms = [], []
            for h in heads:
                new_tms.append(scores(kb + 1, 1 - slot, par, h, all_cols))
                new_ms.append(softmax_pv(kb, slot, h, all_cols, ms[h], tms[h], masked=False))
            return tuple(new_ms), tuple(new_tms)

        def pair(kp, c):
            return full_step(2 * kp + 1, 1, full_step(2 * kp, 0, c))

        ms, tms = lax.fori_loop(0, qb, pair, (ms, tms))

        for h in heads:
            scores(2 * qb + 1, 1, par, h, second, want_max=False)
            softmax_pv(2 * qb, 0, h, first, ms[h][:, first], None, masked=True)
            m_b = softmax_pv(2 * qb, 0, h, second, ms[h][:, second], tms[h][:, second],
                             masked=False)
            softmax_pv(2 * qb + 1, 1, h, second, m_b, None, masked=True)

        nxt = jnp.minimum(qb + 1, n_qb - 1)
        new_tms = []
        for h in heads:
            build_rhs(nxt, 1 - par, h)
            new_tms.append(scores(0, 0, 1 - par, h, all_cols))
            finalize(qb, h)
        return tuple(new_tms)

    n_qb = seq // (2 * blk)
    for h in heads:
        build_rhs(0, 0, h)
    lax.fori_loop(0, n_qb, q_block, tuple(scores(0, 0, 0, h, all_cols) for h in heads))


def _back_kernel(x1_ref, a_ref, c_ref, wo_ref, g3_ref, wg_ref, wu_ref, wd_ref, gf_ref,
                 out_ref, x2_buf, h_buf, a_buf):
    tm = x1_ref.shape[1]
    out_rows = out_ref.at[0]
    sub_tiles = list(range(0, tm, SUB_TILE))

    def out_proj(r0):
        rows = slice(r0, r0 + SUB_TILE)
        x2_buf[rows, :] = (
            x1_ref[0, rows, :]
            + jnp.dot(a_ref[0, rows, :], wo_ref[0:ATTN_W, :], preferred_element_type=F32)
            + jnp.dot(c_ref[0, rows, :], wo_ref[ATTN_W:ATTN_W + CONV_W, :],
                      preferred_element_type=F32))

    def down(r0):
        rows = slice(r0, r0 + SUB_TILE)
        out_ref[0, rows, :] = x2_buf[rows, :] + FFN_RES_W * jnp.dot(
            a_buf[rows, :], wd_ref[...], preferred_element_type=F32)

    out_proj(sub_tiles[0])
    _run(_norm_jobs(x2_buf, g3_ref, h_buf, sub_tiles[0], BF16))
    pending = []
    for i, r0 in enumerate(sub_tiles):
        nxt = sub_tiles[i + 1] if i + 1 < len(sub_tiles) else None
        side = list(pending)
        if nxt is not None:
            side = [lambda nxt=nxt: out_proj(nxt)] + side + _norm_jobs(x2_buf, g3_ref, h_buf, nxt, BF16)
        _gate_up(h_buf, wg_ref, wu_ref, a_buf, r0, side)
        down(r0)
        pending = _norm_jobs(out_rows, gf_ref, out_rows, r0, F32)
    _run(pending)


def _resident(shape):
    return pl.BlockSpec(shape, lambda *_: (0,) * len(shape), pipeline_mode=pl.Buffered(1))


def kernel(x, ffn1_norm, ffn1_w_gate, ffn1_w_up, ffn1_w_down, mix_norm, w_in, q_norm, k_norm,
           lambda_q1, lambda_k1, lambda_q2, lambda_k2, attn_subln, conv_w, conv_norm, w_out,
           ffn2_norm, ffn2_w_gate, ffn2_w_up, ffn2_w_down, final_norm):
    b, s, d = x.shape
    d_ff = ffn1_w_gate.shape[-1]
    tm = TOKEN_TILE
    nt = s // tm
    assert s % tm == 0 and tm % SUB_TILE == 0 and d_ff % FFN_CHUNK == 0
    assert s % (2 * ATT_BLOCK) == 0
    assert ffn1_w_gate.shape[0] == 1, "single layer"

    row = lambda v: v.reshape(1, -1).astype(F32)
    bf = lambda w: w[0].astype(BF16)
    n_groups = 512 // QK_DIM
    gid = jnp.arange(GMAT_W) // QK_DIM
    gmat = (gid[:, None] == gid[None, :]).astype(BF16)
    log2e = math.log2(math.e)
    q_gain = jnp.tile(q_norm[0].astype(F32), n_groups).reshape(1, 512) * (QK_DIM ** -0.5 * log2e)
    k_gain = jnp.tile(k_norm[0].astype(F32), n_groups).reshape(1, 512)
    slopes = jnp.exp2(-8.0 * jnp.arange(1, N_HEADS + 1, dtype=F32) / N_HEADS) * log2e
    trunc = lambda v: lax.bitcast_convert_type(
        lax.bitcast_convert_type(v, jnp.uint32) & jnp.uint32(0xFFFF0000), F32)
    rest = slopes[None, :] * jnp.arange(s, dtype=F32)[:, None]
    pieces = []
    for _ in range(N_BIAS_COLS):
        pieces.append(trunc(rest))
        rest = rest - pieces[-1]
    bias_cols = jnp.stack(pieces, axis=-1)
    bias_cols = jnp.pad(bias_cols, ((0, 0), (0, 0), (0, HEAD_COLS - N_BIAS_COLS)))
    bias_cols = bias_cols.reshape(s, N_HEADS * HEAD_COLS).astype(BF16)
    lam = (jnp.exp(jnp.sum(lambda_q1[0].astype(F32) * lambda_k1[0].astype(F32)))
           - jnp.exp(jnp.sum(lambda_q2[0].astype(F32) * lambda_k2[0].astype(F32)))
           + LAM_INIT).reshape(1)
    sub_gain = row(attn_subln[0]) * (1.0 - LAM_INIT)

    tok = lambda w: pl.BlockSpec((1, tm, w), lambda bi, ti: (bi, ti, 0))
    tok_t = lambda r: pl.BlockSpec((1, r, tm), lambda bi, ti: (bi, 0, ti))
    smem = pl.BlockSpec(memory_space=pltpu.SMEM)
    params = pltpu.CompilerParams(dimension_semantics=("arbitrary", "arbitrary"),
                                  vmem_limit_bytes=VMEM_LIMIT)

    x1, q_t, k_n, v_t, c_n = pl.pallas_call(
        _front_kernel,
        grid=(b, nt),
        in_specs=[tok(d), pl.BlockSpec((tm, N_HEADS * HEAD_COLS), lambda bi, ti: (ti, 0)),
                  _resident((1, d)), _resident((d, d_ff)), _resident((d, d_ff)),
                  _resident((d_ff, d)), _resident((1, d)), _resident((d, 6 * 512)),
                  _resident((GMAT_W, GMAT_W)), _resident((1, 512)), _resident((1, 512)),
                  _resident((CONV_K, CONV_W)), _resident((1, CONV_W))],
        out_specs=[tok(d), tok_t(512), tok(N_HEADS * K_AUG), tok_t(N_HEADS * V_AUG), tok(CONV_W)],
        out_shape=[jax.ShapeDtypeStruct((b, s, d), F32),
                   jax.ShapeDtypeStruct((b, 512, s), BF16),
                   jax.ShapeDtypeStruct((b, s, N_HEADS * K_AUG), BF16),
                   jax.ShapeDtypeStruct((b, N_HEADS * V_AUG, s), BF16),
                   jax.ShapeDtypeStruct((b, s, CONV_W), BF16)],
        scratch_shapes=[pltpu.VMEM((tm, d), BF16),
                        pltpu.VMEM((tm, d_ff), BF16),
                        pltpu.VMEM((tm, d), BF16),
                        pltpu.VMEM((tm + CARRY_ROWS, CONV_W), F32),
                        pltpu.VMEM((tm, CONV_W), F32)],
        compiler_params=params,
        name="front",
    )(x, bias_cols, row(ffn1_norm[0]), bf(ffn1_w_gate), bf(ffn1_w_up), bf(ffn1_w_down),
      row(mix_norm[0]), bf(w_in), gmat, q_gain, k_gain,
      conv_w[0].astype(F32), row(conv_norm[0]))

    blk = ATT_BLOCK
    attn = pl.pallas_call(
        _attn_kernel,
        grid=(b,),
        in_specs=[smem,
                  pl.BlockSpec((1, 512, s), lambda bi: (bi, 0, 0)),
                  pl.BlockSpec((1, s, N_HEADS * K_AUG), lambda bi: (bi, 0, 0)),
                  pl.BlockSpec((1, N_HEADS * V_AUG, s), lambda bi: (bi, 0, 0)),
                  _resident((1, V_DIM))],
        out_specs=pl.BlockSpec((1, s, ATTN_W), lambda bi: (bi, 0, 0)),
        out_shape=jax.ShapeDtypeStruct((b, s, ATTN_W), BF16),
        scratch_shapes=[pltpu.VMEM((2, N_HEADS, K_AUG, 4 * blk), BF16),
                        pltpu.VMEM((blk, 2 * blk), F32),
                        pltpu.VMEM((N_HEADS, V_AUG, 4 * blk), F32),
                        pltpu.VMEM((2, N_HEADS, blk, 4 * blk), F32)],
        compiler_params=pltpu.CompilerParams(dimension_semantics=("arbitrary",),
                                             vmem_limit_bytes=VMEM_LIMIT),
        name="diff_attn",
    )(lam, q_t, k_n, v_t, sub_gain)

    out = pl.pallas_call(
        _back_kernel,
        grid=(b, nt),
        in_specs=[tok(d), tok(ATTN_W), tok(CONV_W), _resident((ATTN_W + CONV_W, d)),
                  _resident((1, d)), _resident((d, d_ff)), _resident((d, d_ff)),
                  _resident((d_ff, d)), _resident((1, d))],
        out_specs=tok(d),
        out_shape=jax.ShapeDtypeStruct((b, s, d), F32),
        scratch_shapes=[pltpu.VMEM((tm, d), F32),
                        pltpu.VMEM((tm, d), BF16),
                        pltpu.VMEM((tm, d_ff), BF16)],
        compiler_params=params,
        name="back",
    )(x1, attn, c_n, bf(w_out), row(ffn2_norm[0]), bf(ffn2_w_gate), bf(ffn2_w_up),
      bf(ffn2_w_down), row(final_norm[0]))
    return out
```

```python
import math

import jax
import jax.numpy as jnp
from jax import lax
from jax.experimental import pallas as pl
from jax.experimental.pallas import tpu as pltpu

F32 = jnp.float32
BF16 = jnp.bfloat16

N_HEADS = 4
QK_DIM = 64
V_DIM = 128
HEAD_COLS = 2 * QK_DIM
K_AUG = 2 * HEAD_COLS
N_BIAS_COLS = 3
V_PAD = 16
V_AUG = V_DIM + V_PAD
ATTN_W = N_HEADS * V_DIM
CONV_W = 512
CONV_GROUP = 64
MXU_W = 256
GMAT_W = MXU_W
CONV_K = 3
NORM_EPS = 1e-6
FFN_RES_W = 0.5
LAM_INIT = 0.8 - 0.6 * math.exp(-0.3 * 0)
NEG_BIG = -1e30

FFN_CHUNK = 256
TOKEN_TILE = 512
BACK_TILE = 1024
SUB_TILE = 256
NORM_ROWS = 32
ATT_BLOCK = 256
CARRY_ROWS = 8
VMEM_LIMIT = 60000 * 1024


def _rms(x, gain):
    ms = jnp.mean(x * x, axis=-1, keepdims=True)
    return x * lax.rsqrt(ms + NORM_EPS) * gain


def _group_rms_scale(t, gmat_ref, group):
    t2 = (t * t).astype(BF16)
    w = gmat_ref.shape[0]
    ss = jnp.concatenate(
        [jnp.dot(t2[:, c:c + w], gmat_ref[...], preferred_element_type=F32)
         for c in range(0, t.shape[1], w)], axis=1)
    return lax.rsqrt(ss * (1.0 / group) + NORM_EPS)


def _norm_jobs(src_ref, gain_ref, dst_ref, r0, dtype):
    def job(g):
        rows = slice(r0 + g * NORM_ROWS, r0 + (g + 1) * NORM_ROWS)
        dst_ref[rows, :] = _rms(src_ref[rows, :], gain_ref[...]).astype(dtype)
    return [lambda g=g: job(g) for g in range(SUB_TILE // NORM_ROWS)]


def _run(jobs):
    for job in jobs:
        job()


def _gate_up(h_buf, wg_ref, wu_ref, a_buf, r0, side_jobs):
    rows = slice(r0, r0 + SUB_TILE)
    side_jobs = list(side_jobs)
    n_chunks = wg_ref.shape[1] // FFN_CHUNK
    done = 0
    for c in range(n_chunks):
        cols = slice(c * FFN_CHUNK, (c + 1) * FFN_CHUNK)
        h = h_buf[rows, :]
        g = jnp.dot(h, wg_ref[:, cols], preferred_element_type=F32)
        u = jnp.dot(h, wu_ref[:, cols], preferred_element_type=F32)
        a_buf[rows, cols] = (g * jax.nn.sigmoid(g) * u).astype(BF16)
        upto = -(-(c + 1) * len(side_jobs) // n_chunks)
        _run(side_jobs[done:upto])
        done = upto


def _front_kernel(x_ref, bias_ref, g1_ref, wg_ref, wu_ref, wd_ref, g2_ref, win_ref,
                  gmat_ref, qg_ref, kg_ref, cw_ref, cg_ref,
                  x1_ref, qt_ref, k_ref, vt_ref, c_ref,
                  h_buf, a_buf, h2_buf, u_buf, gb_buf):
    tm = x_ref.shape[1]
    x_rows, x1_rows = x_ref.at[0], x1_ref.at[0]
    sub_tiles = list(range(0, tm, SUB_TILE))

    @pl.when(pl.program_id(1) == 0)
    def _():
        u_buf[0:CARRY_ROWS, :] = jnp.zeros((CARRY_ROWS, CONV_W), F32)

    def down(r0):
        rows = slice(r0, r0 + SUB_TILE)
        x1_ref[0, rows, :] = x_ref[0, rows, :] + FFN_RES_W * jnp.dot(
            a_buf[rows, :], wd_ref[...], preferred_element_type=F32)

    def proj(r0, j):
        return jnp.dot(h2_buf[r0:r0 + SUB_TILE, :], win_ref[:, j * 512:(j + 1) * 512],
                       preferred_element_type=F32)

    def q_job(r0):
        q = proj(r0, 0)
        q = q * _group_rms_scale(q, gmat_ref, QK_DIM) * qg_ref[...]
        qt_ref[0, :, r0:r0 + SUB_TILE] = q.T.astype(BF16)

    def k_job(r0):
        rows = slice(r0, r0 + SUB_TILE)
        k = proj(r0, 1)
        k = (k * _group_rms_scale(k, gmat_ref, QK_DIM) * kg_ref[...]).astype(BF16)
        for hd in range(N_HEADS):
            k_ref[0, rows, hd * K_AUG:hd * K_AUG + HEAD_COLS] = (
                k[:, hd * HEAD_COLS:(hd + 1) * HEAD_COLS])
            k_ref[0, rows, hd * K_AUG + HEAD_COLS:(hd + 1) * K_AUG] = (
                bias_ref[rows, hd * HEAD_COLS:(hd + 1) * HEAD_COLS])

    def v_job(r0):
        rows = slice(r0, r0 + SUB_TILE)
        vt = proj(r0, 2).T.astype(BF16)
        ones_rows = jnp.where(lax.broadcasted_iota(jnp.int32, (V_PAD, SUB_TILE), 0) == 0, 1.0, 0.0)
        for hd in range(N_HEADS):
            vt_ref[0, hd * V_AUG:hd * V_AUG + V_DIM, rows] = vt[hd * V_DIM:(hd + 1) * V_DIM, :]
            vt_ref[0, hd * V_AUG + V_DIM:(hd + 1) * V_AUG, rows] = ones_rows.astype(BF16)

    def conv_in_job(r0):
        gb_buf[r0:r0 + SUB_TILE, :] = proj(r0, 3)
        u0 = CARRY_ROWS + r0
        u_buf[u0:u0 + SUB_TILE, :] = proj(r0, 4) * proj(r0, 5)

    def conv_out_job(r0):
        u0 = CARRY_ROWS + r0
        y = (cw_ref[0:1, :] * u_buf[u0 - 2:u0 - 2 + SUB_TILE, :]
             + cw_ref[1:2, :] * u_buf[u0 - 1:u0 - 1 + SUB_TILE, :]
             + cw_ref[2:3, :] * u_buf[u0:u0 + SUB_TILE, :])
        c = gb_buf[r0:r0 + SUB_TILE, :] * y
        c = c * _group_rms_scale(c, gmat_ref, CONV_GROUP) * cg_ref[...]
        c_ref[0, r0:r0 + SUB_TILE, :] = c.astype(BF16)

    def mixer_jobs(r0):
        return [lambda: q_job(r0), lambda: k_job(r0), lambda: v_job(r0),
                lambda: conv_in_job(r0), lambda: conv_out_job(r0)]

    _run(_norm_jobs(x_rows, g1_ref, h_buf, sub_tiles[0], BF16))
    pending = []
    for i, r0 in enumerate(sub_tiles):
        nxt = sub_tiles[i + 1] if i + 1 < len(sub_tiles) else None
        side = (_norm_jobs(x_rows, g1_ref, h_buf, nxt, BF16) if nxt is not None else []) + pending
        _gate_up(h_buf, wg_ref, wu_ref, a_buf, r0, side)
        down(r0)
        pending = _norm_jobs(x1_rows, g2_ref, h2_buf, r0, BF16) + mixer_jobs(r0)
    _run(pending)

    u_buf[0:CARRY_ROWS, :] = u_buf[tm:tm + CARRY_ROWS, :]


def _attn_kernel(lam_ref, qt_ref, k_ref, vt_ref, sg_ref, o_ref,
                 rhs_buf, mask_buf, acc_buf, s_buf):
    blk = ATT_BLOCK
    half = 2 * blk
    seq = k_ref.shape[1]
    lam = lam_ref[0]
    heads = range(N_HEADS)
    all_cols, first, second = slice(0, 2 * half), slice(0, half), slice(half, 2 * half)

    row = lax.broadcasted_iota(jnp.int32, (blk, half), 0)
    col = lax.broadcasted_iota(jnp.int32, (blk, half), 1)
    col = jnp.where(col >= blk, col - blk, col)
    mask_buf[...] = jnp.where(row <= col, 0.0, NEG_BIG)

    rhs_buf[...] = jnp.zeros(rhs_buf.shape, BF16)
    for par in range(2):
        for h in heads:
            rhs_buf[par, h, HEAD_COLS:HEAD_COLS + 16, :] = jnp.where(
                lax.broadcasted_iota(jnp.int32, (16, 2 * half), 0) < N_BIAS_COLS, 1.0, 0.0
            ).astype(BF16)

    def block_start(kb):
        return kb * blk if isinstance(kb, int) else pl.multiple_of(kb * blk, blk)

    def build_rhs(qb, par, h):
        q0 = block_start(2 * qb)
        r0 = h * HEAD_COLS
        for part in range(2):
            qs = pl.ds(q0 + part * blk, blk)
            c0 = part * half
            rhs_buf[par, h, 0:QK_DIM, c0:c0 + blk] = qt_ref[0, r0:r0 + QK_DIM, qs]
            rhs_buf[par, h, QK_DIM:HEAD_COLS, c0 + blk:c0 + half] = (
                qt_ref[0, r0 + QK_DIM:r0 + HEAD_COLS, qs])

    def scores(kb, slot, par, h, cols, want_max=True):
        s = jnp.dot(k_ref[0, pl.ds(block_start(kb), blk), h * K_AUG:(h + 1) * K_AUG],
                    rhs_buf[par, h, :, cols], preferred_element_type=F32)
        s_buf[slot, h, :, cols] = s
        return jnp.max(s, axis=0, keepdims=True) if want_max else None

    def softmax_pv(kb, slot, h, cols, m, tile_max, masked):
        t = s_buf[slot, h, :, cols]
        if masked:
            t = t + mask_buf[...]
            tile_max = jnp.max(t, axis=0, keepdims=True)
        m_new = jnp.maximum(m, tile_max)
        alpha = jnp.exp2(m - m_new)
        p = jnp.exp2(t - m_new).astype(BF16)
        pv = jnp.dot(vt_ref[0, h * V_AUG:(h + 1) * V_AUG, pl.ds(block_start(kb), blk)], p,
                     preferred_element_type=F32)
        acc_buf[h, :, cols] = acc_buf[h, :, cols] * alpha + pv
        return m_new

    def finalize(qb, h):
        inv_l = 1.0 / acc_buf[h, V_DIM:V_DIM + 1, :]
        for part in range(2):
            c0 = part * half
            a = (acc_buf[h, 0:V_DIM, c0:c0 + blk] * inv_l[:, c0:c0 + blk]
                 - acc_buf[h, 0:V_DIM, c0 + blk:c0 + half]
                 * (lam * inv_l[:, c0 + blk:c0 + half]))
            ms_a = jnp.mean(a * a, axis=0, keepdims=True)
            a = a * lax.rsqrt(ms_a + NORM_EPS)
            o_ref[0, pl.ds(block_start(2 * qb + part), blk), h * V_DIM:(h + 1) * V_DIM] = (
                (a.T * sg_ref[...]).astype(BF16))

    def q_block(qb, tms):
        par = qb & 1
        acc_buf[...] = jnp.zeros(acc_buf.shape, F32)
        ms = (jnp.full((1, 2 * half), NEG_BIG, F32),) * N_HEADS

        def full_step(kb, slot, carry):
            ms, tms = carry
            new_ms, new_tms = [], []
            for h in heads:
                new_tms.append(scores(kb + 1, 1 - slot, par, h, all_cols))
                new_ms.append(softmax_pv(kb, slot, h, all_cols, ms[h], tms[h], masked=False))
            return tuple(new_ms), tuple(new_tms)

        def pair(kp, c):
            return full_step(2 * kp + 1, 1, full_step(2 * kp, 0, c))

        ms, tms = lax.fori_loop(0, qb, pair, (ms, tms))

        for h in heads:
            scores(2 * qb + 1, 1, par, h, second, want_max=False)
            softmax_pv(2 * qb, 0, h, first, ms[h][:, first], None, masked=True)
            m_b = softmax_pv(2 * qb, 0, h, second, ms[h][:, second], tms[h][:, second],
                             masked=False)
            softmax_pv(2 * qb + 1, 1, h, second, m_b, None, masked=True)

        nxt = jnp.minimum(qb + 1, n_qb - 1)
        new_tms = []
        for h in heads:
            build_rhs(nxt, 1 - par, h)
            new_tms.append(scores(0, 0, 1 - par, h, all_cols))
            finalize(qb, h)
        return tuple(new_tms)

    n_qb = seq // (2 * blk)
    for h in heads:
        build_rhs(0, 0, h)
    lax.fori_loop(0, n_qb, q_block, tuple(scores(0, 0, 0, h, all_cols) for h in heads))


def _back_kernel(x1_ref, a_ref, c_ref, wo_ref, g3_ref, wg_ref, wu_ref, wd_ref, gf_ref,
                 out_ref, x2_buf, h_buf, a_buf):
    tm = x1_ref.shape[1]
    out_rows = out_ref.at[0]
    sub_tiles = list(range(0, tm, SUB_TILE))

    def out_proj(r0):
        rows = slice(r0, r0 + SUB_TILE)
        x2_buf[rows, :] = (
            x1_ref[0, rows, :]
            + jnp.dot(a_ref[0, rows, :], wo_ref[0:ATTN_W, :], preferred_element_type=F32)
            + jnp.dot(c_ref[0, rows, :], wo_ref[ATTN_W:ATTN_W + CONV_W, :],
                      preferred_element_type=F32))

    def down(r0):
        rows = slice(r0, r0 + SUB_TILE)
        out_ref[0, rows, :] = x2_buf[rows, :] + FFN_RES_W * jnp.dot(
            a_buf[rows, :], wd_ref[...], preferred_element_type=F32)

    out_proj(sub_tiles[0])
    _run(_norm_jobs(x2_buf, g3_ref, h_buf, sub_tiles[0], BF16))
    pending = []
    for i, r0 in enumerate(sub_tiles):
        nxt = sub_tiles[i + 1] if i + 1 < len(sub_tiles) else None
        side = list(pending)
        if nxt is not None:
            side = [lambda nxt=nxt: out_proj(nxt)] + side + _norm_jobs(x2_buf, g3_ref, h_buf, nxt, BF16)
        _gate_up(h_buf, wg_ref, wu_ref, a_buf, r0, side)
        down(r0)
        pending = _norm_jobs(out_rows, gf_ref, out_rows, r0, F32)
    _run(pending)


def _resident(shape):
    return pl.BlockSpec(shape, lambda *_: (0,) * len(shape), pipeline_mode=pl.Buffered(1))


def kernel(x, ffn1_norm, ffn1_w_gate, ffn1_w_up, ffn1_w_down, mix_norm, w_in, q_norm, k_norm,
           lambda_q1, lambda_k1, lambda_q2, lambda_k2, attn_subln, conv_w, conv_norm, w_out,
           ffn2_norm, ffn2_w_gate, ffn2_w_up, ffn2_w_down, final_norm):
    b, s, d = x.shape
    d_ff = ffn1_w_gate.shape[-1]
    tm = TOKEN_TILE
    nt = s // tm
    assert s % tm == 0 and tm % SUB_TILE == 0 and d_ff % FFN_CHUNK == 0
    assert s % BACK_TILE == 0 and BACK_TILE % SUB_TILE == 0
    assert s % (2 * ATT_BLOCK) == 0
    assert ffn1_w_gate.shape[0] == 1, "single layer"

    row = lambda v: v.reshape(1, -1).astype(F32)
    bf = lambda w: w[0].astype(BF16)
    n_groups = 512 // QK_DIM
    gid = jnp.arange(GMAT_W) // QK_DIM
    gmat = (gid[:, None] == gid[None, :]).astype(BF16)
    log2e = math.log2(math.e)
    q_gain = jnp.tile(q_norm[0].astype(F32), n_groups).reshape(1, 512) * (QK_DIM ** -0.5 * log2e)
    k_gain = jnp.tile(k_norm[0].astype(F32), n_groups).reshape(1, 512)
    slopes = jnp.exp2(-8.0 * jnp.arange(1, N_HEADS + 1, dtype=F32) / N_HEADS) * log2e
    trunc = lambda v: lax.bitcast_convert_type(
        lax.bitcast_convert_type(v, jnp.uint32) & jnp.uint32(0xFFFF0000), F32)
    rest = slopes[None, :] * jnp.arange(s, dtype=F32)[:, None]
    pieces = []
    for _ in range(N_BIAS_COLS):
        pieces.append(trunc(rest))
        rest = rest - pieces[-1]
    bias_cols = jnp.stack(pieces, axis=-1)
    bias_cols = jnp.pad(bias_cols, ((0, 0), (0, 0), (0, HEAD_COLS - N_BIAS_COLS)))
    bias_cols = bias_cols.reshape(s, N_HEADS * HEAD_COLS).astype(BF16)
    lam = (jnp.exp(jnp.sum(lambda_q1[0].astype(F32) * lambda_k1[0].astype(F32)))
           - jnp.exp(jnp.sum(lambda_q2[0].astype(F32) * lambda_k2[0].astype(F32)))
           + LAM_INIT).reshape(1)
    sub_gain = row(attn_subln[0]) * (1.0 - LAM_INIT)

    tok = lambda w: pl.BlockSpec((1, tm, w), lambda bi, ti: (bi, ti, 0))
    tok_t = lambda r: pl.BlockSpec((1, r, tm), lambda bi, ti: (bi, 0, ti))
    smem = pl.BlockSpec(memory_space=pltpu.SMEM)
    params = pltpu.CompilerParams(dimension_semantics=("arbitrary", "arbitrary"),
                                  vmem_limit_bytes=VMEM_LIMIT)

    x1, q_t, k_n, v_t, c_n = pl.pallas_call(
        _front_kernel,
        grid=(b, nt),
        in_specs=[tok(d), pl.BlockSpec((tm, N_HEADS * HEAD_COLS), lambda bi, ti: (ti, 0)),
                  _resident((1, d)), _resident((d, d_ff)), _resident((d, d_ff)),
                  _resident((d_ff, d)), _resident((1, d)), _resident((d, 6 * 512)),
                  _resident((GMAT_W, GMAT_W)), _resident((1, 512)), _resident((1, 512)),
                  _resident((CONV_K, CONV_W)), _resident((1, CONV_W))],
        out_specs=[tok(d), tok_t(512), tok(N_HEADS * K_AUG), tok_t(N_HEADS * V_AUG), tok(CONV_W)],
        out_shape=[jax.ShapeDtypeStruct((b, s, d), F32),
                   jax.ShapeDtypeStruct((b, 512, s), BF16),
                   jax.ShapeDtypeStruct((b, s, N_HEADS * K_AUG), BF16),
                   jax.ShapeDtypeStruct((b, N_HEADS * V_AUG, s), BF16),
                   jax.ShapeDtypeStruct((b, s, CONV_W), BF16)],
        scratch_shapes=[pltpu.VMEM((tm, d), BF16),
                        pltpu.VMEM((tm, d_ff), BF16),
                        pltpu.VMEM((tm, d), BF16),
                        pltpu.VMEM((tm + CARRY_ROWS, CONV_W), F32),
                        pltpu.VMEM((tm, CONV_W), F32)],
        compiler_params=params,
        name="front",
    )(x, bias_cols, row(ffn1_norm[0]), bf(ffn1_w_gate), bf(ffn1_w_up), bf(ffn1_w_down),
      row(mix_norm[0]), bf(w_in), gmat, q_gain, k_gain,
      conv_w[0].astype(F32), row(conv_norm[0]))

    blk = ATT_BLOCK
    attn = pl.pallas_call(
        _attn_kernel,
        grid=(b,),
        in_specs=[smem,
                  pl.BlockSpec((1, 512, s), lambda bi: (bi, 0, 0)),
                  pl.BlockSpec((1, s, N_HEADS * K_AUG), lambda bi: (bi, 0, 0)),
                  pl.BlockSpec((1, N_HEADS * V_AUG, s), lambda bi: (bi, 0, 0)),
                  _resident((1, V_DIM))],
        out_specs=pl.BlockSpec((1, s, ATTN_W), lambda bi: (bi, 0, 0)),
        out_shape=jax.ShapeDtypeStruct((b, s, ATTN_W), BF16),
        scratch_shapes=[pltpu.VMEM((2, N_HEADS, K_AUG, 4 * blk), BF16),
                        pltpu.VMEM((blk, 2 * blk), F32),
                        pltpu.VMEM((N_HEADS, V_AUG, 4 * blk), F32),
                        pltpu.VMEM((2, N_HEADS, blk, 4 * blk), F32)],
        compiler_params=pltpu.CompilerParams(dimension_semantics=("arbitrary",),
                                             vmem_limit_bytes=VMEM_LIMIT),
        name="diff_attn",
    )(lam, q_t, k_n, v_t, sub_gain)

    tb = BACK_TILE
    tok_b = lambda w: pl.BlockSpec((1, tb, w), lambda bi, ti: (bi, ti, 0))
    out = pl.pallas_call(
        _back_kernel,
        grid=(b, s // tb),
        in_specs=[tok_b(d), tok_b(ATTN_W), tok_b(CONV_W), _resident((ATTN_W + CONV_W, d)),
                  _resident((1, d)), _resident((d, d_ff)), _resident((d, d_ff)),
                  _resident((d_ff, d)), _resident((1, d))],
        out_specs=tok_b(d),
        out_shape=jax.ShapeDtypeStruct((b, s, d), F32),
        scratch_shapes=[pltpu.VMEM((tb, d), F32),
                        pltpu.VMEM((tb, d), BF16),
                        pltpu.VMEM((tb, d_ff), BF16)],
        compiler_params=params,
        name="back",
    )(x1, attn, c_n, bf(w_out), row(ffn2_norm[0]), bf(ffn2_w_gate), bf(ffn2_w_up),
      bf(ffn2_w_down), row(final_norm[0]))
    return out
```

```python
import math

import jax
import jax.numpy as jnp
import numpy as np
from jax import lax
from jax.experimental import pallas as pl
from jax.experimental.pallas import tpu as pltpu

F32 = jnp.float32
BF16 = jnp.bfloat16

N_HEADS = 4
QK_DIM = 64
V_DIM = 128
HEAD_COLS = 2 * QK_DIM
K_AUG = 2 * HEAD_COLS
N_BIAS_COLS = 3
V_PAD = 16
V_AUG = V_DIM + V_PAD
ATTN_W = N_HEADS * V_DIM
CONV_W = 512
CONV_GROUP = 64
MXU_W = 256
GMAT_W = MXU_W
CONV_K = 3
NORM_EPS = 1e-6
FFN_RES_W = 0.5
LAM_INIT = 0.8 - 0.6 * math.exp(-0.3 * 0)
NEG_BIG = -1e30

FFN_CHUNK = 256
TOKEN_TILE = 512
BACK_TILE = 1024
SUB_TILE = 256
NORM_ROWS = 32
ATT_BLOCK = 256
CARRY_ROWS = 8
CAST_ROWS = 16
VMEM_LIMIT = 60000 * 1024


def _rms(x, gain):
    ms = jnp.mean(x * x, axis=-1, keepdims=True)
    return x * lax.rsqrt(ms + NORM_EPS) * gain


def _group_rms_scale(t, gmat_ref, group):
    t2 = (t * t).astype(BF16)
    w = gmat_ref.shape[0]
    ss = jnp.concatenate(
        [jnp.dot(t2[:, c:c + w], gmat_ref[...], preferred_element_type=F32)
         for c in range(0, t.shape[1], w)], axis=1)
    return lax.rsqrt(ss * (1.0 / group) + NORM_EPS)


def _norm_jobs(src_ref, gain_ref, dst_ref, r0, dtype):
    def job(g):
        rows = slice(r0 + g * NORM_ROWS, r0 + (g + 1) * NORM_ROWS)
        dst_ref[rows, :] = _rms(src_ref[rows, :], gain_ref[...]).astype(dtype)
    return [lambda g=g: job(g) for g in range(SUB_TILE // NORM_ROWS)]


def _run(jobs):
    for job in jobs:
        job()


def _gate_up(h_buf, wg_ref, wu_ref, a_buf, r0, side_jobs):
    rows = slice(r0, r0 + SUB_TILE)
    side_jobs = list(side_jobs)
    n_chunks = wg_ref.shape[1] // FFN_CHUNK
    done = 0
    for c in range(n_chunks):
        cols = slice(c * FFN_CHUNK, (c + 1) * FFN_CHUNK)
        h = h_buf[rows, :]
        g = jnp.dot(h, wg_ref[:, cols], preferred_element_type=F32)
        u = jnp.dot(h, wu_ref[:, cols], preferred_element_type=F32)
        a_buf[rows, cols] = (g * jax.nn.sigmoid(g) * u).astype(BF16)
        upto = -(-(c + 1) * len(side_jobs) // n_chunks)
        _run(side_jobs[done:upto])
        done = upto


def _front_kernel(x_ref, bias_ref, g1_ref, wg_ref, wu_ref, wd_ref, g2_ref, win_ref,
                  gmat_ref, qg_ref, kg_ref, cw_ref, cg_ref,
                  later_w0, later_w1, later_w2, later_w3,
                  x1_ref, qt_ref, k_ref, vt_ref, c_ref,
                  later_o0, later_o1, later_o2, later_o3,
                  h_buf, a_buf, h2_buf, u_buf, gb_buf):
    tm = x_ref.shape[1]
    x_rows, x1_rows = x_ref.at[0], x1_ref.at[0]
    sub_tiles = list(range(0, tm, SUB_TILE))

    @pl.when(pl.program_id(1) == 0)
    def _():
        u_buf[0:CARRY_ROWS, :] = jnp.zeros((CARRY_ROWS, CONV_W), F32)

    def down(r0):
        rows = slice(r0, r0 + SUB_TILE)
        x1_ref[0, rows, :] = x_ref[0, rows, :] + FFN_RES_W * jnp.dot(
            a_buf[rows, :], wd_ref[...], preferred_element_type=F32)

    def proj(r0, j):
        return jnp.dot(h2_buf[r0:r0 + SUB_TILE, :], win_ref[:, j * 512:(j + 1) * 512],
                       preferred_element_type=F32)

    def q_job(r0):
        q = proj(r0, 0)
        q = q * _group_rms_scale(q, gmat_ref, QK_DIM) * qg_ref[...]
        qt_ref[0, :, r0:r0 + SUB_TILE] = q.T.astype(BF16)

    def k_job(r0):
        rows = slice(r0, r0 + SUB_TILE)
        k = proj(r0, 1)
        k = (k * _group_rms_scale(k, gmat_ref, QK_DIM) * kg_ref[...]).astype(BF16)
        for hd in range(N_HEADS):
            k_ref[0, rows, hd * K_AUG:hd * K_AUG + HEAD_COLS] = (
                k[:, hd * HEAD_COLS:(hd + 1) * HEAD_COLS])
            k_ref[0, rows, hd * K_AUG + HEAD_COLS:(hd + 1) * K_AUG] = (
                bias_ref[rows, hd * HEAD_COLS:(hd + 1) * HEAD_COLS])

    def v_job(r0):
        rows = slice(r0, r0 + SUB_TILE)
        vt = proj(r0, 2).T.astype(BF16)
        ones_rows = jnp.where(lax.broadcasted_iota(jnp.int32, (V_PAD, SUB_TILE), 0) == 0, 1.0, 0.0)
        for hd in range(N_HEADS):
            vt_ref[0, hd * V_AUG:hd * V_AUG + V_DIM, rows] = vt[hd * V_DIM:(hd + 1) * V_DIM, :]
            vt_ref[0, hd * V_AUG + V_DIM:(hd + 1) * V_AUG, rows] = ones_rows.astype(BF16)

    def conv_in_job(r0):
        gb_buf[r0:r0 + SUB_TILE, :] = proj(r0, 3)
        u0 = CARRY_ROWS + r0
        u_buf[u0:u0 + SUB_TILE, :] = proj(r0, 4) * proj(r0, 5)

    def conv_out_job(r0):
        u0 = CARRY_ROWS + r0
        y = (cw_ref[0:1, :] * u_buf[u0 - 2:u0 - 2 + SUB_TILE, :]
             + cw_ref[1:2, :] * u_buf[u0 - 1:u0 - 1 + SUB_TILE, :]
             + cw_ref[2:3, :] * u_buf[u0:u0 + SUB_TILE, :])
        c = gb_buf[r0:r0 + SUB_TILE, :] * y
        c = c * _group_rms_scale(c, gmat_ref, CONV_GROUP) * cg_ref[...]
        c_ref[0, r0:r0 + SUB_TILE, :] = c.astype(BF16)

    def mixer_jobs(r0):
        return [lambda: q_job(r0), lambda: k_job(r0), lambda: v_job(r0),
                lambda: conv_in_job(r0), lambda: conv_out_job(r0)]

    def cast_job(w_ref, o_ref):
        o_ref[...] = w_ref[...].astype(BF16)

    _run(_norm_jobs(x_rows, g1_ref, h_buf, sub_tiles[0], BF16))
    pending = [lambda: cast_job(later_w0, later_o0), lambda: cast_job(later_w1, later_o1),
               lambda: cast_job(later_w2, later_o2), lambda: cast_job(later_w3, later_o3)]
    for i, r0 in enumerate(sub_tiles):
        nxt = sub_tiles[i + 1] if i + 1 < len(sub_tiles) else None
        side = (_norm_jobs(x_rows, g1_ref, h_buf, nxt, BF16) if nxt is not None else []) + pending
        _gate_up(h_buf, wg_ref, wu_ref, a_buf, r0, side)
        down(r0)
        pending = _norm_jobs(x1_rows, g2_ref, h2_buf, r0, BF16) + mixer_jobs(r0)
    _run(pending)

    u_buf[0:CARRY_ROWS, :] = u_buf[tm:tm + CARRY_ROWS, :]


def _attn_kernel(lam_ref, qt_ref, k_ref, vt_ref, sg_ref, o_ref,
                 rhs_buf, mask_buf, acc_buf, s_buf):
    blk = ATT_BLOCK
    half = 2 * blk
    seq = k_ref.shape[1]
    lam = lam_ref[0]
    heads = range(N_HEADS)
    all_cols, first, second = slice(0, 2 * half), slice(0, half), slice(half, 2 * half)

    row = lax.broadcasted_iota(jnp.int32, (blk, half), 0)
    col = lax.broadcasted_iota(jnp.int32, (blk, half), 1)
    col = jnp.where(col >= blk, col - blk, col)
    mask_buf[...] = jnp.where(row <= col, 0.0, NEG_BIG)

    rhs_buf[...] = jnp.zeros(rhs_buf.shape, BF16)
    for par in range(2):
        for h in heads:
            rhs_buf[par, h, HEAD_COLS:HEAD_COLS + 16, :] = jnp.where(
                lax.broadcasted_iota(jnp.int32, (16, 2 * half), 0) < N_BIAS_COLS, 1.0, 0.0
            ).astype(BF16)

    def block_start(kb):
        return kb * blk if isinstance(kb, int) else pl.multiple_of(kb * blk, blk)

    def build_rhs(qb, par, h):
        q0 = block_start(2 * qb)
        r0 = h * HEAD_COLS
        for part in range(2):
            qs = pl.ds(q0 + part * blk, blk)
            c0 = part * half
            rhs_buf[par, h, 0:QK_DIM, c0:c0 + blk] = qt_ref[0, r0:r0 + QK_DIM, qs]
            rhs_buf[par, h, QK_DIM:HEAD_COLS, c0 + blk:c0 + half] = (
                qt_ref[0, r0 + QK_DIM:r0 + HEAD_COLS, qs])

    def scores(kb, slot, par, h, cols, want_max=True):
        s = jnp.dot(k_ref[0, pl.ds(block_start(kb), blk), h * K_AUG:(h + 1) * K_AUG],
                    rhs_buf[par, h, :, cols], preferred_element_type=F32)
        s_buf[slot, h, :, cols] = s
        return jnp.max(s, axis=0, keepdims=True) if want_max else None

    def softmax_pv(kb, slot, h, cols, m, tile_max, masked):
        t = s_buf[slot, h, :, cols]
        if masked:
            t = t + mask_buf[...]
            tile_max = jnp.max(t, axis=0, keepdims=True)
        m_new = jnp.maximum(m, tile_max)
        alpha = jnp.exp2(m - m_new)
        p = jnp.exp2(t - m_new).astype(BF16)
        pv = jnp.dot(vt_ref[0, h * V_AUG:(h + 1) * V_AUG, pl.ds(block_start(kb), blk)], p,
                     preferred_element_type=F32)
        acc_buf[h, :, cols] = acc_buf[h, :, cols] * alpha + pv
        return m_new

    def finalize(qb, h):
        inv_l = 1.0 / acc_buf[h, V_DIM:V_DIM + 1, :]
        for part in range(2):
            c0 = part * half
            a = (acc_buf[h, 0:V_DIM, c0:c0 + blk] * inv_l[:, c0:c0 + blk]
                 - acc_buf[h, 0:V_DIM, c0 + blk:c0 + half]
                 * (lam * inv_l[:, c0 + blk:c0 + half]))
            ms_a = jnp.mean(a * a, axis=0, keepdims=True)
            a = a * lax.rsqrt(ms_a + NORM_EPS)
            o_ref[0, pl.ds(block_start(2 * qb + part), blk), h * V_DIM:(h + 1) * V_DIM] = (
                (a.T * sg_ref[...]).astype(BF16))

    def q_block(qb, tms):
        par = qb & 1
        acc_buf[...] = jnp.zeros(acc_buf.shape, F32)
        ms = (jnp.full((1, 2 * half), NEG_BIG, F32),) * N_HEADS

        def full_step(kb, slot, carry):
            ms, tms = carry
            new_ms, new_tms = [], []
            for h in heads:
                new_tms.append(scores(kb + 1, 1 - slot, par, h, all_cols))
                new_ms.append(softmax_pv(kb, slot, h, all_cols, ms[h], tms[h], masked=False))
            return tuple(new_ms), tuple(new_tms)

        def pair(kp, c):
            return full_step(2 * kp + 1, 1, full_step(2 * kp, 0, c))

        ms, tms = lax.fori_loop(0, qb, pair, (ms, tms))

        for h in heads:
            scores(2 * qb + 1, 1, par, h, second, want_max=False)
            softmax_pv(2 * qb, 0, h, first, ms[h][:, first], None, masked=True)
            m_b = softmax_pv(2 * qb, 0, h, second, ms[h][:, second], tms[h][:, second],
                             masked=False)
            softmax_pv(2 * qb + 1, 1, h, second, m_b, None, masked=True)

        nxt = jnp.minimum(qb + 1, n_qb - 1)
        new_tms = []
        for h in heads:
            build_rhs(nxt, 1 - par, h)
            new_tms.append(scores(0, 0, 1 - par, h, all_cols))
            finalize(qb, h)
        return tuple(new_tms)

    n_qb = seq // (2 * blk)
    for h in heads:
        build_rhs(0, 0, h)
    lax.fori_loop(0, n_qb, q_block, tuple(scores(0, 0, 0, h, all_cols) for h in heads))


def _back_kernel(x1_ref, a_ref, c_ref, wo_ref, g3_ref, wg_ref, wu_ref, wd_ref, gf_ref,
                 out_ref, x2_buf, h_buf, a_buf):
    tm = x1_ref.shape[1]
    out_rows = out_ref.at[0]
    sub_tiles = list(range(0, tm, SUB_TILE))

    def out_proj(r0):
        rows = slice(r0, r0 + SUB_TILE)
        x2_buf[rows, :] = (
            x1_ref[0, rows, :]
            + jnp.dot(a_ref[0, rows, :], wo_ref[0:ATTN_W, :], preferred_element_type=F32)
            + jnp.dot(c_ref[0, rows, :], wo_ref[ATTN_W:ATTN_W + CONV_W, :],
                      preferred_element_type=F32))

    def down(r0):
        rows = slice(r0, r0 + SUB_TILE)
        out_ref[0, rows, :] = x2_buf[rows, :] + FFN_RES_W * jnp.dot(
            a_buf[rows, :], wd_ref[...], preferred_element_type=F32)

    out_proj(sub_tiles[0])
    _run(_norm_jobs(x2_buf, g3_ref, h_buf, sub_tiles[0], BF16))
    pending = []
    for i, r0 in enumerate(sub_tiles):
        nxt = sub_tiles[i + 1] if i + 1 < len(sub_tiles) else None
        side = list(pending)
        if nxt is not None:
            side = [lambda nxt=nxt: out_proj(nxt)] + side + _norm_jobs(x2_buf, g3_ref, h_buf, nxt, BF16)
        _gate_up(h_buf, wg_ref, wu_ref, a_buf, r0, side)
        down(r0)
        pending = _norm_jobs(out_rows, gf_ref, out_rows, r0, F32)
    _run(pending)


def _alibi_columns(seq):
    slopes = np.exp2(-8.0 * np.arange(1, N_HEADS + 1) / N_HEADS) * math.log2(math.e)
    rest = (slopes[None, :] * np.arange(seq)[:, None]).astype(np.float32)
    table = np.zeros((seq, N_HEADS, HEAD_COLS), np.float32)
    for c in range(N_BIAS_COLS):
        piece = (rest.view(np.uint32) & np.uint32(0xFFFF0000)).view(np.float32)
        table[:, :, c] = piece
        rest = rest - piece
    return table.reshape(seq, N_HEADS * HEAD_COLS)


def _resident(shape):
    return pl.BlockSpec(shape, lambda *_: (0,) * len(shape), pipeline_mode=pl.Buffered(1))


def kernel(x, ffn1_norm, ffn1_w_gate, ffn1_w_up, ffn1_w_down, mix_norm, w_in, q_norm, k_norm,
           lambda_q1, lambda_k1, lambda_q2, lambda_k2, attn_subln, conv_w, conv_norm, w_out,
           ffn2_norm, ffn2_w_gate, ffn2_w_up, ffn2_w_down, final_norm):
    b, s, d = x.shape
    d_ff = ffn1_w_gate.shape[-1]
    tm = TOKEN_TILE
    nt = s // tm
    assert s % tm == 0 and tm % SUB_TILE == 0 and d_ff % FFN_CHUNK == 0
    assert s % BACK_TILE == 0 and BACK_TILE % SUB_TILE == 0
    assert s % (2 * ATT_BLOCK) == 0
    assert ffn1_w_gate.shape[0] == 1, "single layer"

    row = lambda v: v.reshape(1, -1).astype(F32)
    bf = lambda w: w[0].astype(BF16)
    n_groups = 512 // QK_DIM
    gid = jnp.arange(GMAT_W) // QK_DIM
    gmat = (gid[:, None] == gid[None, :]).astype(BF16)
    log2e = math.log2(math.e)
    q_gain = jnp.tile(q_norm[0].astype(F32), n_groups).reshape(1, 512) * (QK_DIM ** -0.5 * log2e)
    k_gain = jnp.tile(k_norm[0].astype(F32), n_groups).reshape(1, 512)
    bias_cols = jnp.asarray(_alibi_columns(s), dtype=BF16)
    lam = (jnp.exp(jnp.sum(lambda_q1[0].astype(F32) * lambda_k1[0].astype(F32)))
           - jnp.exp(jnp.sum(lambda_q2[0].astype(F32) * lambda_k2[0].astype(F32)))
           + LAM_INIT).reshape(1)
    sub_gain = row(attn_subln[0]) * (1.0 - LAM_INIT)

    tok = lambda w: pl.BlockSpec((1, tm, w), lambda bi, ti: (bi, ti, 0))
    tok_t = lambda r: pl.BlockSpec((1, r, tm), lambda bi, ti: (bi, 0, ti))
    smem = pl.BlockSpec(memory_space=pltpu.SMEM)
    params = pltpu.CompilerParams(dimension_semantics=("arbitrary", "arbitrary"),
                                  vmem_limit_bytes=VMEM_LIMIT)

    later = [ffn2_w_gate[0], ffn2_w_up[0], ffn2_w_down[0], w_out[0]]
    n_steps = b * nt

    def cast_spec(w):
        n_rows = w.shape[0]
        rows = next(r for r in range(CAST_ROWS, n_rows + 1, CAST_ROWS)
                    if n_rows % r == 0 and n_rows // r <= n_steps)
        last = n_rows // rows - 1
        return pl.BlockSpec((rows, w.shape[1]),
                            lambda bi, ti: (jnp.minimum(bi * nt + ti, last), 0))

    outs = pl.pallas_call(
        _front_kernel,
        grid=(b, nt),
        in_specs=[tok(d), pl.BlockSpec((tm, N_HEADS * HEAD_COLS), lambda bi, ti: (ti, 0)),
                  _resident((1, d)), _resident((d, d_ff)), _resident((d, d_ff)),
                  _resident((d_ff, d)), _resident((1, d)), _resident((d, 6 * 512)),
                  _resident((GMAT_W, GMAT_W)), _resident((1, 512)), _resident((1, 512)),
                  _resident((CONV_K, CONV_W)), _resident((1, CONV_W))]
                 + [cast_spec(w) for w in later],
        out_specs=[tok(d), tok_t(512), tok(N_HEADS * K_AUG), tok_t(N_HEADS * V_AUG), tok(CONV_W)]
                  + [cast_spec(w) for w in later],
        out_shape=[jax.ShapeDtypeStruct((b, s, d), F32),
                   jax.ShapeDtypeStruct((b, 512, s), BF16),
                   jax.ShapeDtypeStruct((b, s, N_HEADS * K_AUG), BF16),
                   jax.ShapeDtypeStruct((b, N_HEADS * V_AUG, s), BF16),
                   jax.ShapeDtypeStruct((b, s, CONV_W), BF16)]
                  + [jax.ShapeDtypeStruct(w.shape, BF16) for w in later],
        scratch_shapes=[pltpu.VMEM((tm, d), BF16),
                        pltpu.VMEM((tm, d_ff), BF16),
                        pltpu.VMEM((tm, d), BF16),
                        pltpu.VMEM((tm + CARRY_ROWS, CONV_W), F32),
                        pltpu.VMEM((tm, CONV_W), F32)],
        compiler_params=params,
        name="front",
    )(x, bias_cols, row(ffn1_norm[0]), bf(ffn1_w_gate), bf(ffn1_w_up), bf(ffn1_w_down),
      row(mix_norm[0]), bf(w_in), gmat, q_gain, k_gain,
      conv_w[0].astype(F32), row(conv_norm[0]), *[w.astype(F32) for w in later])
    x1, q_t, k_n, v_t, c_n, w2_gate, w2_up, w2_down, w_out_bf = outs

    blk = ATT_BLOCK
    attn = pl.pallas_call(
        _attn_kernel,
        grid=(b,),
        in_specs=[smem,
                  pl.BlockSpec((1, 512, s), lambda bi: (bi, 0, 0)),
                  pl.BlockSpec((1, s, N_HEADS * K_AUG), lambda bi: (bi, 0, 0)),
                  pl.BlockSpec((1, N_HEADS * V_AUG, s), lambda bi: (bi, 0, 0)),
                  _resident((1, V_DIM))],
        out_specs=pl.BlockSpec((1, s, ATTN_W), lambda bi: (bi, 0, 0)),
        out_shape=jax.ShapeDtypeStruct((b, s, ATTN_W), BF16),
        scratch_shapes=[pltpu.VMEM((2, N_HEADS, K_AUG, 4 * blk), BF16),
                        pltpu.VMEM((blk, 2 * blk), F32),
                        pltpu.VMEM((N_HEADS, V_AUG, 4 * blk), F32),
                        pltpu.VMEM((2, N_HEADS, blk, 4 * blk), F32)],
        compiler_params=pltpu.CompilerParams(dimension_semantics=("arbitrary",),
                                             vmem_limit_bytes=VMEM_LIMIT),
        name="diff_attn",
    )(lam, q_t, k_n, v_t, sub_gain)

    tb = BACK_TILE
    tok_b = lambda w: pl.BlockSpec((1, tb, w), lambda bi, ti: (bi, ti, 0))
    out = pl.pallas_call(
        _back_kernel,
        grid=(b, s // tb),
        in_specs=[tok_b(d), tok_b(ATTN_W), tok_b(CONV_W), _resident((ATTN_W + CONV_W, d)),
                  _resident((1, d)), _resident((d, d_ff)), _resident((d, d_ff)),
                  _resident((d_ff, d)), _resident((1, d))],
        out_specs=tok_b(d),
        out_shape=jax.ShapeDtypeStruct((b, s, d), F32),
        scratch_shapes=[pltpu.VMEM((tb, d), F32),
                        pltpu.VMEM((tb, d), BF16),
                        pltpu.VMEM((tb, d_ff), BF16)],
        compiler_params=params,
        name="back",
    )(x1, attn, c_n, w_out_bf, row(ffn2_norm[0]), w2_gate, w2_up, w2_down, row(final_norm[0]))
    return out
```

```python
import math

import jax
import jax.numpy as jnp
import numpy as np
from jax import lax
from jax.experimental import pallas as pl
from jax.experimental.pallas import tpu as pltpu

F32 = jnp.float32
BF16 = jnp.bfloat16

N_HEADS = 4
QK_DIM = 64
V_DIM = 128
HEAD_COLS = 2 * QK_DIM
K_AUG = 2 * HEAD_COLS
N_BIAS_COLS = 3
BF16_ROWS = 16
V_PAD = BF16_ROWS
V_AUG = V_DIM + V_PAD
ATTN_W = N_HEADS * V_DIM
CONV_W = 512
QK_W = N_HEADS * HEAD_COLS
N_PROJ = 6
CONV_GROUP = 64
MXU_W = 256
GMAT_W = MXU_W
CONV_K = 3
NORM_EPS = 1e-6
FFN_RES_W = 0.5
LAM_INIT = 0.8 - 0.6 * math.exp(-0.3 * 0)
NEG_BIG = -1e30

FFN_CHUNK = 256
TOKEN_TILE = 512
BACK_TILE = 1024
SUB_TILE = 256
NORM_ROWS = 32
ATT_BLOCK = 256
CARRY_ROWS = 8
VMEM_LIMIT = 60000 * 1024


def _rms(x, gain):
    ms = jnp.mean(x * x, axis=-1, keepdims=True)
    return x * lax.rsqrt(ms + NORM_EPS) * gain


def _group_rms_scale(t, gmat_ref, group):
    t2 = (t * t).astype(BF16)
    w = gmat_ref.shape[0]
    ss = jnp.concatenate(
        [jnp.dot(t2[:, c:c + w], gmat_ref[...], preferred_element_type=F32)
         for c in range(0, t.shape[1], w)], axis=1)
    return lax.rsqrt(ss * (1.0 / group) + NORM_EPS)


def _norm_jobs(src_ref, gain_ref, dst_ref, r0, dtype):
    def job(g):
        rows = slice(r0 + g * NORM_ROWS, r0 + (g + 1) * NORM_ROWS)
        dst_ref[rows, :] = _rms(src_ref[rows, :], gain_ref[...]).astype(dtype)
    return [lambda g=g: job(g) for g in range(SUB_TILE // NORM_ROWS)]


def _run(jobs):
    for job in jobs:
        job()


def _gate_up(h_buf, wg_ref, wu_ref, a_buf, r0, side_jobs):
    rows = slice(r0, r0 + SUB_TILE)
    side_jobs = list(side_jobs)
    n_chunks = wg_ref.shape[1] // FFN_CHUNK
    done = 0
    for c in range(n_chunks):
        cols = slice(c * FFN_CHUNK, (c + 1) * FFN_CHUNK)
        h = h_buf[rows, :]
        g = jnp.dot(h, wg_ref[:, cols], preferred_element_type=F32)
        u = jnp.dot(h, wu_ref[:, cols], preferred_element_type=F32)
        a_buf[rows, cols] = (g * jax.nn.sigmoid(g) * u).astype(BF16)
        upto = -(-(c + 1) * len(side_jobs) // n_chunks)
        _run(side_jobs[done:upto])
        done = upto


def _front_kernel(x_ref, bias_ref, g1_ref, wg_ref, wu_ref, wd_ref, g2_ref, win_ref,
                  gmat_ref, qg_ref, kg_ref, cw_ref, cg_ref,
                  later_w0, later_w1, later_w2, later_w3,
                  x1_ref, qt_ref, k_ref, vt_ref, c_ref,
                  later_o0, later_o1, later_o2, later_o3,
                  h_buf, a_buf, h2_buf, u_buf, gb_buf):
    tm = x_ref.shape[1]
    x_rows, x1_rows = x_ref.at[0], x1_ref.at[0]
    sub_tiles = list(range(0, tm, SUB_TILE))

    @pl.when(pl.program_id(1) == 0)
    def _():
        u_buf[0:CARRY_ROWS, :] = jnp.zeros((CARRY_ROWS, CONV_W), F32)

    def down(r0):
        rows = slice(r0, r0 + SUB_TILE)
        x1_ref[0, rows, :] = x_ref[0, rows, :] + FFN_RES_W * jnp.dot(
            a_buf[rows, :], wd_ref[...], preferred_element_type=F32)

    def proj(r0, j):
        return jnp.dot(h2_buf[r0:r0 + SUB_TILE, :], win_ref[:, j * QK_W:(j + 1) * QK_W],
                       preferred_element_type=F32)

    def q_job(r0):
        q = proj(r0, 0)
        q = q * _group_rms_scale(q, gmat_ref, QK_DIM) * qg_ref[...]
        qt_ref[0, :, r0:r0 + SUB_TILE] = q.T.astype(BF16)

    def k_job(r0):
        rows = slice(r0, r0 + SUB_TILE)
        k = proj(r0, 1)
        k = (k * _group_rms_scale(k, gmat_ref, QK_DIM) * kg_ref[...]).astype(BF16)
        for hd in range(N_HEADS):
            k_ref[0, rows, hd * K_AUG:hd * K_AUG + HEAD_COLS] = (
                k[:, hd * HEAD_COLS:(hd + 1) * HEAD_COLS])
            k_ref[0, rows, hd * K_AUG + HEAD_COLS:(hd + 1) * K_AUG] = (
                bias_ref[rows, hd * HEAD_COLS:(hd + 1) * HEAD_COLS])

    def v_job(r0):
        rows = slice(r0, r0 + SUB_TILE)
        vt = proj(r0, 2).T.astype(BF16)
        ones_rows = jnp.where(lax.broadcasted_iota(jnp.int32, (V_PAD, SUB_TILE), 0) == 0, 1.0, 0.0)
        for hd in range(N_HEADS):
            vt_ref[0, hd * V_AUG:hd * V_AUG + V_DIM, rows] = vt[hd * V_DIM:(hd + 1) * V_DIM, :]
            vt_ref[0, hd * V_AUG + V_DIM:(hd + 1) * V_AUG, rows] = ones_rows.astype(BF16)

    def conv_in_job(r0):
        gb_buf[r0:r0 + SUB_TILE, :] = proj(r0, 3)
        u0 = CARRY_ROWS + r0
        u_buf[u0:u0 + SUB_TILE, :] = proj(r0, 4) * proj(r0, 5)

    def conv_out_job(r0):
        u0 = CARRY_ROWS + r0
        y = (cw_ref[0:1, :] * u_buf[u0 - 2:u0 - 2 + SUB_TILE, :]
             + cw_ref[1:2, :] * u_buf[u0 - 1:u0 - 1 + SUB_TILE, :]
             + cw_ref[2:3, :] * u_buf[u0:u0 + SUB_TILE, :])
        c = gb_buf[r0:r0 + SUB_TILE, :] * y
        c = c * _group_rms_scale(c, gmat_ref, CONV_GROUP) * cg_ref[...]
        c_ref[0, r0:r0 + SUB_TILE, :] = c.astype(BF16)

    def mixer_jobs(r0):
        return [lambda: q_job(r0), lambda: k_job(r0), lambda: v_job(r0),
                lambda: conv_in_job(r0), lambda: conv_out_job(r0)]

    def cast_job(w_ref, o_ref):
        o_ref[...] = w_ref[...].astype(BF16)

    _run(_norm_jobs(x_rows, g1_ref, h_buf, sub_tiles[0], BF16))
    pending = [lambda: cast_job(later_w0, later_o0), lambda: cast_job(later_w1, later_o1),
               lambda: cast_job(later_w2, later_o2), lambda: cast_job(later_w3, later_o3)]
    for i, r0 in enumerate(sub_tiles):
        nxt = sub_tiles[i + 1] if i + 1 < len(sub_tiles) else None
        side = (_norm_jobs(x_rows, g1_ref, h_buf, nxt, BF16) if nxt is not None else []) + pending
        _gate_up(h_buf, wg_ref, wu_ref, a_buf, r0, side)
        down(r0)
        pending = _norm_jobs(x1_rows, g2_ref, h2_buf, r0, BF16) + mixer_jobs(r0)
    _run(pending)

    u_buf[0:CARRY_ROWS, :] = u_buf[tm:tm + CARRY_ROWS, :]


def _attn_kernel(lq1_ref, lk1_ref, lq2_ref, lk2_ref, qt_ref, k_ref, vt_ref, sg_ref, o_ref,
                 rhs_buf, mask_buf, acc_buf, s_buf):
    blk = ATT_BLOCK
    half = 2 * blk
    seq = k_ref.shape[1]
    lam = (jnp.exp(jnp.sum(lq1_ref[...] * lk1_ref[...], axis=1, keepdims=True))
           - jnp.exp(jnp.sum(lq2_ref[...] * lk2_ref[...], axis=1, keepdims=True))
           + LAM_INIT)
    heads = range(N_HEADS)
    all_cols, first, second = slice(0, 2 * half), slice(0, half), slice(half, 2 * half)

    row = lax.broadcasted_iota(jnp.int32, (blk, half), 0)
    col = lax.broadcasted_iota(jnp.int32, (blk, half), 1)
    col = jnp.where(col >= blk, col - blk, col)
    mask_buf[...] = jnp.where(row <= col, 0.0, NEG_BIG)

    rhs_buf[...] = jnp.zeros(rhs_buf.shape, BF16)
    for par in range(2):
        for h in heads:
            rhs_buf[par, h, HEAD_COLS:HEAD_COLS + BF16_ROWS, :] = jnp.where(
                lax.broadcasted_iota(jnp.int32, (BF16_ROWS, 2 * half), 0) < N_BIAS_COLS, 1.0, 0.0
            ).astype(BF16)

    def block_start(kb):
        return kb * blk if isinstance(kb, int) else pl.multiple_of(kb * blk, blk)

    def build_rhs(qb, par, h):
        q0 = block_start(2 * qb)
        r0 = h * HEAD_COLS
        for part in range(2):
            qs = pl.ds(q0 + part * blk, blk)
            c0 = part * half
            rhs_buf[par, h, 0:QK_DIM, c0:c0 + blk] = qt_ref[0, r0:r0 + QK_DIM, qs]
            rhs_buf[par, h, QK_DIM:HEAD_COLS, c0 + blk:c0 + half] = (
                qt_ref[0, r0 + QK_DIM:r0 + HEAD_COLS, qs])

    def scores(kb, slot, par, h, cols, want_max=True):
        s = jnp.dot(k_ref[0, pl.ds(block_start(kb), blk), h * K_AUG:(h + 1) * K_AUG],
                    rhs_buf[par, h, :, cols], preferred_element_type=F32)
        s_buf[slot, h, :, cols] = s
        return jnp.max(s, axis=0, keepdims=True) if want_max else None

    def softmax_pv(kb, slot, h, cols, m, tile_max, masked):
        t = s_buf[slot, h, :, cols]
        if masked:
            t = t + mask_buf[...]
            tile_max = jnp.max(t, axis=0, keepdims=True)
        m_new = jnp.maximum(m, tile_max)
        alpha = jnp.exp2(m - m_new)
        p = jnp.exp2(t - m_new).astype(BF16)
        pv = jnp.dot(vt_ref[0, h * V_AUG:(h + 1) * V_AUG, pl.ds(block_start(kb), blk)], p,
                     preferred_element_type=F32)
        acc_buf[h, :, cols] = acc_buf[h, :, cols] * alpha + pv
        return m_new

    def finalize(qb, h):
        inv_l = 1.0 / acc_buf[h, V_DIM:V_DIM + 1, :]
        for part in range(2):
            c0 = part * half
            a = (acc_buf[h, 0:V_DIM, c0:c0 + blk] * inv_l[:, c0:c0 + blk]
                 - acc_buf[h, 0:V_DIM, c0 + blk:c0 + half]
                 * (lam * inv_l[:, c0 + blk:c0 + half]))
            ms_a = jnp.mean(a * a, axis=0, keepdims=True)
            a = a * lax.rsqrt(ms_a + NORM_EPS)
            o_ref[0, pl.ds(block_start(2 * qb + part), blk), h * V_DIM:(h + 1) * V_DIM] = (
                (a.T * sg_ref[...]).astype(BF16))

    def q_block(qb, tms):
        par = qb & 1
        acc_buf[...] = jnp.zeros(acc_buf.shape, F32)
        ms = (jnp.full((1, 2 * half), NEG_BIG, F32),) * N_HEADS

        def full_step(kb, slot, carry):
            ms, tms = carry
            new_ms, new_tms = [], []
            for h in heads:
                new_tms.append(scores(kb + 1, 1 - slot, par, h, all_cols))
                new_ms.append(softmax_pv(kb, slot, h, all_cols, ms[h], tms[h], masked=False))
            return tuple(new_ms), tuple(new_tms)

        def pair(kp, c):
            return full_step(2 * kp + 1, 1, full_step(2 * kp, 0, c))

        ms, tms = lax.fori_loop(0, qb, pair, (ms, tms))

        for h in heads:
            scores(2 * qb + 1, 1, par, h, second, want_max=False)
            softmax_pv(2 * qb, 0, h, first, ms[h][:, first], None, masked=True)
            m_b = softmax_pv(2 * qb, 0, h, second, ms[h][:, second], tms[h][:, second],
                             masked=False)
            softmax_pv(2 * qb + 1, 1, h, second, m_b, None, masked=True)

        nxt = jnp.minimum(qb + 1, n_qb - 1)
        new_tms = []
        for h in heads:
            build_rhs(nxt, 1 - par, h)
            new_tms.append(scores(0, 0, 1 - par, h, all_cols))
            finalize(qb, h)
        return tuple(new_tms)

    n_qb = seq // (2 * blk)
    for h in heads:
        build_rhs(0, 0, h)
    lax.fori_loop(0, n_qb, q_block, tuple(scores(0, 0, 0, h, all_cols) for h in heads))


def _back_kernel(x1_ref, a_ref, c_ref, wo_ref, g3_ref, wg_ref, wu_ref, wd_ref, gf_ref,
                 out_ref, x2_buf, h_buf, a_buf):
    tm = x1_ref.shape[1]
    out_rows = out_ref.at[0]
    sub_tiles = list(range(0, tm, SUB_TILE))

    def out_proj(r0):
        rows = slice(r0, r0 + SUB_TILE)
        x2_buf[rows, :] = (
            x1_ref[0, rows, :]
            + jnp.dot(a_ref[0, rows, :], wo_ref[0:ATTN_W, :], preferred_element_type=F32)
            + jnp.dot(c_ref[0, rows, :], wo_ref[ATTN_W:ATTN_W + CONV_W, :],
                      preferred_element_type=F32))

    def down(r0):
        rows = slice(r0, r0 + SUB_TILE)
        out_ref[0, rows, :] = x2_buf[rows, :] + FFN_RES_W * jnp.dot(
            a_buf[rows, :], wd_ref[...], preferred_element_type=F32)

    out_proj(sub_tiles[0])
    _run(_norm_jobs(x2_buf, g3_ref, h_buf, sub_tiles[0], BF16))
    pending = []
    for i, r0 in enumerate(sub_tiles):
        nxt = sub_tiles[i + 1] if i + 1 < len(sub_tiles) else None
        side = list(pending)
        if nxt is not None:
            side = [lambda nxt=nxt: out_proj(nxt)] + side + _norm_jobs(x2_buf, g3_ref, h_buf, nxt, BF16)
        _gate_up(h_buf, wg_ref, wu_ref, a_buf, r0, side)
        down(r0)
        pending = _norm_jobs(out_rows, gf_ref, out_rows, r0, F32)
    _run(pending)


def _alibi_columns(seq):
    slopes = np.exp2(-8.0 * np.arange(1, N_HEADS + 1) / N_HEADS) * math.log2(math.e)
    rest = (slopes[None, :] * np.arange(seq)[:, None]).astype(np.float32)
    table = np.zeros((seq, N_HEADS, HEAD_COLS), np.float32)
    for c in range(N_BIAS_COLS):
        piece = (rest.view(np.uint32) & np.uint32(0xFFFF0000)).view(np.float32)
        table[:, :, c] = piece
        rest = rest - piece
    return table.reshape(seq, N_HEADS * HEAD_COLS)


def _resident(shape):
    return pl.BlockSpec(shape, lambda *_: (0,) * len(shape), pipeline_mode=pl.Buffered(1))


def kernel(x, ffn1_norm, ffn1_w_gate, ffn1_w_up, ffn1_w_down, mix_norm, w_in, q_norm, k_norm,
           lambda_q1, lambda_k1, lambda_q2, lambda_k2, attn_subln, conv_w, conv_norm, w_out,
           ffn2_norm, ffn2_w_gate, ffn2_w_up, ffn2_w_down, final_norm):
    b, s, d = x.shape
    d_ff = ffn1_w_gate.shape[-1]
    tm = TOKEN_TILE
    nt = s // tm
    assert s % tm == 0 and tm % SUB_TILE == 0 and d_ff % FFN_CHUNK == 0
    assert s % BACK_TILE == 0 and BACK_TILE % SUB_TILE == 0
    assert s % (2 * ATT_BLOCK) == 0
    assert ffn1_w_gate.shape[0] == 1, "single layer"

    row = lambda v: v.reshape(1, -1).astype(F32)
    bf = lambda w: w[0].astype(BF16)
    n_groups = QK_W // QK_DIM
    gid = jnp.arange(GMAT_W) // QK_DIM
    gmat = (gid[:, None] == gid[None, :]).astype(BF16)
    log2e = math.log2(math.e)
    q_gain = jnp.tile(q_norm[0].astype(F32), n_groups).reshape(1, QK_W) * (QK_DIM ** -0.5 * log2e)
    k_gain = jnp.tile(k_norm[0].astype(F32), n_groups).reshape(1, QK_W)
    bias_cols = jnp.asarray(_alibi_columns(s), dtype=BF16)
    sub_gain = row(attn_subln[0]) * (1.0 - LAM_INIT)

    tok = lambda w: pl.BlockSpec((1, tm, w), lambda bi, ti: (bi, ti, 0))
    tok_t = lambda r: pl.BlockSpec((1, r, tm), lambda bi, ti: (bi, 0, ti))
    params = pltpu.CompilerParams(dimension_semantics=("arbitrary", "arbitrary"),
                                  vmem_limit_bytes=VMEM_LIMIT)

    later = [ffn2_w_gate[0], ffn2_w_up[0], ffn2_w_down[0], w_out[0]]
    n_steps = b * nt

    def cast_spec(w):
        n_rows = w.shape[0]
        rows = next(r for r in range(BF16_ROWS, n_rows + 1, BF16_ROWS)
                    if n_rows % r == 0 and n_rows // r <= n_steps)
        last = n_rows // rows - 1
        return pl.BlockSpec((rows, w.shape[1]),
                            lambda bi, ti: (jnp.minimum(bi * nt + ti, last), 0))

    outs = pl.pallas_call(
        _front_kernel,
        grid=(b, nt),
        in_specs=[tok(d), pl.BlockSpec((tm, N_HEADS * HEAD_COLS), lambda bi, ti: (ti, 0)),
                  _resident((1, d)), _resident((d, d_ff)), _resident((d, d_ff)),
                  _resident((d_ff, d)), _resident((1, d)), _resident((d, N_PROJ * QK_W)),
                  _resident((GMAT_W, GMAT_W)), _resident((1, QK_W)), _resident((1, QK_W)),
                  _resident((CONV_K, CONV_W)), _resident((1, CONV_W))]
                 + [cast_spec(w) for w in later],
        out_specs=[tok(d), tok_t(QK_W), tok(N_HEADS * K_AUG), tok_t(N_HEADS * V_AUG), tok(CONV_W)]
                  + [cast_spec(w) for w in later],
        out_shape=[jax.ShapeDtypeStruct((b, s, d), F32),
                   jax.ShapeDtypeStruct((b, QK_W, s), BF16),
                   jax.ShapeDtypeStruct((b, s, N_HEADS * K_AUG), BF16),
                   jax.ShapeDtypeStruct((b, N_HEADS * V_AUG, s), BF16),
                   jax.ShapeDtypeStruct((b, s, CONV_W), BF16)]
                  + [jax.ShapeDtypeStruct(w.shape, BF16) for w in later],
        scratch_shapes=[pltpu.VMEM((tm, d), BF16),
                        pltpu.VMEM((tm, d_ff), BF16),
                        pltpu.VMEM((tm, d), BF16),
                        pltpu.VMEM((tm + CARRY_ROWS, CONV_W), F32),
                        pltpu.VMEM((tm, CONV_W), F32)],
        compiler_params=params,
        name="front",
    )(x, bias_cols, row(ffn1_norm[0]), bf(ffn1_w_gate), bf(ffn1_w_up), bf(ffn1_w_down),
      row(mix_norm[0]), bf(w_in), gmat, q_gain, k_gain,
      conv_w[0].astype(F32), row(conv_norm[0]), *later)
    x1, q_t, k_n, v_t, c_n, w2_gate, w2_up, w2_down, w_out_bf = outs

    blk = ATT_BLOCK
    attn = pl.pallas_call(
        _attn_kernel,
        grid=(b,),
        in_specs=[_resident((1, QK_DIM)) for _ in range(4)] + [
                  pl.BlockSpec((1, QK_W, s), lambda bi: (bi, 0, 0)),
                  pl.BlockSpec((1, s, N_HEADS * K_AUG), lambda bi: (bi, 0, 0)),
                  pl.BlockSpec((1, N_HEADS * V_AUG, s), lambda bi: (bi, 0, 0)),
                  _resident((1, V_DIM))],
        out_specs=pl.BlockSpec((1, s, ATTN_W), lambda bi: (bi, 0, 0)),
        out_shape=jax.ShapeDtypeStruct((b, s, ATTN_W), BF16),
        scratch_shapes=[pltpu.VMEM((2, N_HEADS, K_AUG, 4 * blk), BF16),
                        pltpu.VMEM((blk, 2 * blk), F32),
                        pltpu.VMEM((N_HEADS, V_AUG, 4 * blk), F32),
                        pltpu.VMEM((2, N_HEADS, blk, 4 * blk), F32)],
        compiler_params=pltpu.CompilerParams(dimension_semantics=("arbitrary",),
                                             vmem_limit_bytes=VMEM_LIMIT),
        name="diff_attn",
    )(row(lambda_q1[0]), row(lambda_k1[0]), row(lambda_q2[0]), row(lambda_k2[0]),
      q_t, k_n, v_t, sub_gain)

    tb = BACK_TILE
    tok_b = lambda w: pl.BlockSpec((1, tb, w), lambda bi, ti: (bi, ti, 0))
    out = pl.pallas_call(
        _back_kernel,
        grid=(b, s // tb),
        in_specs=[tok_b(d), tok_b(ATTN_W), tok_b(CONV_W), _resident((ATTN_W + CONV_W, d)),
                  _resident((1, d)), _resident((d, d_ff)), _resident((d, d_ff)),
                  _resident((d_ff, d)), _resident((1, d))],
        out_specs=tok_b(d),
        out_shape=jax.ShapeDtypeStruct((b, s, d), F32),
        scratch_shapes=[pltpu.VMEM((tb, d), F32),
                        pltpu.VMEM((tb, d), BF16),
                        pltpu.VMEM((tb, d_ff), BF16)],
        compiler_params=params,
        name="back",
    )(x1, attn, c_n, w_out_bf, row(ffn2_norm[0]), w2_gate, w2_up, w2_down, row(final_norm[0]))
    return out
```

```python
import math

import jax
import jax.numpy as jnp
import numpy as np
from jax import lax
from jax.experimental import pallas as pl
from jax.experimental.pallas import tpu as pltpu

F32 = jnp.float32
BF16 = jnp.bfloat16

N_HEADS = 4
QK_DIM = 64
V_DIM = 128
HEAD_COLS = 2 * QK_DIM
K_AUG = 2 * HEAD_COLS
N_BIAS_COLS = 3
BF16_ROWS = 16
V_PAD = BF16_ROWS
V_AUG = V_DIM + V_PAD
ATTN_W = N_HEADS * V_DIM
CONV_W = 512
QK_W = N_HEADS * HEAD_COLS
N_PROJ = 6
CONV_GROUP = 64
MXU_W = 256
GMAT_W = MXU_W
CONV_K = 3
NORM_EPS = 1e-6
FFN_RES_W = 0.5
LAM_INIT = 0.8 - 0.6 * math.exp(-0.3 * 0)
NEG_BIG = -1e30

FFN_CHUNK = 256
TOKEN_TILE = 512
BACK_TILE = 1024
SUB_TILE = 256
BACK_SUB = 512
NORM_ROWS = 32
ATT_BLOCK = 256
CARRY_ROWS = 8
VMEM_LIMIT = 60000 * 1024


def _rms(x, gain):
    ms = jnp.mean(x * x, axis=-1, keepdims=True)
    return x * lax.rsqrt(ms + NORM_EPS) * gain


def _group_rms_scale(t, gmat_ref, group):
    t2 = (t * t).astype(BF16)
    w = gmat_ref.shape[0]
    ss = jnp.concatenate(
        [jnp.dot(t2[:, c:c + w], gmat_ref[...], preferred_element_type=F32)
         for c in range(0, t.shape[1], w)], axis=1)
    return lax.rsqrt(ss * (1.0 / group) + NORM_EPS)


def _norm_jobs(src_ref, gain_ref, dst_ref, r0, sub, dtype):
    def job(g):
        rows = slice(r0 + g * NORM_ROWS, r0 + (g + 1) * NORM_ROWS)
        dst_ref[rows, :] = _rms(src_ref[rows, :], gain_ref[...]).astype(dtype)
    return [lambda g=g: job(g) for g in range(sub // NORM_ROWS)]


def _run(jobs):
    for job in jobs:
        job()


def _gate_up(h_buf, wg_ref, wu_ref, a_buf, r0, sub, side_jobs):
    rows = slice(r0, r0 + sub)
    side_jobs = list(side_jobs)
    n_chunks = wg_ref.shape[1] // FFN_CHUNK
    done = 0
    for c in range(n_chunks):
        cols = slice(c * FFN_CHUNK, (c + 1) * FFN_CHUNK)
        h = h_buf[rows, :]
        g = jnp.dot(h, wg_ref[:, cols], preferred_element_type=F32)
        u = jnp.dot(h, wu_ref[:, cols], preferred_element_type=F32)
        a_buf[rows, cols] = (g * jax.nn.sigmoid(g) * u).astype(BF16)
        upto = -(-(c + 1) * len(side_jobs) // n_chunks)
        _run(side_jobs[done:upto])
        done = upto


def _front_kernel(x_ref, bias_ref, g1_ref, wg_ref, wu_ref, wd_ref, g2_ref, win_ref,
                  gmat_ref, qg_ref, kg_ref, cw_ref, cg_ref,
                  later_w0, later_w1, later_w2, later_w3,
                  x1_ref, qt_ref, k_ref, vt_ref, c_ref,
                  later_o0, later_o1, later_o2, later_o3,
                  h_buf, a_buf, h2_buf, u_buf, gb_buf):
    tm = x_ref.shape[1]
    x_rows, x1_rows = x_ref.at[0], x1_ref.at[0]
    sub_tiles = list(range(0, tm, SUB_TILE))

    @pl.when(pl.program_id(1) == 0)
    def _():
        u_buf[0:CARRY_ROWS, :] = jnp.zeros((CARRY_ROWS, CONV_W), F32)

    def down(r0):
        rows = slice(r0, r0 + SUB_TILE)
        x1_ref[0, rows, :] = x_ref[0, rows, :] + FFN_RES_W * jnp.dot(
            a_buf[rows, :], wd_ref[...], preferred_element_type=F32)

    def proj(r0, j):
        return jnp.dot(h2_buf[r0:r0 + SUB_TILE, :], win_ref[:, j * QK_W:(j + 1) * QK_W],
                       preferred_element_type=F32)

    def q_job(r0):
        q = proj(r0, 0)
        q = q * _group_rms_scale(q, gmat_ref, QK_DIM) * qg_ref[...]
        qt_ref[0, :, r0:r0 + SUB_TILE] = q.T.astype(BF16)

    def k_job(r0):
        rows = slice(r0, r0 + SUB_TILE)
        k = proj(r0, 1)
        k = (k * _group_rms_scale(k, gmat_ref, QK_DIM) * kg_ref[...]).astype(BF16)
        for hd in range(N_HEADS):
            k_ref[0, rows, hd * K_AUG:hd * K_AUG + HEAD_COLS] = (
                k[:, hd * HEAD_COLS:(hd + 1) * HEAD_COLS])
            k_ref[0, rows, hd * K_AUG + HEAD_COLS:(hd + 1) * K_AUG] = (
                bias_ref[rows, hd * HEAD_COLS:(hd + 1) * HEAD_COLS])

    def v_job(r0):
        rows = slice(r0, r0 + SUB_TILE)
        vt = proj(r0, 2).T.astype(BF16)
        ones_rows = jnp.where(lax.broadcasted_iota(jnp.int32, (V_PAD, SUB_TILE), 0) == 0, 1.0, 0.0)
        for hd in range(N_HEADS):
            vt_ref[0, hd * V_AUG:hd * V_AUG + V_DIM, rows] = vt[hd * V_DIM:(hd + 1) * V_DIM, :]
            vt_ref[0, hd * V_AUG + V_DIM:(hd + 1) * V_AUG, rows] = ones_rows.astype(BF16)

    def conv_in_job(r0):
        gb_buf[r0:r0 + SUB_TILE, :] = proj(r0, 3)
        u0 = CARRY_ROWS + r0
        u_buf[u0:u0 + SUB_TILE, :] = proj(r0, 4) * proj(r0, 5)

    def conv_out_job(r0):
        u0 = CARRY_ROWS + r0
        y = (cw_ref[0:1, :] * u_buf[u0 - 2:u0 - 2 + SUB_TILE, :]
             + cw_ref[1:2, :] * u_buf[u0 - 1:u0 - 1 + SUB_TILE, :]
             + cw_ref[2:3, :] * u_buf[u0:u0 + SUB_TILE, :])
        c = gb_buf[r0:r0 + SUB_TILE, :] * y
        c = c * _group_rms_scale(c, gmat_ref, CONV_GROUP) * cg_ref[...]
        c_ref[0, r0:r0 + SUB_TILE, :] = c.astype(BF16)

    def mixer_jobs(r0):
        return [lambda: q_job(r0), lambda: k_job(r0), lambda: v_job(r0),
                lambda: conv_in_job(r0), lambda: conv_out_job(r0)]

    def cast_job(w_ref, o_ref):
        o_ref[...] = w_ref[...].astype(BF16)

    _run(_norm_jobs(x_rows, g1_ref, h_buf, sub_tiles[0], SUB_TILE, BF16))
    pending = [lambda: cast_job(later_w0, later_o0), lambda: cast_job(later_w1, later_o1),
               lambda: cast_job(later_w2, later_o2), lambda: cast_job(later_w3, later_o3)]
    for i, r0 in enumerate(sub_tiles):
        nxt = sub_tiles[i + 1] if i + 1 < len(sub_tiles) else None
        side = ((_norm_jobs(x_rows, g1_ref, h_buf, nxt, SUB_TILE, BF16) if nxt is not None else [])
                + pending)
        _gate_up(h_buf, wg_ref, wu_ref, a_buf, r0, SUB_TILE, side)
        down(r0)
        pending = _norm_jobs(x1_rows, g2_ref, h2_buf, r0, SUB_TILE, BF16) + mixer_jobs(r0)
    _run(pending)

    u_buf[0:CARRY_ROWS, :] = u_buf[tm:tm + CARRY_ROWS, :]


def _attn_kernel(lq1_ref, lk1_ref, lq2_ref, lk2_ref, qt_ref, k_ref, vt_ref, sg_ref, o_ref,
                 rhs_buf, mask_buf, acc_buf, s_buf):
    blk = ATT_BLOCK
    half = 2 * blk
    seq = k_ref.shape[1]
    lam = (jnp.exp(jnp.sum(lq1_ref[...] * lk1_ref[...], axis=1, keepdims=True))
           - jnp.exp(jnp.sum(lq2_ref[...] * lk2_ref[...], axis=1, keepdims=True))
           + LAM_INIT)
    heads = range(N_HEADS)
    all_cols, first, second = slice(0, 2 * half), slice(0, half), slice(half, 2 * half)

    row = lax.broadcasted_iota(jnp.int32, (blk, half), 0)
    col = lax.broadcasted_iota(jnp.int32, (blk, half), 1)
    col = jnp.where(col >= blk, col - blk, col)
    mask_buf[...] = jnp.where(row <= col, 0.0, NEG_BIG)

    rhs_buf[...] = jnp.zeros(rhs_buf.shape, BF16)
    for par in range(2):
        for h in heads:
            rhs_buf[par, h, HEAD_COLS:HEAD_COLS + BF16_ROWS, :] = jnp.where(
                lax.broadcasted_iota(jnp.int32, (BF16_ROWS, 2 * half), 0) < N_BIAS_COLS, 1.0, 0.0
            ).astype(BF16)

    def block_start(kb):
        return kb * blk if isinstance(kb, int) else pl.multiple_of(kb * blk, blk)

    def build_rhs(qb, par, h):
        q0 = block_start(2 * qb)
        r0 = h * HEAD_COLS
        for part in range(2):
            qs = pl.ds(q0 + part * blk, blk)
            c0 = part * half
            rhs_buf[par, h, 0:QK_DIM, c0:c0 + blk] = qt_ref[0, r0:r0 + QK_DIM, qs]
            rhs_buf[par, h, QK_DIM:HEAD_COLS, c0 + blk:c0 + half] = (
                qt_ref[0, r0 + QK_DIM:r0 + HEAD_COLS, qs])

    def scores(kb, slot, par, h, cols, want_max=True):
        s = jnp.dot(k_ref[0, pl.ds(block_start(kb), blk), h * K_AUG:(h + 1) * K_AUG],
                    rhs_buf[par, h, :, cols], preferred_element_type=F32)
        s_buf[slot, h, :, cols] = s
        return jnp.max(s, axis=0, keepdims=True) if want_max else None

    def softmax_pv(kb, slot, h, cols, m, tile_max, masked):
        t = s_buf[slot, h, :, cols]
        if masked:
            t = t + mask_buf[...]
            tile_max = jnp.max(t, axis=0, keepdims=True)
        m_new = jnp.maximum(m, tile_max)
        alpha = jnp.exp2(m - m_new)
        p = jnp.exp2(t - m_new).astype(BF16)
        pv = jnp.dot(vt_ref[0, h * V_AUG:(h + 1) * V_AUG, pl.ds(block_start(kb), blk)], p,
                     preferred_element_type=F32)
        acc_buf[h, :, cols] = acc_buf[h, :, cols] * alpha + pv
        return m_new

    def finalize(qb, h):
        inv_l = 1.0 / acc_buf[h, V_DIM:V_DIM + 1, :]
        for part in range(2):
            c0 = part * half
            a = (acc_buf[h, 0:V_DIM, c0:c0 + blk] * inv_l[:, c0:c0 + blk]
                 - acc_buf[h, 0:V_DIM, c0 + blk:c0 + half]
                 * (lam * inv_l[:, c0 + blk:c0 + half]))
            ms_a = jnp.mean(a * a, axis=0, keepdims=True)
            a = a * lax.rsqrt(ms_a + NORM_EPS)
            o_ref[0, pl.ds(block_start(2 * qb + part), blk), h * V_DIM:(h + 1) * V_DIM] = (
                (a.T * sg_ref[...]).astype(BF16))

    def q_block(qb, tms):
        par = qb & 1
        acc_buf[...] = jnp.zeros(acc_buf.shape, F32)
        ms = (jnp.full((1, 2 * half), NEG_BIG, F32),) * N_HEADS

        def full_step(kb, slot, carry):
            ms, tms = carry
            new_ms, new_tms = [], []
            for h in heads:
                new_tms.append(scores(kb + 1, 1 - slot, par, h, all_cols))
                new_ms.append(softmax_pv(kb, slot, h, all_cols, ms[h], tms[h], masked=False))
            return tuple(new_ms), tuple(new_tms)

        def pair(kp, c):
            return full_step(2 * kp + 1, 1, full_step(2 * kp, 0, c))

        ms, tms = lax.fori_loop(0, qb, pair, (ms, tms))

        for h in heads:
            scores(2 * qb + 1, 1, par, h, second, want_max=False)
            softmax_pv(2 * qb, 0, h, first, ms[h][:, first], None, masked=True)
            m_b = softmax_pv(2 * qb, 0, h, second, ms[h][:, second], tms[h][:, second],
                             masked=False)
            softmax_pv(2 * qb + 1, 1, h, second, m_b, None, masked=True)

        nxt = jnp.minimum(qb + 1, n_qb - 1)
        new_tms = []
        for h in heads:
            build_rhs(nxt, 1 - par, h)
            new_tms.append(scores(0, 0, 1 - par, h, all_cols))
            finalize(qb, h)
        return tuple(new_tms)

    n_qb = seq // (2 * blk)
    for h in heads:
        build_rhs(0, 0, h)
    lax.fori_loop(0, n_qb, q_block, tuple(scores(0, 0, 0, h, all_cols) for h in heads))


def _back_kernel(x1_ref, a_ref, c_ref, wo_ref, g3_ref, wg_ref, wu_ref, wd_ref, gf_ref,
                 out_ref, x2_buf, h_buf, a_buf):
    tm = x1_ref.shape[1]
    out_rows = out_ref.at[0]
    sub_tiles = list(range(0, tm, BACK_SUB))

    def out_proj(r0):
        rows = slice(r0, r0 + BACK_SUB)
        x2_buf[rows, :] = (
            x1_ref[0, rows, :]
            + jnp.dot(a_ref[0, rows, :], wo_ref[0:ATTN_W, :], preferred_element_type=F32)
            + jnp.dot(c_ref[0, rows, :], wo_ref[ATTN_W:ATTN_W + CONV_W, :],
                      preferred_element_type=F32))

    def down(r0):
        rows = slice(r0, r0 + BACK_SUB)
        out_ref[0, rows, :] = x2_buf[rows, :] + FFN_RES_W * jnp.dot(
            a_buf[rows, :], wd_ref[...], preferred_element_type=F32)

    out_proj(sub_tiles[0])
    _run(_norm_jobs(x2_buf, g3_ref, h_buf, sub_tiles[0], BACK_SUB, BF16))
    pending = []
    for i, r0 in enumerate(sub_tiles):
        nxt = sub_tiles[i + 1] if i + 1 < len(sub_tiles) else None
        side = list(pending)
        if nxt is not None:
            side = ([lambda nxt=nxt: out_proj(nxt)] + side
                    + _norm_jobs(x2_buf, g3_ref, h_buf, nxt, BACK_SUB, BF16))
        _gate_up(h_buf, wg_ref, wu_ref, a_buf, r0, BACK_SUB, side)
        down(r0)
        pending = _norm_jobs(out_rows, gf_ref, out_rows, r0, BACK_SUB, F32)
    _run(pending)


def _alibi_columns(seq):
    slopes = np.exp2(-8.0 * np.arange(1, N_HEADS + 1) / N_HEADS) * math.log2(math.e)
    rest = (slopes[None, :] * np.arange(seq)[:, None]).astype(np.float32)
    table = np.zeros((seq, N_HEADS, HEAD_COLS), np.float32)
    for c in range(N_BIAS_COLS):
        piece = (rest.view(np.uint32) & np.uint32(0xFFFF0000)).view(np.float32)
        table[:, :, c] = piece
        rest = rest - piece
    return table.reshape(seq, N_HEADS * HEAD_COLS)


def _resident(shape):
    return pl.BlockSpec(shape, lambda *_: (0,) * len(shape), pipeline_mode=pl.Buffered(1))


def kernel(x, ffn1_norm, ffn1_w_gate, ffn1_w_up, ffn1_w_down, mix_norm, w_in, q_norm, k_norm,
           lambda_q1, lambda_k1, lambda_q2, lambda_k2, attn_subln, conv_w, conv_norm, w_out,
           ffn2_norm, ffn2_w_gate, ffn2_w_up, ffn2_w_down, final_norm):
    b, s, d = x.shape
    d_ff = ffn1_w_gate.shape[-1]
    tm = TOKEN_TILE
    nt = s // tm
    assert s % tm == 0 and tm % SUB_TILE == 0 and d_ff % FFN_CHUNK == 0
    assert s % BACK_TILE == 0 and BACK_TILE % BACK_SUB == 0
    assert s % (2 * ATT_BLOCK) == 0
    assert ffn1_w_gate.shape[0] == 1, "single layer"

    row = lambda v: v.reshape(1, -1).astype(F32)
    bf = lambda w: w[0].astype(BF16)
    n_groups = QK_W // QK_DIM
    gid = jnp.arange(GMAT_W) // QK_DIM
    gmat = (gid[:, None] == gid[None, :]).astype(BF16)
    log2e = math.log2(math.e)
    q_gain = jnp.tile(q_norm[0].astype(F32), n_groups).reshape(1, QK_W) * (QK_DIM ** -0.5 * log2e)
    k_gain = jnp.tile(k_norm[0].astype(F32), n_groups).reshape(1, QK_W)
    bias_cols = jnp.asarray(_alibi_columns(s), dtype=BF16)
    sub_gain = row(attn_subln[0]) * (1.0 - LAM_INIT)

    tok = lambda w: pl.BlockSpec((1, tm, w), lambda bi, ti: (bi, ti, 0))
    tok_t = lambda r: pl.BlockSpec((1, r, tm), lambda bi, ti: (bi, 0, ti))
    params = pltpu.CompilerParams(dimension_semantics=("arbitrary", "arbitrary"),
                                  vmem_limit_bytes=VMEM_LIMIT)

    later = [ffn2_w_gate[0], ffn2_w_up[0], ffn2_w_down[0], w_out[0]]
    n_steps = b * nt

    def cast_spec(w):
        n_rows = w.shape[0]
        rows = next(r for r in range(BF16_ROWS, n_rows + 1, BF16_ROWS)
                    if n_rows % r == 0 and n_rows // r <= n_steps)
        last = n_rows // rows - 1
        return pl.BlockSpec((rows, w.shape[1]),
                            lambda bi, ti: (jnp.minimum(bi * nt + ti, last), 0))

    outs = pl.pallas_call(
        _front_kernel,
        grid=(b, nt),
        in_specs=[tok(d), pl.BlockSpec((tm, N_HEADS * HEAD_COLS), lambda bi, ti: (ti, 0)),
                  _resident((1, d)), _resident((d, d_ff)), _resident((d, d_ff)),
                  _resident((d_ff, d)), _resident((1, d)), _resident((d, N_PROJ * QK_W)),
                  _resident((GMAT_W, GMAT_W)), _resident((1, QK_W)), _resident((1, QK_W)),
                  _resident((CONV_K, CONV_W)), _resident((1, CONV_W))]
                 + [cast_spec(w) for w in later],
        out_specs=[tok(d), tok_t(QK_W), tok(N_HEADS * K_AUG), tok_t(N_HEADS * V_AUG), tok(CONV_W)]
                  + [cast_spec(w) for w in later],
        out_shape=[jax.ShapeDtypeStruct((b, s, d), F32),
                   jax.ShapeDtypeStruct((b, QK_W, s), BF16),
                   jax.ShapeDtypeStruct((b, s, N_HEADS * K_AUG), BF16),
                   jax.ShapeDtypeStruct((b, N_HEADS * V_AUG, s), BF16),
                   jax.ShapeDtypeStruct((b, s, CONV_W), BF16)]
                  + [jax.ShapeDtypeStruct(w.shape, BF16) for w in later],
        scratch_shapes=[pltpu.VMEM((tm, d), BF16),
                        pltpu.VMEM((tm, d_ff), BF16),
                        pltpu.VMEM((tm, d), BF16),
                        pltpu.VMEM((tm + CARRY_ROWS, CONV_W), F32),
                        pltpu.VMEM((tm, CONV_W), F32)],
        compiler_params=params,
        name="front",
    )(x, bias_cols, row(ffn1_norm[0]), bf(ffn1_w_gate), bf(ffn1_w_up), bf(ffn1_w_down),
      row(mix_norm[0]), bf(w_in), gmat, q_gain, k_gain,
      conv_w[0].astype(F32), row(conv_norm[0]), *later)
    x1, q_t, k_n, v_t, c_n, w2_gate, w2_up, w2_down, w_out_bf = outs

    blk = ATT_BLOCK
    attn = pl.pallas_call(
        _attn_kernel,
        grid=(b,),
        in_specs=[_resident((1, QK_DIM)) for _ in range(4)] + [
                  pl.BlockSpec((1, QK_W, s), lambda bi: (bi, 0, 0)),
                  pl.BlockSpec((1, s, N_HEADS * K_AUG), lambda bi: (bi, 0, 0)),
                  pl.BlockSpec((1, N_HEADS * V_AUG, s), lambda bi: (bi, 0, 0)),
                  _resident((1, V_DIM))],
        out_specs=pl.BlockSpec((1, s, ATTN_W), lambda bi: (bi, 0, 0)),
        out_shape=jax.ShapeDtypeStruct((b, s, ATTN_W), BF16),
        scratch_shapes=[pltpu.VMEM((2, N_HEADS, K_AUG, 4 * blk), BF16),
                        pltpu.VMEM((blk, 2 * blk), F32),
                        pltpu.VMEM((N_HEADS, V_AUG, 4 * blk), F32),
                        pltpu.VMEM((2, N_HEADS, blk, 4 * blk), F32)],
        compiler_params=pltpu.CompilerParams(dimension_semantics=("arbitrary",),
                                             vmem_limit_bytes=VMEM_LIMIT),
        name="diff_attn",
    )(row(lambda_q1[0]), row(lambda_k1[0]), row(lambda_q2[0]), row(lambda_k2[0]),
      q_t, k_n, v_t, sub_gain)

    tb = BACK_TILE
    tok_b = lambda w: pl.BlockSpec((1, tb, w), lambda bi, ti: (bi, ti, 0))
    out = pl.pallas_call(
        _back_kernel,
        grid=(b, s // tb),
        in_specs=[tok_b(d), tok_b(ATTN_W), tok_b(CONV_W), _resident((ATTN_W + CONV_W, d)),
                  _resident((1, d)), _resident((d, d_ff)), _resident((d, d_ff)),
                  _resident((d_ff, d)), _resident((1, d))],
        out_specs=tok_b(d),
        out_shape=jax.ShapeDtypeStruct((b, s, d), F32),
        scratch_shapes=[pltpu.VMEM((tb, d), F32),
                        pltpu.VMEM((tb, d), BF16),
                        pltpu.VMEM((tb, d_ff), BF16)],
        compiler_params=params,
        name="back",
    )(x1, attn, c_n, w_out_bf, row(ffn2_norm[0]), w2_gate, w2_up, w2_down, row(final_norm[0]))
    return out
```

```python
import math

import jax
import jax.numpy as jnp
import numpy as np
from jax import lax
from jax.experimental import pallas as pl
from jax.experimental.pallas import tpu as pltpu

F32 = jnp.float32
BF16 = jnp.bfloat16

N_HEADS = 4
QK_DIM = 64
V_DIM = 128
HEAD_COLS = 2 * QK_DIM
K_AUG = 2 * HEAD_COLS
N_BIAS_COLS = 3
BF16_ROWS = 16
V_PAD = BF16_ROWS
V_AUG = V_DIM + V_PAD
ATTN_W = N_HEADS * V_DIM
CONV_W = 512
QK_W = N_HEADS * HEAD_COLS
N_PROJ = 6
CONV_GROUP = 64
MXU_W = 256
GMAT_W = MXU_W
CONV_K = 3
NORM_EPS = 1e-6
FFN_RES_W = 0.5
LAM_INIT = 0.8 - 0.6 * math.exp(-0.3 * 0)
NEG_BIG = -1e30

FFN_CHUNK = 256
TOKEN_TILE = 512
BACK_TILE = 1024
SUB_TILE = 256
BACK_SUB = 512
NORM_ROWS = 32
ATT_BLOCK = 256
CARRY_ROWS = 8
VMEM_LIMIT = 60000 * 1024


def _rms(x, gain):
    ms = jnp.mean(x * x, axis=-1, keepdims=True)
    return x * lax.rsqrt(ms + NORM_EPS) * gain


def _group_rms_scale(t, gmat_ref, group):
    t2 = (t * t).astype(BF16)
    w = gmat_ref.shape[0]
    ss = jnp.concatenate(
        [jnp.dot(t2[:, c:c + w], gmat_ref[...], preferred_element_type=F32)
         for c in range(0, t.shape[1], w)], axis=1)
    return lax.rsqrt(ss * (1.0 / group) + NORM_EPS)


def _norm_jobs(src_ref, gain_ref, dst_ref, r0, sub, dtype):
    def job(g):
        rows = slice(r0 + g * NORM_ROWS, r0 + (g + 1) * NORM_ROWS)
        dst_ref[rows, :] = _rms(src_ref[rows, :], gain_ref[...]).astype(dtype)
    return [lambda g=g: job(g) for g in range(sub // NORM_ROWS)]


def _run(jobs):
    for job in jobs:
        job()


def _gate_up(h_buf, wg_ref, wu_ref, a_buf, r0, sub, side_jobs):
    rows = slice(r0, r0 + sub)
    side_jobs = list(side_jobs)
    n_chunks = wg_ref.shape[1] // FFN_CHUNK
    done = 0
    for c in range(n_chunks):
        cols = slice(c * FFN_CHUNK, (c + 1) * FFN_CHUNK)
        h = h_buf[rows, :]
        g = jnp.dot(h, wg_ref[:, cols], preferred_element_type=F32)
        u = jnp.dot(h, wu_ref[:, cols], preferred_element_type=F32)
        a_buf[rows, cols] = (g * jax.nn.sigmoid(g) * u).astype(BF16)
        upto = -(-(c + 1) * len(side_jobs) // n_chunks)
        _run(side_jobs[done:upto])
        done = upto


def _front_kernel(x_ref, bias_ref, g1_ref, wg_ref, wu_ref, wd_ref, g2_ref, win_ref,
                  gmat_ref, qg_ref, kg_ref, cw_ref, cg_ref,
                  later_w0, later_w1, later_w2, later_w3,
                  x1_ref, qt_ref, k_ref, vt_ref, c_ref,
                  later_o0, later_o1, later_o2, later_o3,
                  h_buf, a_buf, h2_buf, u_buf, gb_buf):
    tm = x_ref.shape[1]
    x_rows, x1_rows = x_ref.at[0], x1_ref.at[0]
    sub_tiles = list(range(0, tm, SUB_TILE))

    @pl.when(pl.program_id(1) == 0)
    def _():
        u_buf[0:CARRY_ROWS, :] = jnp.zeros((CARRY_ROWS, CONV_W), F32)

    def down(r0):
        rows = slice(r0, r0 + SUB_TILE)
        x1_ref[0, rows, :] = x_ref[0, rows, :] + FFN_RES_W * jnp.dot(
            a_buf[rows, :], wd_ref[...], preferred_element_type=F32)

    def proj(r0, j):
        return jnp.dot(h2_buf[r0:r0 + SUB_TILE, :], win_ref[:, j * QK_W:(j + 1) * QK_W],
                       preferred_element_type=F32)

    def q_job(r0):
        q = proj(r0, 0)
        q = q * _group_rms_scale(q, gmat_ref, QK_DIM) * qg_ref[...]
        qt_ref[0, :, r0:r0 + SUB_TILE] = q.T.astype(BF16)

    def k_job(r0):
        rows = slice(r0, r0 + SUB_TILE)
        k = proj(r0, 1)
        k = (k * _group_rms_scale(k, gmat_ref, QK_DIM) * kg_ref[...]).astype(BF16)
        for hd in range(N_HEADS):
            k_ref[0, rows, hd * K_AUG:hd * K_AUG + HEAD_COLS] = (
                k[:, hd * HEAD_COLS:(hd + 1) * HEAD_COLS])
            k_ref[0, rows, hd * K_AUG + HEAD_COLS:(hd + 1) * K_AUG] = (
                bias_ref[rows, hd * HEAD_COLS:(hd + 1) * HEAD_COLS])

    def v_job(r0):
        rows = slice(r0, r0 + SUB_TILE)
        vt = proj(r0, 2).T.astype(BF16)
        ones_rows = jnp.where(lax.broadcasted_iota(jnp.int32, (V_PAD, SUB_TILE), 0) == 0, 1.0, 0.0)
        for hd in range(N_HEADS):
            vt_ref[0, hd * V_AUG:hd * V_AUG + V_DIM, rows] = vt[hd * V_DIM:(hd + 1) * V_DIM, :]
            vt_ref[0, hd * V_AUG + V_DIM:(hd + 1) * V_AUG, rows] = ones_rows.astype(BF16)

    def conv_in_job(r0):
        gb_buf[r0:r0 + SUB_TILE, :] = proj(r0, 3)
        u0 = CARRY_ROWS + r0
        u_buf[u0:u0 + SUB_TILE, :] = proj(r0, 4) * proj(r0, 5)

    def conv_out_job(r0):
        u0 = CARRY_ROWS + r0
        y = (cw_ref[0:1, :] * u_buf[u0 - 2:u0 - 2 + SUB_TILE, :]
             + cw_ref[1:2, :] * u_buf[u0 - 1:u0 - 1 + SUB_TILE, :]
             + cw_ref[2:3, :] * u_buf[u0:u0 + SUB_TILE, :])
        c = gb_buf[r0:r0 + SUB_TILE, :] * y
        c = c * _group_rms_scale(c, gmat_ref, CONV_GROUP) * cg_ref[...]
        c_ref[0, r0:r0 + SUB_TILE, :] = c.astype(BF16)

    def mixer_jobs(r0):
        return [lambda: q_job(r0), lambda: k_job(r0), lambda: v_job(r0),
                lambda: conv_in_job(r0), lambda: conv_out_job(r0)]

    def cast_job(w_ref, o_ref):
        o_ref[...] = w_ref[...].astype(BF16)

    _run(_norm_jobs(x_rows, g1_ref, h_buf, sub_tiles[0], SUB_TILE, BF16))
    pending = [lambda: cast_job(later_w0, later_o0), lambda: cast_job(later_w1, later_o1),
               lambda: cast_job(later_w2, later_o2), lambda: cast_job(later_w3, later_o3)]
    for i, r0 in enumerate(sub_tiles):
        nxt = sub_tiles[i + 1] if i + 1 < len(sub_tiles) else None
        side = ((_norm_jobs(x_rows, g1_ref, h_buf, nxt, SUB_TILE, BF16) if nxt is not None else [])
                + pending)
        _gate_up(h_buf, wg_ref, wu_ref, a_buf, r0, SUB_TILE, side)
        down(r0)
        pending = _norm_jobs(x1_rows, g2_ref, h2_buf, r0, SUB_TILE, BF16) + mixer_jobs(r0)
    _run(pending)

    u_buf[0:CARRY_ROWS, :] = u_buf[tm:tm + CARRY_ROWS, :]


def _attn_kernel(lam_ref, qt_ref, k_ref, vt_ref, sg_ref, o_ref,
                 rhs_buf, mask_buf, acc_buf, s_buf):
    blk = ATT_BLOCK
    half = 2 * blk
    seq = k_ref.shape[1]
    lam = lam_ref[0]
    heads = range(N_HEADS)
    all_cols, first, second = slice(0, 2 * half), slice(0, half), slice(half, 2 * half)

    row = lax.broadcasted_iota(jnp.int32, (blk, half), 0)
    col = lax.broadcasted_iota(jnp.int32, (blk, half), 1)
    col = jnp.where(col >= blk, col - blk, col)
    mask_buf[...] = jnp.where(row <= col, 0.0, NEG_BIG)

    rhs_buf[...] = jnp.zeros(rhs_buf.shape, BF16)
    for par in range(2):
        for h in heads:
            rhs_buf[par, h, HEAD_COLS:HEAD_COLS + BF16_ROWS, :] = jnp.where(
                lax.broadcasted_iota(jnp.int32, (BF16_ROWS, 2 * half), 0) < N_BIAS_COLS, 1.0, 0.0
            ).astype(BF16)

    def block_start(kb):
        return kb * blk if isinstance(kb, int) else pl.multiple_of(kb * blk, blk)

    def build_rhs(qb, par, h):
        q0 = block_start(2 * qb)
        r0 = h * HEAD_COLS
        for part in range(2):
            qs = pl.ds(q0 + part * blk, blk)
            c0 = part * half
            rhs_buf[par, h, 0:QK_DIM, c0:c0 + blk] = qt_ref[0, r0:r0 + QK_DIM, qs]
            rhs_buf[par, h, QK_DIM:HEAD_COLS, c0 + blk:c0 + half] = (
                qt_ref[0, r0 + QK_DIM:r0 + HEAD_COLS, qs])

    def scores(kb, slot, par, h, cols, want_max=True):
        s = jnp.dot(k_ref[0, pl.ds(block_start(kb), blk), h * K_AUG:(h + 1) * K_AUG],
                    rhs_buf[par, h, :, cols], preferred_element_type=F32)
        s_buf[slot, h, :, cols] = s
        return jnp.max(s, axis=0, keepdims=True) if want_max else None

    def softmax_pv(kb, slot, h, cols, m, tile_max, masked):
        t = s_buf[slot, h, :, cols]
        if masked:
            t = t + mask_buf[...]
            tile_max = jnp.max(t, axis=0, keepdims=True)
        m_new = jnp.maximum(m, tile_max)
        alpha = jnp.exp2(m - m_new)
        p = jnp.exp2(t - m_new).astype(BF16)
        pv = jnp.dot(vt_ref[0, h * V_AUG:(h + 1) * V_AUG, pl.ds(block_start(kb), blk)], p,
                     preferred_element_type=F32)
        acc_buf[h, :, cols] = acc_buf[h, :, cols] * alpha + pv
        return m_new

    def finalize(qb, h):
        inv_l = 1.0 / acc_buf[h, V_DIM:V_DIM + 1, :]
        for part in range(2):
            c0 = part * half
            a = (acc_buf[h, 0:V_DIM, c0:c0 + blk] * inv_l[:, c0:c0 + blk]
                 - acc_buf[h, 0:V_DIM, c0 + blk:c0 + half]
                 * (lam * inv_l[:, c0 + blk:c0 + half]))
            ms_a = jnp.mean(a * a, axis=0, keepdims=True)
            a = a * lax.rsqrt(ms_a + NORM_EPS)
            o_ref[0, pl.ds(block_start(2 * qb + part), blk), h * V_DIM:(h + 1) * V_DIM] = (
                (a.T * sg_ref[...]).astype(BF16))

    def q_block(qb, tms):
        par = qb & 1
        acc_buf[...] = jnp.zeros(acc_buf.shape, F32)
        ms = (jnp.full((1, 2 * half), NEG_BIG, F32),) * N_HEADS

        def full_step(kb, slot, carry):
            ms, tms = carry
            new_ms, new_tms = [], []
            for h in heads:
                new_tms.append(scores(kb + 1, 1 - slot, par, h, all_cols))
                new_ms.append(softmax_pv(kb, slot, h, all_cols, ms[h], tms[h], masked=False))
            return tuple(new_ms), tuple(new_tms)

        def pair(kp, c):
            return full_step(2 * kp + 1, 1, full_step(2 * kp, 0, c))

        ms, tms = lax.fori_loop(0, qb, pair, (ms, tms))

        for h in heads:
            scores(2 * qb + 1, 1, par, h, second, want_max=False)
            softmax_pv(2 * qb, 0, h, first, ms[h][:, first], None, masked=True)
            m_b = softmax_pv(2 * qb, 0, h, second, ms[h][:, second], tms[h][:, second],
                             masked=False)
            softmax_pv(2 * qb + 1, 1, h, second, m_b, None, masked=True)

        nxt = jnp.minimum(qb + 1, n_qb - 1)
        new_tms = []
        for h in heads:
            build_rhs(nxt, 1 - par, h)
            new_tms.append(scores(0, 0, 1 - par, h, all_cols))
            finalize(qb, h)
        return tuple(new_tms)

    n_qb = seq // (2 * blk)
    for h in heads:
        build_rhs(0, 0, h)
    lax.fori_loop(0, n_qb, q_block, tuple(scores(0, 0, 0, h, all_cols) for h in heads))


def _back_kernel(x1_ref, a_ref, c_ref, wo_ref, g3_ref, wg_ref, wu_ref, wd_ref, gf_ref,
                 out_ref, x2_buf, h_buf, a_buf):
    tm = x1_ref.shape[1]
    out_rows = out_ref.at[0]
    sub_tiles = list(range(0, tm, BACK_SUB))

    def out_proj(r0):
        rows = slice(r0, r0 + BACK_SUB)
        x2_buf[rows, :] = (
            x1_ref[0, rows, :]
            + jnp.dot(a_ref[0, rows, :], wo_ref[0:ATTN_W, :], preferred_element_type=F32)
            + jnp.dot(c_ref[0, rows, :], wo_ref[ATTN_W:ATTN_W + CONV_W, :],
                      preferred_element_type=F32))

    def down(r0):
        rows = slice(r0, r0 + BACK_SUB)
        out_ref[0, rows, :] = x2_buf[rows, :] + FFN_RES_W * jnp.dot(
            a_buf[rows, :], wd_ref[...], preferred_element_type=F32)

    out_proj(sub_tiles[0])
    _run(_norm_jobs(x2_buf, g3_ref, h_buf, sub_tiles[0], BACK_SUB, BF16))
    pending = []
    for i, r0 in enumerate(sub_tiles):
        nxt = sub_tiles[i + 1] if i + 1 < len(sub_tiles) else None
        side = list(pending)
        if nxt is not None:
            side = ([lambda nxt=nxt: out_proj(nxt)] + side
                    + _norm_jobs(x2_buf, g3_ref, h_buf, nxt, BACK_SUB, BF16))
        _gate_up(h_buf, wg_ref, wu_ref, a_buf, r0, BACK_SUB, side)
        down(r0)
        pending = _norm_jobs(out_rows, gf_ref, out_rows, r0, BACK_SUB, F32)
    _run(pending)


def _alibi_columns(seq):
    slopes = np.exp2(-8.0 * np.arange(1, N_HEADS + 1) / N_HEADS) * math.log2(math.e)
    rest = (slopes[None, :] * np.arange(seq)[:, None]).astype(np.float32)
    table = np.zeros((seq, N_HEADS, HEAD_COLS), np.float32)
    for c in range(N_BIAS_COLS):
        piece = (rest.view(np.uint32) & np.uint32(0xFFFF0000)).view(np.float32)
        table[:, :, c] = piece
        rest = rest - piece
    return table.reshape(seq, N_HEADS * HEAD_COLS)


def _resident(shape):
    return pl.BlockSpec(shape, lambda *_: (0,) * len(shape), pipeline_mode=pl.Buffered(1))


def kernel(x, ffn1_norm, ffn1_w_gate, ffn1_w_up, ffn1_w_down, mix_norm, w_in, q_norm, k_norm,
           lambda_q1, lambda_k1, lambda_q2, lambda_k2, attn_subln, conv_w, conv_norm, w_out,
           ffn2_norm, ffn2_w_gate, ffn2_w_up, ffn2_w_down, final_norm):
    b, s, d = x.shape
    d_ff = ffn1_w_gate.shape[-1]
    tm = TOKEN_TILE
    nt = s // tm
    assert s % tm == 0 and tm % SUB_TILE == 0 and d_ff % FFN_CHUNK == 0
    assert s % BACK_TILE == 0 and BACK_TILE % BACK_SUB == 0
    assert s % (2 * ATT_BLOCK) == 0
    assert ffn1_w_gate.shape[0] == 1, "single layer"

    row = lambda v: v.reshape(1, -1).astype(F32)
    bf = lambda w: w[0].astype(BF16)
    n_groups = QK_W // QK_DIM
    gid = jnp.arange(GMAT_W) // QK_DIM
    gmat = (gid[:, None] == gid[None, :]).astype(BF16)
    log2e = math.log2(math.e)
    q_gain = jnp.tile(q_norm[0].astype(F32), n_groups).reshape(1, QK_W) * (QK_DIM ** -0.5 * log2e)
    k_gain = jnp.tile(k_norm[0].astype(F32), n_groups).reshape(1, QK_W)
    bias_cols = jnp.asarray(_alibi_columns(s), dtype=BF16)
    sub_gain = row(attn_subln[0]) * (1.0 - LAM_INIT)
    lam = (jnp.exp(jnp.sum(lambda_q1[0].astype(F32) * lambda_k1[0].astype(F32)))
           - jnp.exp(jnp.sum(lambda_q2[0].astype(F32) * lambda_k2[0].astype(F32)))
           + LAM_INIT).reshape(1)

    tok = lambda w: pl.BlockSpec((1, tm, w), lambda bi, ti: (bi, ti, 0))
    tok_t = lambda r: pl.BlockSpec((1, r, tm), lambda bi, ti: (bi, 0, ti))
    params = pltpu.CompilerParams(dimension_semantics=("arbitrary", "arbitrary"),
                                  vmem_limit_bytes=VMEM_LIMIT)

    later = [ffn2_w_gate[0], ffn2_w_up[0], ffn2_w_down[0], w_out[0]]
    n_steps = b * nt

    def cast_spec(w):
        n_rows = w.shape[0]
        rows = next(r for r in range(BF16_ROWS, n_rows + 1, BF16_ROWS)
                    if n_rows % r == 0 and n_rows // r <= n_steps)
        last = n_rows // rows - 1
        return pl.BlockSpec((rows, w.shape[1]),
                            lambda bi, ti: (jnp.minimum(bi * nt + ti, last), 0))

    outs = pl.pallas_call(
        _front_kernel,
        grid=(b, nt),
        in_specs=[tok(d), pl.BlockSpec((tm, N_HEADS * HEAD_COLS), lambda bi, ti: (ti, 0)),
                  _resident((1, d)), _resident((d, d_ff)), _resident((d, d_ff)),
                  _resident((d_ff, d)), _resident((1, d)), _resident((d, N_PROJ * QK_W)),
                  _resident((GMAT_W, GMAT_W)), _resident((1, QK_W)), _resident((1, QK_W)),
                  _resident((CONV_K, CONV_W)), _resident((1, CONV_W))]
                 + [cast_spec(w) for w in later],
        out_specs=[tok(d), tok_t(QK_W), tok(N_HEADS * K_AUG), tok_t(N_HEADS * V_AUG), tok(CONV_W)]
                  + [cast_spec(w) for w in later],
        out_shape=[jax.ShapeDtypeStruct((b, s, d), F32),
                   jax.ShapeDtypeStruct((b, QK_W, s), BF16),
                   jax.ShapeDtypeStruct((b, s, N_HEADS * K_AUG), BF16),
                   jax.ShapeDtypeStruct((b, N_HEADS * V_AUG, s), BF16),
                   jax.ShapeDtypeStruct((b, s, CONV_W), BF16)]
                  + [jax.ShapeDtypeStruct(w.shape, BF16) for w in later],
        scratch_shapes=[pltpu.VMEM((tm, d), BF16),
                        pltpu.VMEM((tm, d_ff), BF16),
                        pltpu.VMEM((tm, d), BF16),
                        pltpu.VMEM((tm + CARRY_ROWS, CONV_W), F32),
                        pltpu.VMEM((tm, CONV_W), F32)],
        compiler_params=params,
        name="front",
    )(x, bias_cols, row(ffn1_norm[0]), bf(ffn1_w_gate), bf(ffn1_w_up), bf(ffn1_w_down),
      row(mix_norm[0]), bf(w_in), gmat, q_gain, k_gain,
      conv_w[0].astype(F32), row(conv_norm[0]), *later)
    x1, q_t, k_n, v_t, c_n, w2_gate, w2_up, w2_down, w_out_bf = outs

    blk = ATT_BLOCK
    attn = pl.pallas_call(
        _attn_kernel,
        grid=(b,),
        in_specs=[pl.BlockSpec(memory_space=pltpu.SMEM),
                  pl.BlockSpec((1, QK_W, s), lambda bi: (bi, 0, 0)),
                  pl.BlockSpec((1, s, N_HEADS * K_AUG), lambda bi: (bi, 0, 0)),
                  pl.BlockSpec((1, N_HEADS * V_AUG, s), lambda bi: (bi, 0, 0)),
                  _resident((1, V_DIM))],
        out_specs=pl.BlockSpec((1, s, ATTN_W), lambda bi: (bi, 0, 0)),
        out_shape=jax.ShapeDtypeStruct((b, s, ATTN_W), BF16),
        scratch_shapes=[pltpu.VMEM((2, N_HEADS, K_AUG, 4 * blk), BF16),
                        pltpu.VMEM((blk, 2 * blk), F32),
                        pltpu.VMEM((N_HEADS, V_AUG, 4 * blk), F32),
                        pltpu.VMEM((2, N_HEADS, blk, 4 * blk), F32)],
        compiler_params=pltpu.CompilerParams(dimension_semantics=("arbitrary",),
                                             vmem_limit_bytes=VMEM_LIMIT),
        name="diff_attn",
    )(lam, q_t, k_n, v_t, sub_gain)

    tb = BACK_TILE
    tok_b = lambda w: pl.BlockSpec((1, tb, w), lambda bi, ti: (bi, ti, 0))
    out = pl.pallas_call(
        _back_kernel,
        grid=(b, s // tb),
        in_specs=[tok_b(d), tok_b(ATTN_W), tok_b(CONV_W), _resident((ATTN_W + CONV_W, d)),
                  _resident((1, d)), _resident((d, d_ff)), _resident((d, d_ff)),
                  _resident((d_ff, d)), _resident((1, d))],
        out_specs=tok_b(d),
        out_shape=jax.ShapeDtypeStruct((b, s, d), F32),
        scratch_shapes=[pltpu.VMEM((tb, d), F32),
                        pltpu.VMEM((tb, d), BF16),
                        pltpu.VMEM((tb, d_ff), BF16)],
        compiler_params=params,
        name="back",
    )(x1, attn, c_n, w_out_bf, row(ffn2_norm[0]), w2_gate, w2_up, w2_down, row(final_norm[0]))
    return out
```

```python
import functools
import math

import jax
import jax.numpy as jnp
import numpy as np
from jax import lax
from jax.experimental import pallas as pl
from jax.experimental.pallas import tpu as pltpu

F32 = jnp.float32
BF16 = jnp.bfloat16

N_HEADS = 4
QK_DIM = 64
V_DIM = 128
HEAD_COLS = 2 * QK_DIM
K_AUG = 2 * HEAD_COLS
N_BIAS_COLS = 3
BF16_ROWS = 16
V_PAD = BF16_ROWS
V_AUG = V_DIM + V_PAD
ATTN_W = N_HEADS * V_DIM
CONV_W = 512
QK_W = N_HEADS * HEAD_COLS
N_PROJ = 6
CONV_GROUP = 64
MXU_W = 256
GMAT_W = MXU_W
CONV_K = 3
NORM_EPS = 1e-6
FFN_RES_W = 0.5
LAM_INIT = 0.8 - 0.6 * math.exp(-0.3 * 0)
NEG_BIG = -1e30

FFN_CHUNK = 256
TOKEN_TILE = 512
BACK_TILE = 1024
BACK_SUB = 512
NORM_ROWS = 32
ATT_BLOCK = 256
CARRY_ROWS = 8
VMEM_LIMIT = 60000 * 1024


def _rms(x, gain):
    ms = jnp.mean(x * x, axis=-1, keepdims=True)
    return x * lax.rsqrt(ms + NORM_EPS) * gain


def _group_rms_scale(t, gmat_ref, group):
    t2 = (t * t).astype(BF16)
    w = gmat_ref.shape[0]
    ss = jnp.concatenate(
        [jnp.dot(t2[:, c:c + w], gmat_ref[...], preferred_element_type=F32)
         for c in range(0, t.shape[1], w)], axis=1)
    return lax.rsqrt(ss * (1.0 / group) + NORM_EPS)


def _norm_jobs(src_ref, gain_ref, dst_ref, r0, sub, dtype):
    def job(g):
        rows = slice(r0 + g * NORM_ROWS, r0 + (g + 1) * NORM_ROWS)
        dst_ref[rows, :] = _rms(src_ref[rows, :], gain_ref[...]).astype(dtype)
    return [lambda g=g: job(g) for g in range(sub // NORM_ROWS)]


def _run(jobs):
    for job in jobs:
        job()


def _gate_up(h_buf, wg_ref, wu_ref, a_buf, r0, sub, side_jobs):
    rows = slice(r0, r0 + sub)
    side_jobs = list(side_jobs)
    n_chunks = wg_ref.shape[1] // FFN_CHUNK
    done = 0
    for c in range(n_chunks):
        cols = slice(c * FFN_CHUNK, (c + 1) * FFN_CHUNK)
        h = h_buf[rows, :]
        g = jnp.dot(h, wg_ref[:, cols], preferred_element_type=F32)
        u = jnp.dot(h, wu_ref[:, cols], preferred_element_type=F32)
        a_buf[rows, cols] = (g * jax.nn.sigmoid(g) * u).astype(BF16)
        upto = -(-(c + 1) * len(side_jobs) // n_chunks)
        _run(side_jobs[done:upto])
        done = upto


def _front_kernel(x_ref, xn_ref, bias_ref, g1_ref, wg_ref, wu_ref, wd_ref, g2_ref, win_ref,
                  gmat_ref, qg_ref, kg_ref, cw_ref, cg_ref,
                  later_w0, later_w1, later_w2, later_w3,
                  x1_ref, qt_ref, k_ref, vt_ref, c_ref,
                  later_o0, later_o1, later_o2, later_o3,
                  h_buf, hn_buf, a_buf, x1_keep, h2_buf, u_buf, gb_buf, *, tiles_per_seq):
    step = pl.program_id(0)
    n_tiles = pl.num_programs(0) - 1
    tm = x_ref.shape[1]

    @pl.when(step == 0)
    def _():
        _run(_norm_jobs(x_ref.at[0], g1_ref, h_buf, 0, tm, BF16))
        x1_keep[...] = jnp.zeros(x1_keep.shape, F32)
        u_buf[0:CARRY_ROWS, :] = jnp.zeros((CARRY_ROWS, CONV_W), F32)

    def proj(j):
        return jnp.dot(h2_buf[...], win_ref[:, j * QK_W:(j + 1) * QK_W],
                       preferred_element_type=F32)

    def q_job():
        q = proj(0)
        q = q * _group_rms_scale(q, gmat_ref, QK_DIM) * qg_ref[...]
        qt_ref[0] = q.T.astype(BF16)

    def k_job():
        k = proj(1)
        k = (k * _group_rms_scale(k, gmat_ref, QK_DIM) * kg_ref[...]).astype(BF16)
        for hd in range(N_HEADS):
            k_ref[0, :, hd * K_AUG:hd * K_AUG + HEAD_COLS] = k[:, hd * HEAD_COLS:(hd + 1) * HEAD_COLS]
            k_ref[0, :, hd * K_AUG + HEAD_COLS:(hd + 1) * K_AUG] = (
                bias_ref[:, hd * HEAD_COLS:(hd + 1) * HEAD_COLS])

    def v_job():
        vt = proj(2).T.astype(BF16)
        ones_rows = jnp.where(lax.broadcasted_iota(jnp.int32, (V_PAD, tm), 0) == 0, 1.0, 0.0)
        for hd in range(N_HEADS):
            vt_ref[0, hd * V_AUG:hd * V_AUG + V_DIM, :] = vt[hd * V_DIM:(hd + 1) * V_DIM, :]
            vt_ref[0, hd * V_AUG + V_DIM:(hd + 1) * V_AUG, :] = ones_rows.astype(BF16)

    def conv_in_job():
        @pl.when((step + tiles_per_seq - 1) % tiles_per_seq == 0)
        def _():
            u_buf[0:CARRY_ROWS, :] = jnp.zeros((CARRY_ROWS, CONV_W), F32)

        gb_buf[...] = proj(3)
        u_buf[CARRY_ROWS:CARRY_ROWS + tm, :] = proj(4) * proj(5)

    def conv_out_job():
        y = (cw_ref[0:1, :] * u_buf[CARRY_ROWS - 2:CARRY_ROWS - 2 + tm, :]
             + cw_ref[1:2, :] * u_buf[CARRY_ROWS - 1:CARRY_ROWS - 1 + tm, :]
             + cw_ref[2:3, :] * u_buf[CARRY_ROWS:CARRY_ROWS + tm, :])
        c = gb_buf[...] * y
        c = c * _group_rms_scale(c, gmat_ref, CONV_GROUP) * cg_ref[...]
        c_ref[0] = c.astype(BF16)
        u_buf[0:CARRY_ROWS, :] = u_buf[tm:tm + CARRY_ROWS, :]

    def mixer_jobs():
        return (_norm_jobs(x1_keep, g2_ref, h2_buf, 0, tm, BF16)
                + [q_job, k_job, v_job, conv_in_job, conv_out_job])

    def cast_job(w_ref, o_ref):
        o_ref[...] = w_ref[...].astype(BF16)

    cast_jobs = [lambda: cast_job(later_w0, later_o0), lambda: cast_job(later_w1, later_o1),
                 lambda: cast_job(later_w2, later_o2), lambda: cast_job(later_w3, later_o3)]

    @pl.when(step < n_tiles)
    def _():
        side = cast_jobs + mixer_jobs() + _norm_jobs(xn_ref.at[0], g1_ref, hn_buf, 0, tm, BF16)
        _gate_up(h_buf, wg_ref, wu_ref, a_buf, 0, tm, side)
        x1 = x_ref[0] + FFN_RES_W * jnp.dot(a_buf[...], wd_ref[...], preferred_element_type=F32)
        x1_ref[0] = x1
        x1_keep[...] = x1
        h_buf[...] = hn_buf[...]

    @pl.when(step == n_tiles)
    def _():
        _run(cast_jobs + mixer_jobs())


def _attn_kernel(lam_ref, qt_ref, k_ref, vt_ref, sg_ref, o_ref,
                 rhs_buf, mask_buf, acc_buf, s_buf):
    blk = ATT_BLOCK
    half = 2 * blk
    seq = k_ref.shape[1]
    lam = lam_ref[0]
    heads = range(N_HEADS)
    all_cols, first, second = slice(0, 2 * half), slice(0, half), slice(half, 2 * half)

    row = lax.broadcasted_iota(jnp.int32, (blk, half), 0)
    col = lax.broadcasted_iota(jnp.int32, (blk, half), 1)
    col = jnp.where(col >= blk, col - blk, col)
    mask_buf[...] = jnp.where(row <= col, 0.0, NEG_BIG)

    rhs_buf[...] = jnp.zeros(rhs_buf.shape, BF16)
    for par in range(2):
        for h in heads:
            rhs_buf[par, h, HEAD_COLS:HEAD_COLS + BF16_ROWS, :] = jnp.where(
                lax.broadcasted_iota(jnp.int32, (BF16_ROWS, 2 * half), 0) < N_BIAS_COLS, 1.0, 0.0
            ).astype(BF16)

    def block_start(kb):
        return kb * blk if isinstance(kb, int) else pl.multiple_of(kb * blk, blk)

    def build_rhs(qb, par, h):
        q0 = block_start(2 * qb)
        r0 = h * HEAD_COLS
        for part in range(2):
            qs = pl.ds(q0 + part * blk, blk)
            c0 = part * half
            rhs_buf[par, h, 0:QK_DIM, c0:c0 + blk] = qt_ref[0, r0:r0 + QK_DIM, qs]
            rhs_buf[par, h, QK_DIM:HEAD_COLS, c0 + blk:c0 + half] = (
                qt_ref[0, r0 + QK_DIM:r0 + HEAD_COLS, qs])

    def scores(kb, slot, par, h, cols, want_max=True):
        s = jnp.dot(k_ref[0, pl.ds(block_start(kb), blk), h * K_AUG:(h + 1) * K_AUG],
                    rhs_buf[par, h, :, cols], preferred_element_type=F32)
        s_buf[slot, h, :, cols] = s
        return jnp.max(s, axis=0, keepdims=True) if want_max else None

    def softmax_pv(kb, slot, h, cols, m, tile_max, masked):
        t = s_buf[slot, h, :, cols]
        if masked:
            t = t + mask_buf[...]
            tile_max = jnp.max(t, axis=0, keepdims=True)
        m_new = jnp.maximum(m, tile_max)
        alpha = jnp.exp2(m - m_new)
        p = jnp.exp2(t - m_new).astype(BF16)
        pv = jnp.dot(vt_ref[0, h * V_AUG:(h + 1) * V_AUG, pl.ds(block_start(kb), blk)], p,
                     preferred_element_type=F32)
        acc_buf[h, :, cols] = acc_buf[h, :, cols] * alpha + pv
        return m_new

    def finalize(qb, h):
        inv_l = 1.0 / acc_buf[h, V_DIM:V_DIM + 1, :]
        for part in range(2):
            c0 = part * half
            a = (acc_buf[h, 0:V_DIM, c0:c0 + blk] * inv_l[:, c0:c0 + blk]
                 - acc_buf[h, 0:V_DIM, c0 + blk:c0 + half]
                 * (lam * inv_l[:, c0 + blk:c0 + half]))
            ms_a = jnp.mean(a * a, axis=0, keepdims=True)
            a = a * lax.rsqrt(ms_a + NORM_EPS)
            o_ref[0, pl.ds(block_start(2 * qb + part), blk), h * V_DIM:(h + 1) * V_DIM] = (
                (a.T * sg_ref[...]).astype(BF16))

    def q_block(qb, tms):
        par = qb & 1
        acc_buf[...] = jnp.zeros(acc_buf.shape, F32)
        ms = (jnp.full((1, 2 * half), NEG_BIG, F32),) * N_HEADS

        def full_step(kb, slot, carry):
            ms, tms = carry
            new_ms, new_tms = [], []
            for h in heads:
                new_tms.append(scores(kb + 1, 1 - slot, par, h, all_cols))
                new_ms.append(softmax_pv(kb, slot, h, all_cols, ms[h], tms[h], masked=False))
            return tuple(new_ms), tuple(new_tms)

        def pair(kp, c):
            return full_step(2 * kp + 1, 1, full_step(2 * kp, 0, c))

        ms, tms = lax.fori_loop(0, qb, pair, (ms, tms))

        for h in heads:
            scores(2 * qb + 1, 1, par, h, second, want_max=False)
            softmax_pv(2 * qb, 0, h, first, ms[h][:, first], None, masked=True)
            m_b = softmax_pv(2 * qb, 0, h, second, ms[h][:, second], tms[h][:, second],
                             masked=False)
            softmax_pv(2 * qb + 1, 1, h, second, m_b, None, masked=True)

        nxt = jnp.minimum(qb + 1, n_qb - 1)
        new_tms = []
        for h in heads:
            build_rhs(nxt, 1 - par, h)
            new_tms.append(scores(0, 0, 1 - par, h, all_cols))
            finalize(qb, h)
        return tuple(new_tms)

    n_qb = seq // (2 * blk)
    for h in heads:
        build_rhs(0, 0, h)
    lax.fori_loop(0, n_qb, q_block, tuple(scores(0, 0, 0, h, all_cols) for h in heads))


def _back_kernel(x1_ref, a_ref, c_ref, wo_ref, g3_ref, wg_ref, wu_ref, wd_ref, gf_ref,
                 out_ref, x2_buf, h_buf, a_buf):
    tm = x1_ref.shape[1]
    out_rows = out_ref.at[0]
    sub_tiles = list(range(0, tm, BACK_SUB))

    def out_proj(r0):
        rows = slice(r0, r0 + BACK_SUB)
        x2_buf[rows, :] = (
            x1_ref[0, rows, :]
            + jnp.dot(a_ref[0, rows, :], wo_ref[0:ATTN_W, :], preferred_element_type=F32)
            + jnp.dot(c_ref[0, rows, :], wo_ref[ATTN_W:ATTN_W + CONV_W, :],
                      preferred_element_type=F32))

    def down(r0):
        rows = slice(r0, r0 + BACK_SUB)
        out_ref[0, rows, :] = x2_buf[rows, :] + FFN_RES_W * jnp.dot(
            a_buf[rows, :], wd_ref[...], preferred_element_type=F32)

    out_proj(sub_tiles[0])
    _run(_norm_jobs(x2_buf, g3_ref, h_buf, sub_tiles[0], BACK_SUB, BF16))
    pending = []
    for i, r0 in enumerate(sub_tiles):
        nxt = sub_tiles[i + 1] if i + 1 < len(sub_tiles) else None
        side = list(pending)
        if nxt is not None:
            side = ([lambda nxt=nxt: out_proj(nxt)] + side
                    + _norm_jobs(x2_buf, g3_ref, h_buf, nxt, BACK_SUB, BF16))
        _gate_up(h_buf, wg_ref, wu_ref, a_buf, r0, BACK_SUB, side)
        down(r0)
        pending = _norm_jobs(out_rows, gf_ref, out_rows, r0, BACK_SUB, F32)
    _run(pending)


def _alibi_columns(seq):
    slopes = np.exp2(-8.0 * np.arange(1, N_HEADS + 1) / N_HEADS) * math.log2(math.e)
    rest = (slopes[None, :] * np.arange(seq)[:, None]).astype(np.float32)
    table = np.zeros((seq, N_HEADS, HEAD_COLS), np.float32)
    for c in range(N_BIAS_COLS):
        piece = (rest.view(np.uint32) & np.uint32(0xFFFF0000)).view(np.float32)
        table[:, :, c] = piece
        rest = rest - piece
    return table.reshape(seq, N_HEADS * HEAD_COLS)


def _resident(shape):
    return pl.BlockSpec(shape, lambda *_: (0,) * len(shape), pipeline_mode=pl.Buffered(1))


def kernel(x, ffn1_norm, ffn1_w_gate, ffn1_w_up, ffn1_w_down, mix_norm, w_in, q_norm, k_norm,
           lambda_q1, lambda_k1, lambda_q2, lambda_k2, attn_subln, conv_w, conv_norm, w_out,
           ffn2_norm, ffn2_w_gate, ffn2_w_up, ffn2_w_down, final_norm):
    b, s, d = x.shape
    d_ff = ffn1_w_gate.shape[-1]
    tm = TOKEN_TILE
    nt = s // tm
    assert s % tm == 0 and d_ff % FFN_CHUNK == 0
    assert s % BACK_TILE == 0 and BACK_TILE % BACK_SUB == 0
    assert s % (2 * ATT_BLOCK) == 0
    assert ffn1_w_gate.shape[0] == 1, "single layer"

    row = lambda v: v.reshape(1, -1).astype(F32)
    bf = lambda w: w[0].astype(BF16)
    n_groups = QK_W // QK_DIM
    gid = jnp.arange(GMAT_W) // QK_DIM
    gmat = (gid[:, None] == gid[None, :]).astype(BF16)
    log2e = math.log2(math.e)
    q_gain = jnp.tile(q_norm[0].astype(F32), n_groups).reshape(1, QK_W) * (QK_DIM ** -0.5 * log2e)
    k_gain = jnp.tile(k_norm[0].astype(F32), n_groups).reshape(1, QK_W)
    bias_cols = jnp.asarray(_alibi_columns(s), dtype=BF16)
    sub_gain = row(attn_subln[0]) * (1.0 - LAM_INIT)
    lam = (jnp.exp(jnp.sum(lambda_q1[0].astype(F32) * lambda_k1[0].astype(F32)))
           - jnp.exp(jnp.sum(lambda_q2[0].astype(F32) * lambda_k2[0].astype(F32)))
           + LAM_INIT).reshape(1)

    params = pltpu.CompilerParams(dimension_semantics=("arbitrary", "arbitrary"),
                                  vmem_limit_bytes=VMEM_LIMIT)

    n_tiles = b * nt
    cur = lambda i: jnp.minimum(i, n_tiles - 1)
    nxt = lambda i: jnp.minimum(i + 1, n_tiles - 1)
    prev = lambda i: jnp.maximum(i - 1, 0)
    tok = lambda w, sel: pl.BlockSpec((1, tm, w), lambda i: (sel(i) // nt, sel(i) % nt, 0))
    tok_t = lambda r, sel: pl.BlockSpec((1, r, tm), lambda i: (sel(i) // nt, 0, sel(i) % nt))

    later = [ffn2_w_gate[0], ffn2_w_up[0], ffn2_w_down[0], w_out[0]]

    def cast_spec(w):
        n_rows = w.shape[0]
        rows = next(r for r in range(BF16_ROWS, n_rows + 1, BF16_ROWS)
                    if n_rows % r == 0 and n_rows // r <= n_tiles)
        last = n_rows // rows - 1
        return pl.BlockSpec((rows, w.shape[1]), lambda i: (jnp.minimum(i, last), 0))

    outs = pl.pallas_call(
        functools.partial(_front_kernel, tiles_per_seq=nt),
        grid=(n_tiles + 1,),
        in_specs=[tok(d, cur), tok(d, nxt),
                  pl.BlockSpec((tm, N_HEADS * HEAD_COLS), lambda i: (prev(i) % nt, 0)),
                  _resident((1, d)), _resident((d, d_ff)), _resident((d, d_ff)),
                  _resident((d_ff, d)), _resident((1, d)), _resident((d, N_PROJ * QK_W)),
                  _resident((GMAT_W, GMAT_W)), _resident((1, QK_W)), _resident((1, QK_W)),
                  _resident((CONV_K, CONV_W)), _resident((1, CONV_W))]
                 + [cast_spec(w) for w in later],
        out_specs=[tok(d, cur), tok_t(QK_W, prev), tok(N_HEADS * K_AUG, prev),
                   tok_t(N_HEADS * V_AUG, prev), tok(CONV_W, prev)]
                  + [cast_spec(w) for w in later],
        out_shape=[jax.ShapeDtypeStruct((b, s, d), F32),
                   jax.ShapeDtypeStruct((b, QK_W, s), BF16),
                   jax.ShapeDtypeStruct((b, s, N_HEADS * K_AUG), BF16),
                   jax.ShapeDtypeStruct((b, N_HEADS * V_AUG, s), BF16),
                   jax.ShapeDtypeStruct((b, s, CONV_W), BF16)]
                  + [jax.ShapeDtypeStruct(w.shape, BF16) for w in later],
        scratch_shapes=[pltpu.VMEM((tm, d), BF16),
                        pltpu.VMEM((tm, d), BF16),
                        pltpu.VMEM((tm, d_ff), BF16),
                        pltpu.VMEM((tm, d), F32),
                        pltpu.VMEM((tm, d), BF16),
                        pltpu.VMEM((tm + CARRY_ROWS, CONV_W), F32),
                        pltpu.VMEM((tm, CONV_W), F32)],
        compiler_params=pltpu.CompilerParams(dimension_semantics=("arbitrary",),
                                             vmem_limit_bytes=VMEM_LIMIT),
        name="front",
    )(x, x, bias_cols, row(ffn1_norm[0]), bf(ffn1_w_gate), bf(ffn1_w_up), bf(ffn1_w_down),
      row(mix_norm[0]), bf(w_in), gmat, q_gain, k_gain,
      conv_w[0].astype(F32), row(conv_norm[0]), *later)
    x1, q_t, k_n, v_t, c_n, w2_gate, w2_up, w2_down, w_out_bf = outs

    blk = ATT_BLOCK
    attn = pl.pallas_call(
        _attn_kernel,
        grid=(b,),
        in_specs=[pl.BlockSpec(memory_space=pltpu.SMEM),
                  pl.BlockSpec((1, QK_W, s), lambda bi: (bi, 0, 0)),
                  pl.BlockSpec((1, s, N_HEADS * K_AUG), lambda bi: (bi, 0, 0)),
                  pl.BlockSpec((1, N_HEADS * V_AUG, s), lambda bi: (bi, 0, 0)),
                  _resident((1, V_DIM))],
        out_specs=pl.BlockSpec((1, s, ATTN_W), lambda bi: (bi, 0, 0)),
        out_shape=jax.ShapeDtypeStruct((b, s, ATTN_W), BF16),
        scratch_shapes=[pltpu.VMEM((2, N_HEADS, K_AUG, 4 * blk), BF16),
                        pltpu.VMEM((blk, 2 * blk), F32),
                        pltpu.VMEM((N_HEADS, V_AUG, 4 * blk), F32),
                        pltpu.VMEM((2, N_HEADS, blk, 4 * blk), F32)],
        compiler_params=pltpu.CompilerParams(dimension_semantics=("arbitrary",),
                                             vmem_limit_bytes=VMEM_LIMIT),
        name="diff_attn",
    )(lam, q_t, k_n, v_t, sub_gain)

    tb = BACK_TILE
    tok_b = lambda w: pl.BlockSpec((1, tb, w), lambda bi, ti: (bi, ti, 0))
    out = pl.pallas_call(
        _back_kernel,
        grid=(b, s // tb),
        in_specs=[tok_b(d), tok_b(ATTN_W), tok_b(CONV_W), _resident((ATTN_W + CONV_W, d)),
                  _resident((1, d)), _resident((d, d_ff)), _resident((d, d_ff)),
                  _resident((d_ff, d)), _resident((1, d))],
        out_specs=tok_b(d),
        out_shape=jax.ShapeDtypeStruct((b, s, d), F32),
        scratch_shapes=[pltpu.VMEM((tb, d), F32),
                        pltpu.VMEM((tb, d), BF16),
                        pltpu.VMEM((tb, d_ff), BF16)],
        compiler_params=params,
        name="back",
    )(x1, attn, c_n, w_out_bf, row(ffn2_norm[0]), w2_gate, w2_up, w2_down, row(final_norm[0]))
    return out
```

```python
import math

import jax
import jax.numpy as jnp
import numpy as np
from jax import lax
from jax.experimental import pallas as pl
from jax.experimental.pallas import tpu as pltpu

F32 = jnp.float32
BF16 = jnp.bfloat16

N_HEADS = 4
QK_DIM = 64
V_DIM = 128
HEAD_COLS = 2 * QK_DIM
K_AUG = 2 * HEAD_COLS
N_BIAS_COLS = 3
BF16_ROWS = 16
V_PAD = BF16_ROWS
V_AUG = V_DIM + V_PAD
ATTN_W = N_HEADS * V_DIM
CONV_W = 512
QK_W = N_HEADS * HEAD_COLS
N_PROJ = 6
CONV_GROUP = 64
MXU_W = 256
GMAT_W = MXU_W
CONV_K = 3
NORM_EPS = 1e-6
FFN_RES_W = 0.5
LAM_INIT = 0.8 - 0.6 * math.exp(-0.3 * 0)
NEG_BIG = -1e30

FFN_CHUNK = 256
TOKEN_TILE = 512
BACK_TILE = 1024
SUB_TILE = 256
BACK_SUB = 512
NORM_ROWS = 32
ATT_BLOCK = 256
CARRY_ROWS = 8
VMEM_LIMIT = 60000 * 1024


def _rms(x, gain):
    ms = jnp.mean(x * x, axis=-1, keepdims=True)
    return x * lax.rsqrt(ms + NORM_EPS) * gain


def _group_rms_scale(t, gmat_ref, group):
    t2 = (t * t).astype(BF16)
    w = gmat_ref.shape[0]
    ss = jnp.concatenate(
        [jnp.dot(t2[:, c:c + w], gmat_ref[...], preferred_element_type=F32)
         for c in range(0, t.shape[1], w)], axis=1)
    return lax.rsqrt(ss * (1.0 / group) + NORM_EPS)


def _norm_jobs(src_ref, gain_ref, dst_ref, r0, sub, dtype):
    def job(g):
        rows = slice(r0 + g * NORM_ROWS, r0 + (g + 1) * NORM_ROWS)
        dst_ref[rows, :] = _rms(src_ref[rows, :], gain_ref[...]).astype(dtype)
    return [lambda g=g: job(g) for g in range(sub // NORM_ROWS)]


def _run(jobs):
    for job in jobs:
        job()


def _gate_up(h_buf, wg_ref, wu_ref, a_buf, r0, sub, side_jobs):
    rows = slice(r0, r0 + sub)
    side_jobs = list(side_jobs)
    n_chunks = wg_ref.shape[1] // FFN_CHUNK
    done = 0
    for c in range(n_chunks):
        cols = slice(c * FFN_CHUNK, (c + 1) * FFN_CHUNK)
        h = h_buf[rows, :]
        g = jnp.dot(h, wg_ref[:, cols], preferred_element_type=F32)
        u = jnp.dot(h, wu_ref[:, cols], preferred_element_type=F32)
        a_buf[rows, cols] = (g * jax.nn.sigmoid(g) * u).astype(BF16)
        upto = -(-(c + 1) * len(side_jobs) // n_chunks)
        _run(side_jobs[done:upto])
        done = upto


def _front_kernel(x_ref, bias_ref, g1_ref, wg_ref, wu_ref, wd_ref, g2_ref, win_ref,
                  gmat_ref, qg_ref, kg_ref, cw_ref, cg_ref,
                  later_w0, later_w1, later_w2, later_w3,
                  x1_ref, qt_ref, k_ref, vt_ref, c_ref,
                  later_o0, later_o1, later_o2, later_o3,
                  h_buf, a_buf, h2_buf, u_buf, gb_buf):
    tm = x_ref.shape[1]
    x_rows, x1_rows = x_ref.at[0], x1_ref.at[0]
    sub_tiles = list(range(0, tm, SUB_TILE))

    @pl.when(pl.program_id(1) == 0)
    def _():
        u_buf[0:CARRY_ROWS, :] = jnp.zeros((CARRY_ROWS, CONV_W), F32)

    def down(r0):
        rows = slice(r0, r0 + SUB_TILE)
        x1_ref[0, rows, :] = x_ref[0, rows, :] + FFN_RES_W * jnp.dot(
            a_buf[rows, :], wd_ref[...], preferred_element_type=F32)

    def proj(r0, j):
        return jnp.dot(h2_buf[r0:r0 + SUB_TILE, :], win_ref[:, j * QK_W:(j + 1) * QK_W],
                       preferred_element_type=F32)

    def q_job(r0):
        q = proj(r0, 0)
        q = q * _group_rms_scale(q, gmat_ref, QK_DIM) * qg_ref[...]
        qt_ref[0, :, r0:r0 + SUB_TILE] = q.T.astype(BF16)

    def k_job(r0):
        rows = slice(r0, r0 + SUB_TILE)
        k = proj(r0, 1)
        k = (k * _group_rms_scale(k, gmat_ref, QK_DIM) * kg_ref[...]).astype(BF16)
        for hd in range(N_HEADS):
            k_ref[0, rows, hd * K_AUG:hd * K_AUG + HEAD_COLS] = (
                k[:, hd * HEAD_COLS:(hd + 1) * HEAD_COLS])
            k_ref[0, rows, hd * K_AUG + HEAD_COLS:(hd + 1) * K_AUG] = (
                bias_ref[rows, hd * HEAD_COLS:(hd + 1) * HEAD_COLS])

    def v_job(r0):
        rows = slice(r0, r0 + SUB_TILE)
        vt = proj(r0, 2).T.astype(BF16)
        ones_rows = jnp.where(lax.broadcasted_iota(jnp.int32, (V_PAD, SUB_TILE), 0) == 0, 1.0, 0.0)
        for hd in range(N_HEADS):
            vt_ref[0, hd * V_AUG:hd * V_AUG + V_DIM, rows] = vt[hd * V_DIM:(hd + 1) * V_DIM, :]
            vt_ref[0, hd * V_AUG + V_DIM:(hd + 1) * V_AUG, rows] = ones_rows.astype(BF16)

    def conv_in_job(r0):
        gb_buf[r0:r0 + SUB_TILE, :] = proj(r0, 3)
        u0 = CARRY_ROWS + r0
        u_buf[u0:u0 + SUB_TILE, :] = proj(r0, 4) * proj(r0, 5)

    def conv_out_job(r0):
        u0 = CARRY_ROWS + r0
        y = (cw_ref[0:1, :] * u_buf[u0 - 2:u0 - 2 + SUB_TILE, :]
             + cw_ref[1:2, :] * u_buf[u0 - 1:u0 - 1 + SUB_TILE, :]
             + cw_ref[2:3, :] * u_buf[u0:u0 + SUB_TILE, :])
        c = gb_buf[r0:r0 + SUB_TILE, :] * y
        c = c * _group_rms_scale(c, gmat_ref, CONV_GROUP) * cg_ref[...]
        c_ref[0, r0:r0 + SUB_TILE, :] = c.astype(BF16)

    def mixer_jobs(r0):
        return [lambda: q_job(r0), lambda: k_job(r0), lambda: v_job(r0),
                lambda: conv_in_job(r0), lambda: conv_out_job(r0)]

    def cast_job(w_ref, o_ref):
        o_ref[...] = w_ref[...].astype(BF16)

    _run(_norm_jobs(x_rows, g1_ref, h_buf, sub_tiles[0], SUB_TILE, BF16))
    pending = [lambda: cast_job(later_w0, later_o0), lambda: cast_job(later_w1, later_o1),
               lambda: cast_job(later_w2, later_o2), lambda: cast_job(later_w3, later_o3)]
    for i, r0 in enumerate(sub_tiles):
        nxt = sub_tiles[i + 1] if i + 1 < len(sub_tiles) else None
        side = ((_norm_jobs(x_rows, g1_ref, h_buf, nxt, SUB_TILE, BF16) if nxt is not None else [])
                + pending)
        _gate_up(h_buf, wg_ref, wu_ref, a_buf, r0, SUB_TILE, side)
        down(r0)
        pending = _norm_jobs(x1_rows, g2_ref, h2_buf, r0, SUB_TILE, BF16) + mixer_jobs(r0)
    _run(pending)

    u_buf[0:CARRY_ROWS, :] = u_buf[tm:tm + CARRY_ROWS, :]


def _attn_kernel(lam_ref, qt_ref, k_ref, vt_ref, sg_ref, o_ref,
                 rhs_buf, mask_buf, acc_buf, s_buf):
    blk = ATT_BLOCK
    half = 2 * blk
    seq = k_ref.shape[1]
    lam = lam_ref[0]
    heads = range(N_HEADS)
    all_cols, first, second = slice(0, 2 * half), slice(0, half), slice(half, 2 * half)

    row = lax.broadcasted_iota(jnp.int32, (blk, half), 0)
    col = lax.broadcasted_iota(jnp.int32, (blk, half), 1)
    col = jnp.where(col >= blk, col - blk, col)
    mask_buf[...] = jnp.where(row <= col, 0.0, NEG_BIG)

    rhs_buf[...] = jnp.zeros(rhs_buf.shape, BF16)
    for par in range(2):
        for h in heads:
            rhs_buf[par, h, HEAD_COLS:HEAD_COLS + BF16_ROWS, :] = jnp.where(
                lax.broadcasted_iota(jnp.int32, (BF16_ROWS, 2 * half), 0) < N_BIAS_COLS, 1.0, 0.0
            ).astype(BF16)

    def block_start(kb):
        return kb * blk if isinstance(kb, int) else pl.multiple_of(kb * blk, blk)

    def build_rhs(qb, par, h):
        q0 = block_start(2 * qb)
        r0 = h * HEAD_COLS
        for part in range(2):
            qs = pl.ds(q0 + part * blk, blk)
            c0 = part * half
            rhs_buf[par, h, 0:QK_DIM, c0:c0 + blk] = qt_ref[0, r0:r0 + QK_DIM, qs]
            rhs_buf[par, h, QK_DIM:HEAD_COLS, c0 + blk:c0 + half] = (
                qt_ref[0, r0 + QK_DIM:r0 + HEAD_COLS, qs])

    def scores(kb, slot, par, h, cols, want_max=True):
        s = jnp.dot(k_ref[0, pl.ds(block_start(kb), blk), h * K_AUG:(h + 1) * K_AUG],
                    rhs_buf[par, h, :, cols], preferred_element_type=F32)
        s_buf[slot, h, :, cols] = s
        return jnp.max(s, axis=0, keepdims=True) if want_max else None

    def softmax_pv(kb, slot, h, cols, m, tile_max, masked):
        t = s_buf[slot, h, :, cols]
        if masked:
            t = t + mask_buf[...]
            tile_max = jnp.max(t, axis=0, keepdims=True)
        m_new = jnp.maximum(m, tile_max)
        alpha = jnp.exp2(m - m_new)
        p = jnp.exp2(t - m_new).astype(BF16)
        pv = jnp.dot(vt_ref[0, h * V_AUG:(h + 1) * V_AUG, pl.ds(block_start(kb), blk)], p,
                     preferred_element_type=F32)
        acc_buf[h, :, cols] = acc_buf[h, :, cols] * alpha + pv
        return m_new

    def finalize(qb, h):
        inv_l = 1.0 / acc_buf[h, V_DIM:V_DIM + 1, :]
        for part in range(2):
            c0 = part * half
            a = (acc_buf[h, 0:V_DIM, c0:c0 + blk] * inv_l[:, c0:c0 + blk]
                 - acc_buf[h, 0:V_DIM, c0 + blk:c0 + half]
                 * (lam * inv_l[:, c0 + blk:c0 + half]))
            ms_a = jnp.mean(a * a, axis=0, keepdims=True)
            a = a * lax.rsqrt(ms_a + NORM_EPS)
            o_ref[0, pl.ds(block_start(2 * qb + part), blk), h * V_DIM:(h + 1) * V_DIM] = (
                (a.T * sg_ref[...]).astype(BF16))

    def q_block(qb, tms):
        par = qb & 1
        acc_buf[...] = jnp.zeros(acc_buf.shape, F32)
        ms = (jnp.full((1, 2 * half), NEG_BIG, F32),) * N_HEADS

        def full_step(kb, slot, carry):
            ms, tms = carry
            new_ms, new_tms = [], []
            for h in heads:
                new_tms.append(scores(kb + 1, 1 - slot, par, h, all_cols))
                new_ms.append(softmax_pv(kb, slot, h, all_cols, ms[h], tms[h], masked=False))
            return tuple(new_ms), tuple(new_tms)

        def pair(kp, c):
            ms, tms = c
            k0 = 2 * kp
            tm1 = [scores(k0 + 1, 1, par, 0, all_cols)] + [None] * (N_HEADS - 1)
            new_ms, new_tms = [], []
            for h in heads:
                if h + 1 < N_HEADS:
                    tm1[h + 1] = scores(k0 + 1, 1, par, h + 1, all_cols)
                m_new = jnp.maximum(ms[h], jnp.maximum(tms[h], tm1[h]))
                alpha = jnp.exp2(ms[h] - m_new)
                p = jnp.concatenate(
                    [jnp.exp2(s_buf[0, h] - m_new).astype(BF16),
                     jnp.exp2(s_buf[1, h] - m_new).astype(BF16)], axis=0)
                pv = jnp.dot(vt_ref[0, h * V_AUG:(h + 1) * V_AUG, pl.ds(block_start(k0), 2 * blk)],
                             p, preferred_element_type=F32)
                acc_buf[h] = acc_buf[h] * alpha + pv
                new_tms.append(scores(k0 + 2, 0, par, h, all_cols))
                new_ms.append(m_new)
            return tuple(new_ms), tuple(new_tms)

        ms, tms = lax.fori_loop(0, qb, pair, (ms, tms))

        for h in heads:
            scores(2 * qb + 1, 1, par, h, second, want_max=False)
            softmax_pv(2 * qb, 0, h, first, ms[h][:, first], None, masked=True)
            m_b = softmax_pv(2 * qb, 0, h, second, ms[h][:, second], tms[h][:, second],
                             masked=False)
            softmax_pv(2 * qb + 1, 1, h, second, m_b, None, masked=True)

        nxt = jnp.minimum(qb + 1, n_qb - 1)
        new_tms = []
        for h in heads:
            build_rhs(nxt, 1 - par, h)
            new_tms.append(scores(0, 0, 1 - par, h, all_cols))
            finalize(qb, h)
        return tuple(new_tms)

    n_qb = seq // (2 * blk)
    for h in heads:
        build_rhs(0, 0, h)
    lax.fori_loop(0, n_qb, q_block, tuple(scores(0, 0, 0, h, all_cols) for h in heads))


def _back_kernel(x1_ref, a_ref, c_ref, wo_ref, g3_ref, wg_ref, wu_ref, wd_ref, gf_ref,
                 out_ref, x2_buf, h_buf, a_buf):
    tm = x1_ref.shape[1]
    out_rows = out_ref.at[0]
    sub_tiles = list(range(0, tm, BACK_SUB))

    def out_proj(r0):
        rows = slice(r0, r0 + BACK_SUB)
        x2_buf[rows, :] = (
            x1_ref[0, rows, :]
            + jnp.dot(a_ref[0, rows, :], wo_ref[0:ATTN_W, :], preferred_element_type=F32)
            + jnp.dot(c_ref[0, rows, :], wo_ref[ATTN_W:ATTN_W + CONV_W, :],
                      preferred_element_type=F32))

    def down(r0):
        rows = slice(r0, r0 + BACK_SUB)
        out_ref[0, rows, :] = x2_buf[rows, :] + FFN_RES_W * jnp.dot(
            a_buf[rows, :], wd_ref[...], preferred_element_type=F32)

    out_proj(sub_tiles[0])
    _run(_norm_jobs(x2_buf, g3_ref, h_buf, sub_tiles[0], BACK_SUB, BF16))
    pending = []
    for i, r0 in enumerate(sub_tiles):
        nxt = sub_tiles[i + 1] if i + 1 < len(sub_tiles) else None
        side = list(pending)
        if nxt is not None:
            side = ([lambda nxt=nxt: out_proj(nxt)] + side
                    + _norm_jobs(x2_buf, g3_ref, h_buf, nxt, BACK_SUB, BF16))
        _gate_up(h_buf, wg_ref, wu_ref, a_buf, r0, BACK_SUB, side)
        down(r0)
        pending = _norm_jobs(out_rows, gf_ref, out_rows, r0, BACK_SUB, F32)
    _run(pending)


def _alibi_columns(seq):
    slopes = np.exp2(-8.0 * np.arange(1, N_HEADS + 1) / N_HEADS) * math.log2(math.e)
    rest = (slopes[None, :] * np.arange(seq)[:, None]).astype(np.float32)
    table = np.zeros((seq, N_HEADS, HEAD_COLS), np.float32)
    for c in range(N_BIAS_COLS):
        piece = (rest.view(np.uint32) & np.uint32(0xFFFF0000)).view(np.float32)
        table[:, :, c] = piece
        rest = rest - piece
    return table.reshape(seq, N_HEADS * HEAD_COLS)


def _resident(shape):
    return pl.BlockSpec(shape, lambda *_: (0,) * len(shape), pipeline_mode=pl.Buffered(1))


def kernel(x, ffn1_norm, ffn1_w_gate, ffn1_w_up, ffn1_w_down, mix_norm, w_in, q_norm, k_norm,
           lambda_q1, lambda_k1, lambda_q2, lambda_k2, attn_subln, conv_w, conv_norm, w_out,
           ffn2_norm, ffn2_w_gate, ffn2_w_up, ffn2_w_down, final_norm):
    b, s, d = x.shape
    d_ff = ffn1_w_gate.shape[-1]
    tm = TOKEN_TILE
    nt = s // tm
    assert s % tm == 0 and tm % SUB_TILE == 0 and d_ff % FFN_CHUNK == 0
    assert s % BACK_TILE == 0 and BACK_TILE % BACK_SUB == 0
    assert s % (2 * ATT_BLOCK) == 0
    assert ffn1_w_gate.shape[0] == 1, "single layer"

    row = lambda v: v.reshape(1, -1).astype(F32)
    bf = lambda w: w[0].astype(BF16)
    n_groups = QK_W // QK_DIM
    gid = jnp.arange(GMAT_W) // QK_DIM
    gmat = (gid[:, None] == gid[None, :]).astype(BF16)
    log2e = math.log2(math.e)
    q_gain = jnp.tile(q_norm[0].astype(F32), n_groups).reshape(1, QK_W) * (QK_DIM ** -0.5 * log2e)
    k_gain = jnp.tile(k_norm[0].astype(F32), n_groups).reshape(1, QK_W)
    bias_cols = jnp.asarray(_alibi_columns(s), dtype=BF16)
    sub_gain = row(attn_subln[0]) * (1.0 - LAM_INIT)
    lam = (jnp.exp(jnp.sum(lambda_q1[0].astype(F32) * lambda_k1[0].astype(F32)))
           - jnp.exp(jnp.sum(lambda_q2[0].astype(F32) * lambda_k2[0].astype(F32)))
           + LAM_INIT).reshape(1)

    tok = lambda w: pl.BlockSpec((1, tm, w), lambda bi, ti: (bi, ti, 0))
    tok_t = lambda r: pl.BlockSpec((1, r, tm), lambda bi, ti: (bi, 0, ti))
    params = pltpu.CompilerParams(dimension_semantics=("arbitrary", "arbitrary"),
                                  vmem_limit_bytes=VMEM_LIMIT)

    later = [ffn2_w_gate[0], ffn2_w_up[0], ffn2_w_down[0], w_out[0]]
    n_steps = b * nt

    def cast_spec(w):
        n_rows = w.shape[0]
        rows = next(r for r in range(BF16_ROWS, n_rows + 1, BF16_ROWS)
                    if n_rows % r == 0 and n_rows // r <= n_steps)
        last = n_rows // rows - 1
        return pl.BlockSpec((rows, w.shape[1]),
                            lambda bi, ti: (jnp.minimum(bi * nt + ti, last), 0))

    outs = pl.pallas_call(
        _front_kernel,
        grid=(b, nt),
        in_specs=[tok(d), pl.BlockSpec((tm, N_HEADS * HEAD_COLS), lambda bi, ti: (ti, 0)),
                  _resident((1, d)), _resident((d, d_ff)), _resident((d, d_ff)),
                  _resident((d_ff, d)), _resident((1, d)), _resident((d, N_PROJ * QK_W)),
                  _resident((GMAT_W, GMAT_W)), _resident((1, QK_W)), _resident((1, QK_W)),
                  _resident((CONV_K, CONV_W)), _resident((1, CONV_W))]
                 + [cast_spec(w) for w in later],
        out_specs=[tok(d), tok_t(QK_W), tok(N_HEADS * K_AUG), tok_t(N_HEADS * V_AUG), tok(CONV_W)]
                  + [cast_spec(w) for w in later],
        out_shape=[jax.ShapeDtypeStruct((b, s, d), F32),
                   jax.ShapeDtypeStruct((b, QK_W, s), BF16),
                   jax.ShapeDtypeStruct((b, s, N_HEADS * K_AUG), BF16),
                   jax.ShapeDtypeStruct((b, N_HEADS * V_AUG, s), BF16),
                   jax.ShapeDtypeStruct((b, s, CONV_W), BF16)]
                  + [jax.ShapeDtypeStruct(w.shape, BF16) for w in later],
        scratch_shapes=[pltpu.VMEM((tm, d), BF16),
                        pltpu.VMEM((tm, d_ff), BF16),
                        pltpu.VMEM((tm, d), BF16),
                        pltpu.VMEM((tm + CARRY_ROWS, CONV_W), F32),
                        pltpu.VMEM((tm, CONV_W), F32)],
        compiler_params=params,
        name="front",
    )(x, bias_cols, row(ffn1_norm[0]), bf(ffn1_w_gate), bf(ffn1_w_up), bf(ffn1_w_down),
      row(mix_norm[0]), bf(w_in), gmat, q_gain, k_gain,
      conv_w[0].astype(F32), row(conv_norm[0]), *later)
    x1, q_t, k_n, v_t, c_n, w2_gate, w2_up, w2_down, w_out_bf = outs

    blk = ATT_BLOCK
    attn = pl.pallas_call(
        _attn_kernel,
        grid=(b,),
        in_specs=[pl.BlockSpec(memory_space=pltpu.SMEM),
                  pl.BlockSpec((1, QK_W, s), lambda bi: (bi, 0, 0)),
                  pl.BlockSpec((1, s, N_HEADS * K_AUG), lambda bi: (bi, 0, 0)),
                  pl.BlockSpec((1, N_HEADS * V_AUG, s), lambda bi: (bi, 0, 0)),
                  _resident((1, V_DIM))],
        out_specs=pl.BlockSpec((1, s, ATTN_W), lambda bi: (bi, 0, 0)),
        out_shape=jax.ShapeDtypeStruct((b, s, ATTN_W), BF16),
        scratch_shapes=[pltpu.VMEM((2, N_HEADS, K_AUG, 4 * blk), BF16),
                        pltpu.VMEM((blk, 2 * blk), F32),
                        pltpu.VMEM((N_HEADS, V_AUG, 4 * blk), F32),
                        pltpu.VMEM((2, N_HEADS, blk, 4 * blk), F32)],
        compiler_params=pltpu.CompilerParams(dimension_semantics=("arbitrary",),
                                             vmem_limit_bytes=VMEM_LIMIT),
        name="diff_attn",
    )(lam, q_t, k_n, v_t, sub_gain)

    tb = BACK_TILE
    tok_b = lambda w: pl.BlockSpec((1, tb, w), lambda bi, ti: (bi, ti, 0))
    out = pl.pallas_call(
        _back_kernel,
        grid=(b, s // tb),
        in_specs=[tok_b(d), tok_b(ATTN_W), tok_b(CONV_W), _resident((ATTN_W + CONV_W, d)),
                  _resident((1, d)), _resident((d, d_ff)), _resident((d, d_ff)),
                  _resident((d_ff, d)), _resident((1, d))],
        out_specs=tok_b(d),
        out_shape=jax.ShapeDtypeStruct((b, s, d), F32),
        scratch_shapes=[pltpu.VMEM((tb, d), F32),
                        pltpu.VMEM((tb, d), BF16),
                        pltpu.VMEM((tb, d_ff), BF16)],
        compiler_params=params,
        name="back",
    )(x1, attn, c_n, w_out_bf, row(ffn2_norm[0]), w2_gate, w2_up, w2_down, row(final_norm[0]))
    return out
```

```python
import math

import jax
import jax.numpy as jnp
import numpy as np
from jax import lax
from jax.experimental import pallas as pl
from jax.experimental.pallas import tpu as pltpu

F32 = jnp.float32
BF16 = jnp.bfloat16

N_HEADS = 4
QK_DIM = 64
V_DIM = 128
HEAD_COLS = 2 * QK_DIM
K_AUG = 2 * HEAD_COLS
N_BIAS_COLS = 3
BF16_ROWS = 16
V_PAD = BF16_ROWS
V_AUG = V_DIM + V_PAD
ATTN_W = N_HEADS * V_DIM
CONV_W = 512
QK_W = N_HEADS * HEAD_COLS
N_PROJ = 6
CONV_GROUP = 64
MXU_W = 256
GMAT_W = MXU_W
CONV_K = 3
NORM_EPS = 1e-6
FFN_RES_W = 0.5
LAM_INIT = 0.8 - 0.6 * math.exp(-0.3 * 0)
NEG_BIG = -1e30

FFN_CHUNK = 256
TOKEN_TILE = 512
BACK_TILE = 1024
SUB_TILE = 256
BACK_SUB = 512
NORM_ROWS = 32
ATT_BLOCK = 256
CARRY_ROWS = 8
VMEM_LIMIT = 60000 * 1024


def _rms(x, gain):
    ms = jnp.mean(x * x, axis=-1, keepdims=True)
    return x * lax.rsqrt(ms + NORM_EPS) * gain


def _group_rms_scale(t, gmat_ref, group):
    t2 = (t * t).astype(BF16)
    w = gmat_ref.shape[0]
    ss = jnp.concatenate(
        [jnp.dot(t2[:, c:c + w], gmat_ref[...], preferred_element_type=F32)
         for c in range(0, t.shape[1], w)], axis=1)
    return lax.rsqrt(ss * (1.0 / group) + NORM_EPS)


def _norm_jobs(src_ref, gain_ref, dst_ref, r0, sub, dtype):
    def job(g):
        rows = slice(r0 + g * NORM_ROWS, r0 + (g + 1) * NORM_ROWS)
        dst_ref[rows, :] = _rms(src_ref[rows, :], gain_ref[...]).astype(dtype)
    return [lambda g=g: job(g) for g in range(sub // NORM_ROWS)]


def _run(jobs):
    for job in jobs:
        job()


def _gate_up(h_buf, wg_ref, wu_ref, a_buf, r0, sub, side_jobs):
    rows = slice(r0, r0 + sub)
    side_jobs = list(side_jobs)
    n_chunks = wg_ref.shape[1] // FFN_CHUNK
    done = 0
    for c in range(n_chunks):
        cols = slice(c * FFN_CHUNK, (c + 1) * FFN_CHUNK)
        h = h_buf[rows, :]
        g = jnp.dot(h, wg_ref[:, cols], preferred_element_type=F32)
        u = jnp.dot(h, wu_ref[:, cols], preferred_element_type=F32)
        a_buf[rows, cols] = (g * jax.nn.sigmoid(g) * u).astype(BF16)
        upto = -(-(c + 1) * len(side_jobs) // n_chunks)
        _run(side_jobs[done:upto])
        done = upto


def _front_kernel(x_ref, bias_ref, g1_ref, wg_ref, wu_ref, wd_ref, g2_ref, win_ref,
                  gmat_ref, qg_ref, kg_ref, cw_ref, cg_ref,
                  later_w0, later_w1, later_w2, later_w3,
                  x1_ref, qt_ref, k_ref, vt_ref, c_ref,
                  later_o0, later_o1, later_o2, later_o3,
                  h_buf, a_buf, h2_buf, u_buf, gb_buf):
    tm = x_ref.shape[1]
    x_rows, x1_rows = x_ref.at[0], x1_ref.at[0]
    sub_tiles = list(range(0, tm, SUB_TILE))

    @pl.when(pl.program_id(1) == 0)
    def _():
        u_buf[0:CARRY_ROWS, :] = jnp.zeros((CARRY_ROWS, CONV_W), F32)

    def down(r0):
        rows = slice(r0, r0 + SUB_TILE)
        x1_ref[0, rows, :] = x_ref[0, rows, :] + FFN_RES_W * jnp.dot(
            a_buf[rows, :], wd_ref[...], preferred_element_type=F32)

    def proj(r0, j):
        return jnp.dot(h2_buf[r0:r0 + SUB_TILE, :], win_ref[:, j * QK_W:(j + 1) * QK_W],
                       preferred_element_type=F32)

    def q_job(r0):
        q = proj(r0, 0)
        q = q * _group_rms_scale(q, gmat_ref, QK_DIM) * qg_ref[...]
        qt_ref[0, :, r0:r0 + SUB_TILE] = q.T.astype(BF16)

    def k_job(r0):
        rows = slice(r0, r0 + SUB_TILE)
        k = proj(r0, 1)
        k = (k * _group_rms_scale(k, gmat_ref, QK_DIM) * kg_ref[...]).astype(BF16)
        for hd in range(N_HEADS):
            k_ref[0, rows, hd * K_AUG:hd * K_AUG + HEAD_COLS] = (
                k[:, hd * HEAD_COLS:(hd + 1) * HEAD_COLS])
            k_ref[0, rows, hd * K_AUG + HEAD_COLS:(hd + 1) * K_AUG] = (
                bias_ref[rows, hd * HEAD_COLS:(hd + 1) * HEAD_COLS])

    def v_job(r0):
        rows = slice(r0, r0 + SUB_TILE)
        vt = proj(r0, 2).T.astype(BF16)
        ones_rows = jnp.where(lax.broadcasted_iota(jnp.int32, (V_PAD, SUB_TILE), 0) == 0, 1.0, 0.0)
        for hd in range(N_HEADS):
            vt_ref[0, hd * V_AUG:hd * V_AUG + V_DIM, rows] = vt[hd * V_DIM:(hd + 1) * V_DIM, :]
            vt_ref[0, hd * V_AUG + V_DIM:(hd + 1) * V_AUG, rows] = ones_rows.astype(BF16)

    def conv_in_job(r0):
        gb_buf[r0:r0 + SUB_TILE, :] = proj(r0, 3)
        u0 = CARRY_ROWS + r0
        u_buf[u0:u0 + SUB_TILE, :] = proj(r0, 4) * proj(r0, 5)

    def conv_out_job(r0):
        u0 = CARRY_ROWS + r0
        y = (cw_ref[0:1, :] * u_buf[u0 - 2:u0 - 2 + SUB_TILE, :]
             + cw_ref[1:2, :] * u_buf[u0 - 1:u0 - 1 + SUB_TILE, :]
             + cw_ref[2:3, :] * u_buf[u0:u0 + SUB_TILE, :])
        c = gb_buf[r0:r0 + SUB_TILE, :] * y
        c = c * _group_rms_scale(c, gmat_ref, CONV_GROUP) * cg_ref[...]
        c_ref[0, r0:r0 + SUB_TILE, :] = c.astype(BF16)

    def mixer_jobs(r0):
        return [lambda: q_job(r0), lambda: k_job(r0), lambda: v_job(r0),
                lambda: conv_in_job(r0), lambda: conv_out_job(r0)]

    def cast_job(w_ref, o_ref):
        o_ref[...] = w_ref[...].astype(BF16)

    _run(_norm_jobs(x_rows, g1_ref, h_buf, sub_tiles[0], SUB_TILE, BF16))
    pending = [lambda: cast_job(later_w0, later_o0), lambda: cast_job(later_w1, later_o1),
               lambda: cast_job(later_w2, later_o2), lambda: cast_job(later_w3, later_o3)]
    for i, r0 in enumerate(sub_tiles):
        nxt = sub_tiles[i + 1] if i + 1 < len(sub_tiles) else None
        side = ((_norm_jobs(x_rows, g1_ref, h_buf, nxt, SUB_TILE, BF16) if nxt is not None else [])
                + pending)
        _gate_up(h_buf, wg_ref, wu_ref, a_buf, r0, SUB_TILE, side)
        down(r0)
        pending = _norm_jobs(x1_rows, g2_ref, h2_buf, r0, SUB_TILE, BF16) + mixer_jobs(r0)
    _run(pending)

    u_buf[0:CARRY_ROWS, :] = u_buf[tm:tm + CARRY_ROWS, :]


def _attn_kernel(lam_ref, qt_ref, k_ref, vt_ref, sg_ref, o_ref,
                 rhs_buf, mask_buf, acc_buf, s_buf):
    blk = ATT_BLOCK
    half = 2 * blk
    seq = k_ref.shape[1]
    lam = lam_ref[0]
    heads = range(N_HEADS)
    all_cols, first, second = slice(0, 2 * half), slice(0, half), slice(half, 2 * half)

    row = lax.broadcasted_iota(jnp.int32, (blk, half), 0)
    col = lax.broadcasted_iota(jnp.int32, (blk, half), 1)
    col = jnp.where(col >= blk, col - blk, col)
    mask_buf[...] = jnp.where(row <= col, 0.0, NEG_BIG)

    rhs_buf[...] = jnp.zeros(rhs_buf.shape, BF16)
    for par in range(2):
        for h in heads:
            rhs_buf[par, h, HEAD_COLS:HEAD_COLS + BF16_ROWS, :] = jnp.where(
                lax.broadcasted_iota(jnp.int32, (BF16_ROWS, 2 * half), 0) < N_BIAS_COLS, 1.0, 0.0
            ).astype(BF16)

    def block_start(kb):
        return kb * blk if isinstance(kb, int) else pl.multiple_of(kb * blk, blk)

    def build_rhs(qb, par, h):
        q0 = block_start(2 * qb)
        r0 = h * HEAD_COLS
        for part in range(2):
            qs = pl.ds(q0 + part * blk, blk)
            c0 = part * half
            rhs_buf[par, h, 0:QK_DIM, c0:c0 + blk] = qt_ref[0, r0:r0 + QK_DIM, qs]
            rhs_buf[par, h, QK_DIM:HEAD_COLS, c0 + blk:c0 + half] = (
                qt_ref[0, r0 + QK_DIM:r0 + HEAD_COLS, qs])

    def scores(kb, slot, par, h, cols, want_max=True):
        s = jnp.dot(k_ref[0, pl.ds(block_start(kb), blk), h * K_AUG:(h + 1) * K_AUG],
                    rhs_buf[par, h, :, cols], preferred_element_type=F32)
        s_buf[slot, h, :, cols] = s
        return jnp.max(s, axis=0, keepdims=True) if want_max else None

    def masked_probs(slot, h, cols, m):
        hb = blk // 2
        live = [slice(hb, blk), slice(blk + hb, 2 * blk)]
        t_top = s_buf[slot, h, 0:hb, cols] + mask_buf[0:hb, :]
        t_bot = [s_buf[slot, h, hb:blk, cols.start + c.start:cols.start + c.stop]
                 + mask_buf[hb:blk, c] for c in live]
        dead_max = jnp.full((1, hb), NEG_BIG, F32)
        bot_max = jnp.concatenate([dead_max, jnp.max(t_bot[0], axis=0, keepdims=True),
                                   dead_max, jnp.max(t_bot[1], axis=0, keepdims=True)], axis=1)
        m_new = jnp.maximum(m, jnp.maximum(jnp.max(t_top, axis=0, keepdims=True), bot_max))
        dead_p = jnp.zeros((hb, hb), BF16)
        p_bot = jnp.concatenate(
            [dead_p, jnp.exp2(t_bot[0] - m_new[:, live[0]]).astype(BF16),
             dead_p, jnp.exp2(t_bot[1] - m_new[:, live[1]]).astype(BF16)], axis=1)
        p = jnp.concatenate([jnp.exp2(t_top - m_new).astype(BF16), p_bot], axis=0)
        return m_new, p

    def softmax_pv(kb, slot, h, cols, m, tile_max, masked):
        if masked:
            m_new, p = masked_probs(slot, h, cols, m)
        else:
            m_new = jnp.maximum(m, tile_max)
            p = jnp.exp2(s_buf[slot, h, :, cols] - m_new).astype(BF16)
        alpha = jnp.exp2(m - m_new)
        pv = jnp.dot(vt_ref[0, h * V_AUG:(h + 1) * V_AUG, pl.ds(block_start(kb), blk)], p,
                     preferred_element_type=F32)
        acc_buf[h, :, cols] = acc_buf[h, :, cols] * alpha + pv
        return m_new

    def finalize(qb, h):
        inv_l = 1.0 / acc_buf[h, V_DIM:V_DIM + 1, :]
        for part in range(2):
            c0 = part * half
            a = (acc_buf[h, 0:V_DIM, c0:c0 + blk] * inv_l[:, c0:c0 + blk]
                 - acc_buf[h, 0:V_DIM, c0 + blk:c0 + half]
                 * (lam * inv_l[:, c0 + blk:c0 + half]))
            ms_a = jnp.mean(a * a, axis=0, keepdims=True)
            a = a * lax.rsqrt(ms_a + NORM_EPS)
            o_ref[0, pl.ds(block_start(2 * qb + part), blk), h * V_DIM:(h + 1) * V_DIM] = (
                (a.T * sg_ref[...]).astype(BF16))

    def q_block(qb, tms):
        par = qb & 1
        acc_buf[...] = jnp.zeros(acc_buf.shape, F32)
        ms = (jnp.full((1, 2 * half), NEG_BIG, F32),) * N_HEADS

        def full_step(kb, slot, carry):
            ms, tms = carry
            new_ms, new_tms = [], []
            for h in heads:
                new_tms.append(scores(kb + 1, 1 - slot, par, h, all_cols))
                new_ms.append(softmax_pv(kb, slot, h, all_cols, ms[h], tms[h], masked=False))
            return tuple(new_ms), tuple(new_tms)

        def pair(kp, c):
            return full_step(2 * kp + 1, 1, full_step(2 * kp, 0, c))

        ms, tms = lax.fori_loop(0, qb // 2, lambda kq, c: pair(2 * kq + 1, pair(2 * kq, c)),
                                (ms, tms))
        ms, tms = lax.cond(qb % 2 == 1, lambda c: pair(qb - 1, c), lambda c: c, (ms, tms))

        for h in heads:
            scores(2 * qb + 1, 1, par, h, second, want_max=False)
            softmax_pv(2 * qb, 0, h, first, ms[h][:, first], None, masked=True)
            m_b = softmax_pv(2 * qb, 0, h, second, ms[h][:, second], tms[h][:, second],
                             masked=False)
            softmax_pv(2 * qb + 1, 1, h, second, m_b, None, masked=True)

        nxt = jnp.minimum(qb + 1, n_qb - 1)
        new_tms = []
        for h in heads:
            build_rhs(nxt, 1 - par, h)
            new_tms.append(scores(0, 0, 1 - par, h, all_cols))
            finalize(qb, h)
        return tuple(new_tms)

    n_qb = seq // (2 * blk)
    for h in heads:
        build_rhs(0, 0, h)
    lax.fori_loop(0, n_qb, q_block, tuple(scores(0, 0, 0, h, all_cols) for h in heads))


def _back_kernel(x1_ref, a_ref, c_ref, wo_ref, g3_ref, wg_ref, wu_ref, wd_ref, gf_ref,
                 out_ref, x2_buf, h_buf, a_buf):
    tm = x1_ref.shape[1]
    out_rows = out_ref.at[0]
    sub_tiles = list(range(0, tm, BACK_SUB))

    def out_proj(r0):
        rows = slice(r0, r0 + BACK_SUB)
        x2_buf[rows, :] = (
            x1_ref[0, rows, :]
            + jnp.dot(a_ref[0, rows, :], wo_ref[0:ATTN_W, :], preferred_element_type=F32)
            + jnp.dot(c_ref[0, rows, :], wo_ref[ATTN_W:ATTN_W + CONV_W, :],
                      preferred_element_type=F32))

    def down(r0):
        rows = slice(r0, r0 + BACK_SUB)
        out_ref[0, rows, :] = x2_buf[rows, :] + FFN_RES_W * jnp.dot(
            a_buf[rows, :], wd_ref[...], preferred_element_type=F32)

    out_proj(sub_tiles[0])
    _run(_norm_jobs(x2_buf, g3_ref, h_buf, sub_tiles[0], BACK_SUB, BF16))
    pending = []
    for i, r0 in enumerate(sub_tiles):
        nxt = sub_tiles[i + 1] if i + 1 < len(sub_tiles) else None
        side = list(pending)
        if nxt is not None:
            side = ([lambda nxt=nxt: out_proj(nxt)] + side
                    + _norm_jobs(x2_buf, g3_ref, h_buf, nxt, BACK_SUB, BF16))
        _gate_up(h_buf, wg_ref, wu_ref, a_buf, r0, BACK_SUB, side)
        down(r0)
        pending = _norm_jobs(out_rows, gf_ref, out_rows, r0, BACK_SUB, F32)
    _run(pending)


def _alibi_columns(seq):
    slopes = np.exp2(-8.0 * np.arange(1, N_HEADS + 1) / N_HEADS) * math.log2(math.e)
    rest = (slopes[None, :] * np.arange(seq)[:, None]).astype(np.float32)
    table = np.zeros((seq, N_HEADS, HEAD_COLS), np.float32)
    for c in range(N_BIAS_COLS):
        piece = (rest.view(np.uint32) & np.uint32(0xFFFF0000)).view(np.float32)
        table[:, :, c] = piece
        rest = rest - piece
    return table.reshape(seq, N_HEADS * HEAD_COLS)


def _resident(shape):
    return pl.BlockSpec(shape, lambda *_: (0,) * len(shape), pipeline_mode=pl.Buffered(1))


def kernel(x, ffn1_norm, ffn1_w_gate, ffn1_w_up, ffn1_w_down, mix_norm, w_in, q_norm, k_norm,
           lambda_q1, lambda_k1, lambda_q2, lambda_k2, attn_subln, conv_w, conv_norm, w_out,
           ffn2_norm, ffn2_w_gate, ffn2_w_up, ffn2_w_down, final_norm):
    b, s, d = x.shape
    d_ff = ffn1_w_gate.shape[-1]
    tm = TOKEN_TILE
    nt = s // tm
    assert s % tm == 0 and tm % SUB_TILE == 0 and d_ff % FFN_CHUNK == 0
    assert s % BACK_TILE == 0 and BACK_TILE % BACK_SUB == 0
    assert s % (2 * ATT_BLOCK) == 0
    assert ffn1_w_gate.shape[0] == 1, "single layer"

    row = lambda v: v.reshape(1, -1).astype(F32)
    bf = lambda w: w[0].astype(BF16)
    n_groups = QK_W // QK_DIM
    gid = jnp.arange(GMAT_W) // QK_DIM
    gmat = (gid[:, None] == gid[None, :]).astype(BF16)
    log2e = math.log2(math.e)
    q_gain = jnp.tile(q_norm[0].astype(F32), n_groups).reshape(1, QK_W) * (QK_DIM ** -0.5 * log2e)
    k_gain = jnp.tile(k_norm[0].astype(F32), n_groups).reshape(1, QK_W)
    bias_cols = jnp.asarray(_alibi_columns(s), dtype=BF16)
    sub_gain = row(attn_subln[0]) * (1.0 - LAM_INIT)
    lam = (jnp.exp(jnp.sum(lambda_q1[0].astype(F32) * lambda_k1[0].astype(F32)))
           - jnp.exp(jnp.sum(lambda_q2[0].astype(F32) * lambda_k2[0].astype(F32)))
           + LAM_INIT).reshape(1)

    tok = lambda w: pl.BlockSpec((1, tm, w), lambda bi, ti: (bi, ti, 0))
    tok_t = lambda r: pl.BlockSpec((1, r, tm), lambda bi, ti: (bi, 0, ti))
    params = pltpu.CompilerParams(dimension_semantics=("arbitrary", "arbitrary"),
                                  vmem_limit_bytes=VMEM_LIMIT)

    later = [ffn2_w_gate[0], ffn2_w_up[0], ffn2_w_down[0], w_out[0]]
    n_steps = b * nt

    def cast_spec(w):
        n_rows = w.shape[0]
        rows = next(r for r in range(BF16_ROWS, n_rows + 1, BF16_ROWS)
                    if n_rows % r == 0 and n_rows // r <= n_steps)
        last = n_rows // rows - 1
        return pl.BlockSpec((rows, w.shape[1]),
                            lambda bi, ti: (jnp.minimum(bi * nt + ti, last), 0))

    outs = pl.pallas_call(
        _front_kernel,
        grid=(b, nt),
        in_specs=[tok(d), pl.BlockSpec((tm, N_HEADS * HEAD_COLS), lambda bi, ti: (ti, 0)),
                  _resident((1, d)), _resident((d, d_ff)), _resident((d, d_ff)),
                  _resident((d_ff, d)), _resident((1, d)), _resident((d, N_PROJ * QK_W)),
                  _resident((GMAT_W, GMAT_W)), _resident((1, QK_W)), _resident((1, QK_W)),
                  _resident((CONV_K, CONV_W)), _resident((1, CONV_W))]
                 + [cast_spec(w) for w in later],
        out_specs=[tok(d), tok_t(QK_W), tok(N_HEADS * K_AUG), tok_t(N_HEADS * V_AUG), tok(CONV_W)]
                  + [cast_spec(w) for w in later],
        out_shape=[jax.ShapeDtypeStruct((b, s, d), F32),
                   jax.ShapeDtypeStruct((b, QK_W, s), BF16),
                   jax.ShapeDtypeStruct((b, s, N_HEADS * K_AUG), BF16),
                   jax.ShapeDtypeStruct((b, N_HEADS * V_AUG, s), BF16),
                   jax.ShapeDtypeStruct((b, s, CONV_W), BF16)]
                  + [jax.ShapeDtypeStruct(w.shape, BF16) for w in later],
        scratch_shapes=[pltpu.VMEM((tm, d), BF16),
                        pltpu.VMEM((tm, d_ff), BF16),
                        pltpu.VMEM((tm, d), BF16),
                        pltpu.VMEM((tm + CARRY_ROWS, CONV_W), F32),
                        pltpu.VMEM((tm, CONV_W), F32)],
        compiler_params=params,
        name="front",
    )(x, bias_cols, row(ffn1_norm[0]), bf(ffn1_w_gate), bf(ffn1_w_up), bf(ffn1_w_down),
      row(mix_norm[0]), bf(w_in), gmat, q_gain, k_gain,
      conv_w[0].astype(F32), row(conv_norm[0]), *later)
    x1, q_t, k_n, v_t, c_n, w2_gate, w2_up, w2_down, w_out_bf = outs

    blk = ATT_BLOCK
    attn = pl.pallas_call(
        _attn_kernel,
        grid=(b,),
        in_specs=[pl.BlockSpec(memory_space=pltpu.SMEM),
                  pl.BlockSpec((1, QK_W, s), lambda bi: (bi, 0, 0)),
                  pl.BlockSpec((1, s, N_HEADS * K_AUG), lambda bi: (bi, 0, 0)),
                  pl.BlockSpec((1, N_HEADS * V_AUG, s), lambda bi: (bi, 0, 0)),
                  _resident((1, V_DIM))],
        out_specs=pl.BlockSpec((1, s, ATTN_W), lambda bi: (bi, 0, 0)),
        out_shape=jax.ShapeDtypeStruct((b, s, ATTN_W), BF16),
        scratch_shapes=[pltpu.VMEM((2, N_HEADS, K_AUG, 4 * blk), BF16),
                        pltpu.VMEM((blk, 2 * blk), F32),
                        pltpu.VMEM((N_HEADS, V_AUG, 4 * blk), F32),
                        pltpu.VMEM((2, N_HEADS, blk, 4 * blk), F32)],
        compiler_params=pltpu.CompilerParams(dimension_semantics=("arbitrary",),
                                             vmem_limit_bytes=VMEM_LIMIT),
        name="diff_attn",
    )(lam, q_t, k_n, v_t, sub_gain)

    tb = BACK_TILE
    tok_b = lambda w: pl.BlockSpec((1, tb, w), lambda bi, ti: (bi, ti, 0))
    out = pl.pallas_call(
        _back_kernel,
        grid=(b, s // tb),
        in_specs=[tok_b(d), tok_b(ATTN_W), tok_b(CONV_W), _resident((ATTN_W + CONV_W, d)),
                  _resident((1, d)), _resident((d, d_ff)), _resident((d, d_ff)),
                  _resident((d_ff, d)), _resident((1, d))],
        out_specs=tok_b(d),
        out_shape=jax.ShapeDtypeStruct((b, s, d), F32),
        scratch_shapes=[pltpu.VMEM((tb, d), F32),
                        pltpu.VMEM((tb, d), BF16),
                        pltpu.VMEM((tb, d_ff), BF16)],
        compiler_params=params,
        name="back",
    )(x1, attn, c_n, w_out_bf, row(ffn2_norm[0]), w2_gate, w2_up, w2_down, row(final_norm[0]))
    return out
```

```python
import math

import jax
import jax.numpy as jnp
import numpy as np
from jax import lax
from jax.experimental import pallas as pl
from jax.experimental.pallas import tpu as pltpu

F32 = jnp.float32
BF16 = jnp.bfloat16

N_HEADS = 4
QK_DIM = 64
V_DIM = 128
HEAD_COLS = 2 * QK_DIM
K_AUG = 2 * HEAD_COLS
N_BIAS_COLS = 3
BF16_ROWS = 16
V_PAD = BF16_ROWS
V_AUG = V_DIM + V_PAD
ATTN_W = N_HEADS * V_DIM
CONV_W = 512
QK_W = N_HEADS * HEAD_COLS
N_PROJ = 6
CONV_GROUP = 64
MXU_W = 256
GMAT_W = MXU_W
CONV_K = 3
NORM_EPS = 1e-6
FFN_RES_W = 0.5
LAM_INIT = 0.8 - 0.6 * math.exp(-0.3 * 0)
NEG_BIG = -1e30

FFN_CHUNK = 256
TOKEN_TILE = 512
BACK_TILE = 1024
SUB_TILE = 256
BACK_SUB = 512
NORM_ROWS = 32
ATT_BLOCK = 256
CARRY_ROWS = 8
VMEM_LIMIT = 60000 * 1024


def _rms(x, gain):
    ms = jnp.mean(x * x, axis=-1, keepdims=True)
    return x * lax.rsqrt(ms + NORM_EPS) * gain


def _group_rms_scale(t, gmat_ref, group):
    t2 = (t * t).astype(BF16)
    w = gmat_ref.shape[0]
    ss = jnp.concatenate(
        [jnp.dot(t2[:, c:c + w], gmat_ref[...], preferred_element_type=F32)
         for c in range(0, t.shape[1], w)], axis=1)
    return lax.rsqrt(ss * (1.0 / group) + NORM_EPS)


def _norm_jobs(src_ref, gain_ref, dst_ref, r0, sub, dtype):
    def job(g):
        rows = slice(r0 + g * NORM_ROWS, r0 + (g + 1) * NORM_ROWS)
        dst_ref[rows, :] = _rms(src_ref[rows, :], gain_ref[...]).astype(dtype)
    return [lambda g=g: job(g) for g in range(sub // NORM_ROWS)]


def _run(jobs):
    for job in jobs:
        job()


def _gate_up(h_buf, wg_ref, wu_ref, a_buf, r0, sub, side_jobs):
    rows = slice(r0, r0 + sub)
    side_jobs = list(side_jobs)
    n_chunks = wg_ref.shape[1] // FFN_CHUNK
    done = 0
    for c in range(n_chunks):
        cols = slice(c * FFN_CHUNK, (c + 1) * FFN_CHUNK)
        h = h_buf[rows, :]
        g = jnp.dot(h, wg_ref[:, cols], preferred_element_type=F32)
        u = jnp.dot(h, wu_ref[:, cols], preferred_element_type=F32)
        a_buf[rows, cols] = (g * jax.nn.sigmoid(g) * u).astype(BF16)
        upto = -(-(c + 1) * len(side_jobs) // n_chunks)
        _run(side_jobs[done:upto])
        done = upto


def _front_kernel(x_ref, bias_ref, g1_ref, wg_ref, wu_ref, wd_ref, g2_ref, win_ref,
                  gmat_ref, qg_ref, kg_ref, cw_ref, cg_ref,
                  later_w0, later_w1, later_w2, later_w3,
                  x1_ref, qt_ref, k_ref, vt_ref, c_ref,
                  later_o0, later_o1, later_o2, later_o3,
                  h_buf, a_buf, h2_buf, u_buf, gb_buf):
    tm = x_ref.shape[1]
    x_rows, x1_rows = x_ref.at[0], x1_ref.at[0]
    sub_tiles = list(range(0, tm, SUB_TILE))

    @pl.when(pl.program_id(1) == 0)
    def _():
        u_buf[0:CARRY_ROWS, :] = jnp.zeros((CARRY_ROWS, CONV_W), F32)

    def down(r0):
        rows = slice(r0, r0 + SUB_TILE)
        x1_ref[0, rows, :] = x_ref[0, rows, :] + FFN_RES_W * jnp.dot(
            a_buf[rows, :], wd_ref[...], preferred_element_type=F32)

    def proj(r0, j):
        return jnp.dot(h2_buf[r0:r0 + SUB_TILE, :], win_ref[:, j * QK_W:(j + 1) * QK_W],
                       preferred_element_type=F32)

    def q_job(r0):
        q = proj(r0, 0)
        q = q * _group_rms_scale(q, gmat_ref, QK_DIM) * qg_ref[...]
        qt_ref[0, :, r0:r0 + SUB_TILE] = q.T.astype(BF16)

    def k_job(r0):
        rows = slice(r0, r0 + SUB_TILE)
        k = proj(r0, 1)
        k = (k * _group_rms_scale(k, gmat_ref, QK_DIM) * kg_ref[...]).astype(BF16)
        for hd in range(N_HEADS):
            k_ref[0, rows, hd * K_AUG:hd * K_AUG + HEAD_COLS] = (
                k[:, hd * HEAD_COLS:(hd + 1) * HEAD_COLS])
            k_ref[0, rows, hd * K_AUG + HEAD_COLS:(hd + 1) * K_AUG] = (
                bias_ref[rows, hd * HEAD_COLS:(hd + 1) * HEAD_COLS])

    def v_job(r0):
        rows = slice(r0, r0 + SUB_TILE)
        vt = proj(r0, 2).T.astype(BF16)
        ones_rows = jnp.where(lax.broadcasted_iota(jnp.int32, (V_PAD, SUB_TILE), 0) == 0, 1.0, 0.0)
        for hd in range(N_HEADS):
            vt_ref[0, hd * V_AUG:hd * V_AUG + V_DIM, rows] = vt[hd * V_DIM:(hd + 1) * V_DIM, :]
            vt_ref[0, hd * V_AUG + V_DIM:(hd + 1) * V_AUG, rows] = ones_rows.astype(BF16)

    def conv_in_job(r0):
        gb_buf[r0:r0 + SUB_TILE, :] = proj(r0, 3)
        u0 = CARRY_ROWS + r0
        u_buf[u0:u0 + SUB_TILE, :] = proj(r0, 4) * proj(r0, 5)

    def conv_out_job(r0):
        u0 = CARRY_ROWS + r0
        y = (cw_ref[0:1, :] * u_buf[u0 - 2:u0 - 2 + SUB_TILE, :]
             + cw_ref[1:2, :] * u_buf[u0 - 1:u0 - 1 + SUB_TILE, :]
             + cw_ref[2:3, :] * u_buf[u0:u0 + SUB_TILE, :])
        c = gb_buf[r0:r0 + SUB_TILE, :] * y
        c = c * _group_rms_scale(c, gmat_ref, CONV_GROUP) * cg_ref[...]
        c_ref[0, r0:r0 + SUB_TILE, :] = c.astype(BF16)

    def mixer_jobs(r0):
        return [lambda: q_job(r0), lambda: k_job(r0), lambda: v_job(r0),
                lambda: conv_in_job(r0), lambda: conv_out_job(r0)]

    def cast_job(w_ref, o_ref):
        o_ref[...] = w_ref[...].astype(BF16)

    _run(_norm_jobs(x_rows, g1_ref, h_buf, sub_tiles[0], SUB_TILE, BF16))
    pending = [lambda: cast_job(later_w0, later_o0), lambda: cast_job(later_w1, later_o1),
               lambda: cast_job(later_w2, later_o2), lambda: cast_job(later_w3, later_o3)]
    for i, r0 in enumerate(sub_tiles):
        nxt = sub_tiles[i + 1] if i + 1 < len(sub_tiles) else None
        side = ((_norm_jobs(x_rows, g1_ref, h_buf, nxt, SUB_TILE, BF16) if nxt is not None else [])
                + pending)
        _gate_up(h_buf, wg_ref, wu_ref, a_buf, r0, SUB_TILE, side)
        down(r0)
        pending = _norm_jobs(x1_rows, g2_ref, h2_buf, r0, SUB_TILE, BF16) + mixer_jobs(r0)
    _run(pending)

    u_buf[0:CARRY_ROWS, :] = u_buf[tm:tm + CARRY_ROWS, :]


def _attn_kernel(lam_ref, qt_ref, k_ref, vt_ref, sg_ref, o_ref,
                 rhs_buf, mask_buf, acc_buf, s_buf):
    blk = ATT_BLOCK
    half = 2 * blk
    seq = k_ref.shape[1]
    lam = lam_ref[0]
    heads = range(N_HEADS)
    all_cols, first, second = slice(0, 2 * half), slice(0, half), slice(half, 2 * half)

    @pl.when(pl.program_id(0) == 0)
    def _():
        row = lax.broadcasted_iota(jnp.int32, (blk, half), 0)
        col = lax.broadcasted_iota(jnp.int32, (blk, half), 1)
        col = jnp.where(col >= blk, col - blk, col)
        mask_buf[...] = jnp.where(row <= col, 0.0, NEG_BIG)

        rhs_buf[...] = jnp.zeros(rhs_buf.shape, BF16)
        for par in range(2):
            for h in heads:
                rhs_buf[par, h, HEAD_COLS:HEAD_COLS + BF16_ROWS, :] = jnp.where(
                    lax.broadcasted_iota(jnp.int32, (BF16_ROWS, 2 * half), 0) < N_BIAS_COLS,
                    1.0, 0.0).astype(BF16)

    def block_start(kb):
        return kb * blk if isinstance(kb, int) else pl.multiple_of(kb * blk, blk)

    def build_rhs(qb, par, h):
        q0 = block_start(2 * qb)
        r0 = h * HEAD_COLS
        for part in range(2):
            qs = pl.ds(q0 + part * blk, blk)
            c0 = part * half
            rhs_buf[par, h, 0:QK_DIM, c0:c0 + blk] = qt_ref[0, r0:r0 + QK_DIM, qs]
            rhs_buf[par, h, QK_DIM:HEAD_COLS, c0 + blk:c0 + half] = (
                qt_ref[0, r0 + QK_DIM:r0 + HEAD_COLS, qs])

    def scores(kb, slot, par, h, cols, want_max=True):
        s = jnp.dot(k_ref[0, pl.ds(block_start(kb), blk), h * K_AUG:(h + 1) * K_AUG],
                    rhs_buf[par, h, :, cols], preferred_element_type=F32)
        s_buf[slot, h, :, cols] = s
        return jnp.max(s, axis=0, keepdims=True) if want_max else None

    def masked_probs(slot, h, cols, m):
        hb = blk // 2
        live = [slice(hb, blk), slice(blk + hb, 2 * blk)]
        dead = [slice(0, hb), slice(blk, blk + hb)]
        s_top = s_buf[slot, h, 0:hb, cols]
        t_top = jnp.concatenate(
            [s_top[:, dead[0]] + mask_buf[0:hb, dead[0]], s_top[:, live[0]],
             s_top[:, dead[1]] + mask_buf[0:hb, dead[1]], s_top[:, live[1]]], axis=1)
        t_bot = [s_buf[slot, h, hb:blk, cols.start + c.start:cols.start + c.stop]
                 + mask_buf[hb:blk, c] for c in live]
        dead_max = jnp.full((1, hb), NEG_BIG, F32)
        bot_max = jnp.concatenate([dead_max, jnp.max(t_bot[0], axis=0, keepdims=True),
                                   dead_max, jnp.max(t_bot[1], axis=0, keepdims=True)], axis=1)
        m_new = jnp.maximum(m, jnp.maximum(jnp.max(t_top, axis=0, keepdims=True), bot_max))
        dead_p = jnp.zeros((hb, hb), BF16)
        p_bot = jnp.concatenate(
            [dead_p, jnp.exp2(t_bot[0] - m_new[:, live[0]]).astype(BF16),
             dead_p, jnp.exp2(t_bot[1] - m_new[:, live[1]]).astype(BF16)], axis=1)
        p = jnp.concatenate([jnp.exp2(t_top - m_new).astype(BF16), p_bot], axis=0)
        return m_new, p

    def softmax_pv(kb, slot, h, cols, m, tile_max, masked):
        if masked:
            m_new, p = masked_probs(slot, h, cols, m)
        else:
            m_new = jnp.maximum(m, tile_max)
            p = jnp.exp2(s_buf[slot, h, :, cols] - m_new).astype(BF16)
        alpha = jnp.exp2(m - m_new)
        pv = jnp.dot(vt_ref[0, h * V_AUG:(h + 1) * V_AUG, pl.ds(block_start(kb), blk)], p,
                     preferred_element_type=F32)
        acc_buf[h, :, cols] = acc_buf[h, :, cols] * alpha + pv
        return m_new

    def finalize(qb, h):
        inv_l = 1.0 / acc_buf[h, V_DIM:V_DIM + 1, :]
        for part in range(2):
            c0 = part * half
            a = (acc_buf[h, 0:V_DIM, c0:c0 + blk] * inv_l[:, c0:c0 + blk]
                 - acc_buf[h, 0:V_DIM, c0 + blk:c0 + half]
                 * (lam * inv_l[:, c0 + blk:c0 + half]))
            ms_a = jnp.mean(a * a, axis=0, keepdims=True)
            a = a * lax.rsqrt(ms_a + NORM_EPS)
            o_ref[0, pl.ds(block_start(2 * qb + part), blk), h * V_DIM:(h + 1) * V_DIM] = (
                (a.T * sg_ref[...]).astype(BF16))

    def q_block(qb, tms, odd):
        par = int(odd)
        acc_buf[...] = jnp.zeros(acc_buf.shape, F32)
        ms = (jnp.full((1, 2 * half), NEG_BIG, F32),) * N_HEADS

        def full_step(kb, slot, carry):
            ms, tms = carry
            new_ms, new_tms = [], []
            for h in heads:
                new_tms.append(scores(kb + 1, 1 - slot, par, h, all_cols))
                new_ms.append(softmax_pv(kb, slot, h, all_cols, ms[h], tms[h], masked=False))
            return tuple(new_ms), tuple(new_tms)

        def pair(kp, c):
            return full_step(2 * kp + 1, 1, full_step(2 * kp, 0, c))

        ms, tms = lax.fori_loop(0, qb // 2, lambda kq, c: pair(2 * kq + 1, pair(2 * kq, c)),
                                (ms, tms))
        if odd:
            ms, tms = pair(qb - 1, (ms, tms))

        for h in heads:
            scores(2 * qb + 1, 1, par, h, second, want_max=False)
            softmax_pv(2 * qb, 0, h, first, ms[h][:, first], None, masked=True)
            m_b = softmax_pv(2 * qb, 0, h, second, ms[h][:, second], tms[h][:, second],
                             masked=False)
            softmax_pv(2 * qb + 1, 1, h, second, m_b, None, masked=True)

        nxt = jnp.minimum(qb + 1, n_qb - 1)
        new_tms = []
        for h in heads:
            build_rhs(nxt, 1 - par, h)
            new_tms.append(scores(0, 0, 1 - par, h, all_cols))
            finalize(qb, h)
        return tuple(new_tms)

    n_qb = seq // (2 * blk)
    for h in heads:
        build_rhs(0, 0, h)
    lax.fori_loop(0, n_qb // 2,
                  lambda j, tms: q_block(2 * j + 1, q_block(2 * j, tms, odd=False), odd=True),
                  tuple(scores(0, 0, 0, h, all_cols) for h in heads))


def _back_kernel(x1_ref, a_ref, c_ref, wo_ref, g3_ref, wg_ref, wu_ref, wd_ref, gf_ref,
                 out_ref, x2_buf, h_buf, a_buf):
    tm = x1_ref.shape[1]
    out_rows = out_ref.at[0]
    sub_tiles = list(range(0, tm, BACK_SUB))

    def out_proj(r0):
        rows = slice(r0, r0 + BACK_SUB)
        x2_buf[rows, :] = (
            x1_ref[0, rows, :]
            + jnp.dot(a_ref[0, rows, :], wo_ref[0:ATTN_W, :], preferred_element_type=F32)
            + jnp.dot(c_ref[0, rows, :], wo_ref[ATTN_W:ATTN_W + CONV_W, :],
                      preferred_element_type=F32))

    def down(r0):
        rows = slice(r0, r0 + BACK_SUB)
        out_ref[0, rows, :] = x2_buf[rows, :] + FFN_RES_W * jnp.dot(
            a_buf[rows, :], wd_ref[...], preferred_element_type=F32)

    out_proj(sub_tiles[0])
    _run(_norm_jobs(x2_buf, g3_ref, h_buf, sub_tiles[0], BACK_SUB, BF16))
    pending = []
    for i, r0 in enumerate(sub_tiles):
        nxt = sub_tiles[i + 1] if i + 1 < len(sub_tiles) else None
        side = list(pending)
        if nxt is not None:
            side = ([lambda nxt=nxt: out_proj(nxt)] + side
                    + _norm_jobs(x2_buf, g3_ref, h_buf, nxt, BACK_SUB, BF16))
        _gate_up(h_buf, wg_ref, wu_ref, a_buf, r0, BACK_SUB, side)
        down(r0)
        pending = _norm_jobs(out_rows, gf_ref, out_rows, r0, BACK_SUB, F32)
    _run(pending)


def _alibi_columns(seq):
    slopes = np.exp2(-8.0 * np.arange(1, N_HEADS + 1) / N_HEADS) * math.log2(math.e)
    rest = (slopes[None, :] * np.arange(seq)[:, None]).astype(np.float32)
    table = np.zeros((seq, N_HEADS, HEAD_COLS), np.float32)
    for c in range(N_BIAS_COLS):
        piece = (rest.view(np.uint32) & np.uint32(0xFFFF0000)).view(np.float32)
        table[:, :, c] = piece
        rest = rest - piece
    return table.reshape(seq, N_HEADS * HEAD_COLS)


def _resident(shape):
    return pl.BlockSpec(shape, lambda *_: (0,) * len(shape), pipeline_mode=pl.Buffered(1))


def kernel(x, ffn1_norm, ffn1_w_gate, ffn1_w_up, ffn1_w_down, mix_norm, w_in, q_norm, k_norm,
           lambda_q1, lambda_k1, lambda_q2, lambda_k2, attn_subln, conv_w, conv_norm, w_out,
           ffn2_norm, ffn2_w_gate, ffn2_w_up, ffn2_w_down, final_norm):
    b, s, d = x.shape
    d_ff = ffn1_w_gate.shape[-1]
    tm = TOKEN_TILE
    nt = s // tm
    assert s % tm == 0 and tm % SUB_TILE == 0 and d_ff % FFN_CHUNK == 0
    assert s % BACK_TILE == 0 and BACK_TILE % BACK_SUB == 0
    assert s % (2 * ATT_BLOCK) == 0
    assert ffn1_w_gate.shape[0] == 1, "single layer"

    row = lambda v: v.reshape(1, -1).astype(F32)
    bf = lambda w: w[0].astype(BF16)
    n_groups = QK_W // QK_DIM
    gid = jnp.arange(GMAT_W) // QK_DIM
    gmat = (gid[:, None] == gid[None, :]).astype(BF16)
    log2e = math.log2(math.e)
    q_gain = jnp.tile(q_norm[0].astype(F32), n_groups).reshape(1, QK_W) * (QK_DIM ** -0.5 * log2e)
    k_gain = jnp.tile(k_norm[0].astype(F32), n_groups).reshape(1, QK_W)
    bias_cols = jnp.asarray(_alibi_columns(s), dtype=BF16)
    sub_gain = row(attn_subln[0]) * (1.0 - LAM_INIT)
    lam = (jnp.exp(jnp.sum(lambda_q1[0].astype(F32) * lambda_k1[0].astype(F32)))
           - jnp.exp(jnp.sum(lambda_q2[0].astype(F32) * lambda_k2[0].astype(F32)))
           + LAM_INIT).reshape(1)

    tok = lambda w: pl.BlockSpec((1, tm, w), lambda bi, ti: (bi, ti, 0))
    tok_t = lambda r: pl.BlockSpec((1, r, tm), lambda bi, ti: (bi, 0, ti))
    params = pltpu.CompilerParams(dimension_semantics=("arbitrary", "arbitrary"),
                                  vmem_limit_bytes=VMEM_LIMIT)

    later = [ffn2_w_gate[0], ffn2_w_up[0], ffn2_w_down[0], w_out[0]]
    n_steps = b * nt

    def cast_spec(w):
        n_rows = w.shape[0]
        rows = next(r for r in range(BF16_ROWS, n_rows + 1, BF16_ROWS)
                    if n_rows % r == 0 and n_rows // r <= n_steps)
        last = n_rows // rows - 1
        return pl.BlockSpec((rows, w.shape[1]),
                            lambda bi, ti: (jnp.minimum(bi * nt + ti, last), 0))

    outs = pl.pallas_call(
        _front_kernel,
        grid=(b, nt),
        in_specs=[tok(d), pl.BlockSpec((tm, N_HEADS * HEAD_COLS), lambda bi, ti: (ti, 0)),
                  _resident((1, d)), _resident((d, d_ff)), _resident((d, d_ff)),
                  _resident((d_ff, d)), _resident((1, d)), _resident((d, N_PROJ * QK_W)),
                  _resident((GMAT_W, GMAT_W)), _resident((1, QK_W)), _resident((1, QK_W)),
                  _resident((CONV_K, CONV_W)), _resident((1, CONV_W))]
                 + [cast_spec(w) for w in later],
        out_specs=[tok(d), tok_t(QK_W), tok(N_HEADS * K_AUG), tok_t(N_HEADS * V_AUG), tok(CONV_W)]
                  + [cast_spec(w) for w in later],
        out_shape=[jax.ShapeDtypeStruct((b, s, d), F32),
                   jax.ShapeDtypeStruct((b, QK_W, s), BF16),
                   jax.ShapeDtypeStruct((b, s, N_HEADS * K_AUG), BF16),
                   jax.ShapeDtypeStruct((b, N_HEADS * V_AUG, s), BF16),
                   jax.ShapeDtypeStruct((b, s, CONV_W), BF16)]
                  + [jax.ShapeDtypeStruct(w.shape, BF16) for w in later],
        scratch_shapes=[pltpu.VMEM((tm, d), BF16),
                        pltpu.VMEM((tm, d_ff), BF16),
                        pltpu.VMEM((tm, d), BF16),
                        pltpu.VMEM((tm + CARRY_ROWS, CONV_W), F32),
                        pltpu.VMEM((tm, CONV_W), F32)],
        compiler_params=params,
        name="front",
    )(x, bias_cols, row(ffn1_norm[0]), bf(ffn1_w_gate), bf(ffn1_w_up), bf(ffn1_w_down),
      row(mix_norm[0]), bf(w_in), gmat, q_gain, k_gain,
      conv_w[0].astype(F32), row(conv_norm[0]), *later)
    x1, q_t, k_n, v_t, c_n, w2_gate, w2_up, w2_down, w_out_bf = outs

    blk = ATT_BLOCK
    attn = pl.pallas_call(
        _attn_kernel,
        grid=(b,),
        in_specs=[pl.BlockSpec(memory_space=pltpu.SMEM),
                  pl.BlockSpec((1, QK_W, s), lambda bi: (bi, 0, 0)),
                  pl.BlockSpec((1, s, N_HEADS * K_AUG), lambda bi: (bi, 0, 0)),
                  pl.BlockSpec((1, N_HEADS * V_AUG, s), lambda bi: (bi, 0, 0)),
                  _resident((1, V_DIM))],
        out_specs=pl.BlockSpec((1, s, ATTN_W), lambda bi: (bi, 0, 0)),
        out_shape=jax.ShapeDtypeStruct((b, s, ATTN_W), BF16),
        scratch_shapes=[pltpu.VMEM((2, N_HEADS, K_AUG, 4 * blk), BF16),
                        pltpu.VMEM((blk, 2 * blk), F32),
                        pltpu.VMEM((N_HEADS, V_AUG, 4 * blk), F32),
                        pltpu.VMEM((2, N_HEADS, blk, 4 * blk), F32)],
        compiler_params=pltpu.CompilerParams(dimension_semantics=("arbitrary",),
                                             vmem_limit_bytes=VMEM_LIMIT),
        name="diff_attn",
    )(lam, q_t, k_n, v_t, sub_gain)

    tb = BACK_TILE
    tok_b = lambda w: pl.BlockSpec((1, tb, w), lambda bi, ti: (bi, ti, 0))
    out = pl.pallas_call(
        _back_kernel,
        grid=(b, s // tb),
        in_specs=[tok_b(d), tok_b(ATTN_W), tok_b(CONV_W), _resident((ATTN_W + CONV_W, d)),
                  _resident((1, d)), _resident((d, d_ff)), _resident((d, d_ff)),
                  _resident((d_ff, d)), _resident((1, d))],
        out_specs=tok_b(d),
        out_shape=jax.ShapeDtypeStruct((b, s, d), F32),
        scratch_shapes=[pltpu.VMEM((tb, d), F32),
                        pltpu.VMEM((tb, d), BF16),
                        pltpu.VMEM((tb, d_ff), BF16)],
        compiler_params=params,
        name="back",
    )(x1, attn, c_n, w_out_bf, row(ffn2_norm[0]), w2_gate, w2_up, w2_down, row(final_norm[0]))
    return out
```

```python
import math

import jax
import jax.numpy as jnp
import numpy as np
from jax import lax
from jax.experimental import pallas as pl
from jax.experimental.pallas import tpu as pltpu

F32 = jnp.float32
BF16 = jnp.bfloat16

N_HEADS = 4
QK_DIM = 64
V_DIM = 128
HEAD_COLS = 2 * QK_DIM
K_AUG = 2 * HEAD_COLS
N_BIAS_COLS = 3
BF16_ROWS = 16
V_PAD = BF16_ROWS
V_AUG = V_DIM + V_PAD
ATTN_W = N_HEADS * V_DIM
CONV_W = 512
QK_W = N_HEADS * HEAD_COLS
N_PROJ = 6
CONV_GROUP = 64
MXU_W = 256
GMAT_W = MXU_W
CONV_K = 3
NORM_EPS = 1e-6
FFN_RES_W = 0.5
LAM_INIT = 0.8 - 0.6 * math.exp(-0.3 * 0)
NEG_BIG = -1e30

FFN_CHUNK = 256
TOKEN_TILE = 512
BACK_TILE = 1024
SUB_TILE = 256
BACK_SUB = 512
NORM_ROWS = 32
ATT_BLOCK = 256
CARRY_ROWS = 8
STAGE_ROWS = 256
VMEM_LIMIT = 60000 * 1024


def _rms(x, gain):
    ms = jnp.mean(x * x, axis=-1, keepdims=True)
    return x * lax.rsqrt(ms + NORM_EPS) * gain


def _group_rms_scale(t, gmat_ref, group):
    t2 = (t * t).astype(BF16)
    w = gmat_ref.shape[0]
    ss = jnp.concatenate(
        [jnp.dot(t2[:, c:c + w], gmat_ref[...], preferred_element_type=F32)
         for c in range(0, t.shape[1], w)], axis=1)
    return lax.rsqrt(ss * (1.0 / group) + NORM_EPS)


def _norm_jobs(src_ref, gain_ref, dst_ref, r0, sub, dtype):
    def job(g):
        rows = slice(r0 + g * NORM_ROWS, r0 + (g + 1) * NORM_ROWS)
        dst_ref[rows, :] = _rms(src_ref[rows, :], gain_ref[...]).astype(dtype)
    return [lambda g=g: job(g) for g in range(sub // NORM_ROWS)]


def _run(jobs):
    for job in jobs:
        job()


def _gate_up(h_buf, wg_ref, wu_ref, a_buf, r0, sub, side_jobs):
    rows = slice(r0, r0 + sub)
    side_jobs = list(side_jobs)
    n_chunks = wg_ref.shape[1] // FFN_CHUNK
    done = 0
    for c in range(n_chunks):
        cols = slice(c * FFN_CHUNK, (c + 1) * FFN_CHUNK)
        h = h_buf[rows, :]
        g = jnp.dot(h, wg_ref[:, cols], preferred_element_type=F32)
        u = jnp.dot(h, wu_ref[:, cols], preferred_element_type=F32)
        a_buf[rows, cols] = (g * jax.nn.sigmoid(g) * u).astype(BF16)
        upto = -(-(c + 1) * len(side_jobs) // n_chunks)
        _run(side_jobs[done:upto])
        done = upto


def _front_kernel(x_ref, bias_ref, g1_ref, wg_hbm, wu_hbm, wd_hbm, g2_ref, win_hbm,
                  gmat_ref, qg_ref, kg_ref, cw_ref, cg_ref,
                  later_w0, later_w1, later_w2, later_w3,
                  x1_ref, qt_ref, k_ref, vt_ref, c_ref,
                  later_o0, later_o1, later_o2, later_o3,
                  h_buf, a_buf, h2_buf, u_buf, gb_buf,
                  wg_ref, wu_ref, wd_ref, win_ref, stage, stage_sem):
    tm = x_ref.shape[1]

    @pl.when((pl.program_id(0) == 0) & (pl.program_id(1) == 0))
    def _():
        rows = stage.shape[1]
        chunks = [(src, dst, r0) for src, dst in ((wg_hbm, wg_ref), (wu_hbm, wu_ref),
                                                  (win_hbm, win_ref), (wd_hbm, wd_ref))
                  for r0 in range(0, src.shape[0], rows)]

        def copy(i):
            src, _, r0 = chunks[i]
            return pltpu.make_async_copy(src.at[pl.ds(r0, rows), :],
                                         stage.at[i % 2, :, pl.ds(0, src.shape[1])],
                                         stage_sem.at[i % 2])

        copy(0).start()
        for i, (src, dst, r0) in enumerate(chunks):
            if i + 1 < len(chunks):
                copy(i + 1).start()
            copy(i).wait()
            dst[r0:r0 + rows, :] = stage[i % 2, :, 0:src.shape[1]].astype(BF16)
    x_rows, x1_rows = x_ref.at[0], x1_ref.at[0]
    sub_tiles = list(range(0, tm, SUB_TILE))

    @pl.when(pl.program_id(1) == 0)
    def _():
        u_buf[0:CARRY_ROWS, :] = jnp.zeros((CARRY_ROWS, CONV_W), F32)

    def down(r0):
        rows = slice(r0, r0 + SUB_TILE)
        x1_ref[0, rows, :] = x_ref[0, rows, :] + FFN_RES_W * jnp.dot(
            a_buf[rows, :], wd_ref[...], preferred_element_type=F32)

    def proj(r0, j):
        return jnp.dot(h2_buf[r0:r0 + SUB_TILE, :], win_ref[:, j * QK_W:(j + 1) * QK_W],
                       preferred_element_type=F32)

    def q_job(r0):
        q = proj(r0, 0)
        q = q * _group_rms_scale(q, gmat_ref, QK_DIM) * qg_ref[...]
        qt_ref[0, :, r0:r0 + SUB_TILE] = q.T.astype(BF16)

    def k_job(r0):
        rows = slice(r0, r0 + SUB_TILE)
        k = proj(r0, 1)
        k = (k * _group_rms_scale(k, gmat_ref, QK_DIM) * kg_ref[...]).astype(BF16)
        for hd in range(N_HEADS):
            k_ref[0, rows, hd * K_AUG:hd * K_AUG + HEAD_COLS] = (
                k[:, hd * HEAD_COLS:(hd + 1) * HEAD_COLS])
            k_ref[0, rows, hd * K_AUG + HEAD_COLS:(hd + 1) * K_AUG] = (
                bias_ref[rows, hd * HEAD_COLS:(hd + 1) * HEAD_COLS])

    def v_job(r0):
        rows = slice(r0, r0 + SUB_TILE)
        vt = proj(r0, 2).T.astype(BF16)
        ones_rows = jnp.where(lax.broadcasted_iota(jnp.int32, (V_PAD, SUB_TILE), 0) == 0, 1.0, 0.0)
        for hd in range(N_HEADS):
            vt_ref[0, hd * V_AUG:hd * V_AUG + V_DIM, rows] = vt[hd * V_DIM:(hd + 1) * V_DIM, :]
            vt_ref[0, hd * V_AUG + V_DIM:(hd + 1) * V_AUG, rows] = ones_rows.astype(BF16)

    def conv_in_job(r0):
        gb_buf[r0:r0 + SUB_TILE, :] = proj(r0, 3)
        u0 = CARRY_ROWS + r0
        u_buf[u0:u0 + SUB_TILE, :] = proj(r0, 4) * proj(r0, 5)

    def conv_out_job(r0):
        u0 = CARRY_ROWS + r0
        y = (cw_ref[0:1, :] * u_buf[u0 - 2:u0 - 2 + SUB_TILE, :]
             + cw_ref[1:2, :] * u_buf[u0 - 1:u0 - 1 + SUB_TILE, :]
             + cw_ref[2:3, :] * u_buf[u0:u0 + SUB_TILE, :])
        c = gb_buf[r0:r0 + SUB_TILE, :] * y
        c = c * _group_rms_scale(c, gmat_ref, CONV_GROUP) * cg_ref[...]
        c_ref[0, r0:r0 + SUB_TILE, :] = c.astype(BF16)

    def mixer_jobs(r0):
        return [lambda: q_job(r0), lambda: k_job(r0), lambda: v_job(r0),
                lambda: conv_in_job(r0), lambda: conv_out_job(r0)]

    def cast_job(w_ref, o_ref):
        o_ref[...] = w_ref[...].astype(BF16)

    _run(_norm_jobs(x_rows, g1_ref, h_buf, sub_tiles[0], SUB_TILE, BF16))
    pending = [lambda: cast_job(later_w0, later_o0), lambda: cast_job(later_w1, later_o1),
               lambda: cast_job(later_w2, later_o2), lambda: cast_job(later_w3, later_o3)]
    for i, r0 in enumerate(sub_tiles):
        nxt = sub_tiles[i + 1] if i + 1 < len(sub_tiles) else None
        side = ((_norm_jobs(x_rows, g1_ref, h_buf, nxt, SUB_TILE, BF16) if nxt is not None else [])
                + pending)
        _gate_up(h_buf, wg_ref, wu_ref, a_buf, r0, SUB_TILE, side)
        down(r0)
        pending = _norm_jobs(x1_rows, g2_ref, h2_buf, r0, SUB_TILE, BF16) + mixer_jobs(r0)
    _run(pending)

    u_buf[0:CARRY_ROWS, :] = u_buf[tm:tm + CARRY_ROWS, :]


def _attn_kernel(lam_ref, qt_ref, k_ref, vt_ref, sg_ref, o_ref,
                 rhs_buf, mask_buf, acc_buf, s_buf):
    blk = ATT_BLOCK
    half = 2 * blk
    seq = k_ref.shape[1]
    lam = lam_ref[0]
    heads = range(N_HEADS)
    all_cols, first, second = slice(0, 2 * half), slice(0, half), slice(half, 2 * half)

    row = lax.broadcasted_iota(jnp.int32, (blk, half), 0)
    col = lax.broadcasted_iota(jnp.int32, (blk, half), 1)
    col = jnp.where(col >= blk, col - blk, col)
    mask_buf[...] = jnp.where(row <= col, 0.0, NEG_BIG)

    rhs_buf[...] = jnp.zeros(rhs_buf.shape, BF16)
    for par in range(2):
        for h in heads:
            rhs_buf[par, h, HEAD_COLS:HEAD_COLS + BF16_ROWS, :] = jnp.where(
                lax.broadcasted_iota(jnp.int32, (BF16_ROWS, 2 * half), 0) < N_BIAS_COLS, 1.0, 0.0
            ).astype(BF16)

    def block_start(kb):
        return kb * blk if isinstance(kb, int) else pl.multiple_of(kb * blk, blk)

    def build_rhs(qb, par, h):
        q0 = block_start(2 * qb)
        r0 = h * HEAD_COLS
        for part in range(2):
            qs = pl.ds(q0 + part * blk, blk)
            c0 = part * half
            rhs_buf[par, h, 0:QK_DIM, c0:c0 + blk] = qt_ref[0, r0:r0 + QK_DIM, qs]
            rhs_buf[par, h, QK_DIM:HEAD_COLS, c0 + blk:c0 + half] = (
                qt_ref[0, r0 + QK_DIM:r0 + HEAD_COLS, qs])

    def scores(kb, slot, par, h, cols, want_max=True):
        s = jnp.dot(k_ref[0, pl.ds(block_start(kb), blk), h * K_AUG:(h + 1) * K_AUG],
                    rhs_buf[par, h, :, cols], preferred_element_type=F32)
        s_buf[slot, h, :, cols] = s
        return jnp.max(s, axis=0, keepdims=True) if want_max else None

    def masked_probs(slot, h, cols, m):
        hb = blk // 2
        live = [slice(hb, blk), slice(blk + hb, 2 * blk)]
        t_top = s_buf[slot, h, 0:hb, cols] + mask_buf[0:hb, :]
        t_bot = [s_buf[slot, h, hb:blk, cols.start + c.start:cols.start + c.stop]
                 + mask_buf[hb:blk, c] for c in live]
        dead_max = jnp.full((1, hb), NEG_BIG, F32)
        bot_max = jnp.concatenate([dead_max, jnp.max(t_bot[0], axis=0, keepdims=True),
                                   dead_max, jnp.max(t_bot[1], axis=0, keepdims=True)], axis=1)
        m_new = jnp.maximum(m, jnp.maximum(jnp.max(t_top, axis=0, keepdims=True), bot_max))
        dead_p = jnp.zeros((hb, hb), BF16)
        p_bot = jnp.concatenate(
            [dead_p, jnp.exp2(t_bot[0] - m_new[:, live[0]]).astype(BF16),
             dead_p, jnp.exp2(t_bot[1] - m_new[:, live[1]]).astype(BF16)], axis=1)
        p = jnp.concatenate([jnp.exp2(t_top - m_new).astype(BF16), p_bot], axis=0)
        return m_new, p

    def softmax_pv(kb, slot, h, cols, m, tile_max, masked):
        if masked:
            m_new, p = masked_probs(slot, h, cols, m)
        else:
            m_new = jnp.maximum(m, tile_max)
            p = jnp.exp2(s_buf[slot, h, :, cols] - m_new).astype(BF16)
        alpha = jnp.exp2(m - m_new)
        pv = jnp.dot(vt_ref[0, h * V_AUG:(h + 1) * V_AUG, pl.ds(block_start(kb), blk)], p,
                     preferred_element_type=F32)
        acc_buf[h, :, cols] = acc_buf[h, :, cols] * alpha + pv
        return m_new

    def finalize(qb, h):
        inv_l = 1.0 / acc_buf[h, V_DIM:V_DIM + 1, :]
        for part in range(2):
            c0 = part * half
            a = (acc_buf[h, 0:V_DIM, c0:c0 + blk] * inv_l[:, c0:c0 + blk]
                 - acc_buf[h, 0:V_DIM, c0 + blk:c0 + half]
                 * (lam * inv_l[:, c0 + blk:c0 + half]))
            ms_a = jnp.mean(a * a, axis=0, keepdims=True)
            a = a * lax.rsqrt(ms_a + NORM_EPS)
            o_ref[0, pl.ds(block_start(2 * qb + part), blk), h * V_DIM:(h + 1) * V_DIM] = (
                (a.T * sg_ref[...]).astype(BF16))

    def q_block(qb, tms, odd):
        par = int(odd)
        acc_buf[...] = jnp.zeros(acc_buf.shape, F32)
        ms = (jnp.full((1, 2 * half), NEG_BIG, F32),) * N_HEADS

        def full_step(kb, slot, carry):
            ms, tms = carry
            new_ms, new_tms = [], []
            for h in heads:
                new_tms.append(scores(kb + 1, 1 - slot, par, h, all_cols))
                new_ms.append(softmax_pv(kb, slot, h, all_cols, ms[h], tms[h], masked=False))
            return tuple(new_ms), tuple(new_tms)

        def pair(kp, c):
            return full_step(2 * kp + 1, 1, full_step(2 * kp, 0, c))

        ms, tms = lax.fori_loop(0, qb // 2, lambda kq, c: pair(2 * kq + 1, pair(2 * kq, c)),
                                (ms, tms))
        if odd:
            ms, tms = pair(qb - 1, (ms, tms))

        for h in heads:
            scores(2 * qb + 1, 1, par, h, second, want_max=False)
            softmax_pv(2 * qb, 0, h, first, ms[h][:, first], None, masked=True)
            m_b = softmax_pv(2 * qb, 0, h, second, ms[h][:, second], tms[h][:, second],
                             masked=False)
            softmax_pv(2 * qb + 1, 1, h, second, m_b, None, masked=True)

        nxt = jnp.minimum(qb + 1, n_qb - 1)
        new_tms = []
        for h in heads:
            build_rhs(nxt, 1 - par, h)
            new_tms.append(scores(0, 0, 1 - par, h, all_cols))
            finalize(qb, h)
        return tuple(new_tms)

    n_qb = seq // (2 * blk)
    for h in heads:
        build_rhs(0, 0, h)
    lax.fori_loop(0, n_qb // 2,
                  lambda j, tms: q_block(2 * j + 1, q_block(2 * j, tms, odd=False), odd=True),
                  tuple(scores(0, 0, 0, h, all_cols) for h in heads))


def _back_kernel(x1_ref, a_ref, c_ref, wo_ref, g3_ref, wg_ref, wu_ref, wd_ref, gf_ref,
                 out_ref, x2_buf, h_buf, a_buf):
    tm = x1_ref.shape[1]
    out_rows = out_ref.at[0]
    sub_tiles = list(range(0, tm, BACK_SUB))

    def out_proj(r0):
        rows = slice(r0, r0 + BACK_SUB)
        x2_buf[rows, :] = (
            x1_ref[0, rows, :]
            + jnp.dot(a_ref[0, rows, :], wo_ref[0:ATTN_W, :], preferred_element_type=F32)
            + jnp.dot(c_ref[0, rows, :], wo_ref[ATTN_W:ATTN_W + CONV_W, :],
                      preferred_element_type=F32))

    def down(r0):
        rows = slice(r0, r0 + BACK_SUB)
        out_ref[0, rows, :] = x2_buf[rows, :] + FFN_RES_W * jnp.dot(
            a_buf[rows, :], wd_ref[...], preferred_element_type=F32)

    out_proj(sub_tiles[0])
    _run(_norm_jobs(x2_buf, g3_ref, h_buf, sub_tiles[0], BACK_SUB, BF16))
    pending = []
    for i, r0 in enumerate(sub_tiles):
        nxt = sub_tiles[i + 1] if i + 1 < len(sub_tiles) else None
        side = list(pending)
        if nxt is not None:
            side = ([lambda nxt=nxt: out_proj(nxt)] + side
                    + _norm_jobs(x2_buf, g3_ref, h_buf, nxt, BACK_SUB, BF16))
        _gate_up(h_buf, wg_ref, wu_ref, a_buf, r0, BACK_SUB, side)
        down(r0)
        pending = _norm_jobs(out_rows, gf_ref, out_rows, r0, BACK_SUB, F32)
    _run(pending)


def _alibi_columns(seq):
    slopes = np.exp2(-8.0 * np.arange(1, N_HEADS + 1) / N_HEADS) * math.log2(math.e)
    rest = (slopes[None, :] * np.arange(seq)[:, None]).astype(np.float32)
    table = np.zeros((seq, N_HEADS, HEAD_COLS), np.float32)
    for c in range(N_BIAS_COLS):
        piece = (rest.view(np.uint32) & np.uint32(0xFFFF0000)).view(np.float32)
        table[:, :, c] = piece
        rest = rest - piece
    return table.reshape(seq, N_HEADS * HEAD_COLS)


def _resident(shape):
    return pl.BlockSpec(shape, lambda *_: (0,) * len(shape), pipeline_mode=pl.Buffered(1))


def kernel(x, ffn1_norm, ffn1_w_gate, ffn1_w_up, ffn1_w_down, mix_norm, w_in, q_norm, k_norm,
           lambda_q1, lambda_k1, lambda_q2, lambda_k2, attn_subln, conv_w, conv_norm, w_out,
           ffn2_norm, ffn2_w_gate, ffn2_w_up, ffn2_w_down, final_norm):
    b, s, d = x.shape
    d_ff = ffn1_w_gate.shape[-1]
    tm = TOKEN_TILE
    nt = s // tm
    assert s % tm == 0 and tm % SUB_TILE == 0 and d_ff % FFN_CHUNK == 0
    assert s % BACK_TILE == 0 and BACK_TILE % BACK_SUB == 0
    assert s % (4 * ATT_BLOCK) == 0
    assert ffn1_w_gate.shape[0] == 1, "single layer"

    row = lambda v: v.reshape(1, -1).astype(F32)
    n_groups = QK_W // QK_DIM
    gid = jnp.arange(GMAT_W) // QK_DIM
    gmat = (gid[:, None] == gid[None, :]).astype(BF16)
    log2e = math.log2(math.e)
    q_gain = jnp.tile(q_norm[0].astype(F32), n_groups).reshape(1, QK_W) * (QK_DIM ** -0.5 * log2e)
    k_gain = jnp.tile(k_norm[0].astype(F32), n_groups).reshape(1, QK_W)
    bias_cols = jnp.asarray(_alibi_columns(s), dtype=BF16)
    sub_gain = row(attn_subln[0]) * (1.0 - LAM_INIT)
    lam = (jnp.exp(jnp.sum(lambda_q1[0].astype(F32) * lambda_k1[0].astype(F32)))
           - jnp.exp(jnp.sum(lambda_q2[0].astype(F32) * lambda_k2[0].astype(F32)))
           + LAM_INIT).reshape(1)

    tok = lambda w: pl.BlockSpec((1, tm, w), lambda bi, ti: (bi, ti, 0))
    tok_t = lambda r: pl.BlockSpec((1, r, tm), lambda bi, ti: (bi, 0, ti))
    params = pltpu.CompilerParams(dimension_semantics=("arbitrary", "arbitrary"),
                                  vmem_limit_bytes=VMEM_LIMIT)

    later = [ffn2_w_gate[0], ffn2_w_up[0], ffn2_w_down[0], w_out[0]]
    n_steps = b * nt

    def cast_spec(w):
        n_rows = w.shape[0]
        rows = next(r for r in range(BF16_ROWS, n_rows + 1, BF16_ROWS)
                    if n_rows % r == 0 and n_rows // r <= n_steps)
        last = n_rows // rows - 1
        return pl.BlockSpec((rows, w.shape[1]),
                            lambda bi, ti: (jnp.minimum(bi * nt + ti, last), 0))

    hbm = pl.BlockSpec(memory_space=pl.ANY)
    assert d % STAGE_ROWS == 0 and d_ff % STAGE_ROWS == 0 and d_ff <= N_PROJ * QK_W
    outs = pl.pallas_call(
        _front_kernel,
        grid=(b, nt),
        in_specs=[tok(d), pl.BlockSpec((tm, N_HEADS * HEAD_COLS), lambda bi, ti: (ti, 0)),
                  _resident((1, d)), hbm, hbm, hbm, _resident((1, d)), hbm,
                  _resident((GMAT_W, GMAT_W)), _resident((1, QK_W)), _resident((1, QK_W)),
                  _resident((CONV_K, CONV_W)), _resident((1, CONV_W))]
                 + [cast_spec(w) for w in later],
        out_specs=[tok(d), tok_t(QK_W), tok(N_HEADS * K_AUG), tok_t(N_HEADS * V_AUG), tok(CONV_W)]
                  + [cast_spec(w) for w in later],
        out_shape=[jax.ShapeDtypeStruct((b, s, d), F32),
                   jax.ShapeDtypeStruct((b, QK_W, s), BF16),
                   jax.ShapeDtypeStruct((b, s, N_HEADS * K_AUG), BF16),
                   jax.ShapeDtypeStruct((b, N_HEADS * V_AUG, s), BF16),
                   jax.ShapeDtypeStruct((b, s, CONV_W), BF16)]
                  + [jax.ShapeDtypeStruct(w.shape, BF16) for w in later],
        scratch_shapes=[pltpu.VMEM((tm, d), BF16),
                        pltpu.VMEM((tm, d_ff), BF16),
                        pltpu.VMEM((tm, d), BF16),
                        pltpu.VMEM((tm + CARRY_ROWS, CONV_W), F32),
                        pltpu.VMEM((tm, CONV_W), F32),
                        pltpu.VMEM((d, d_ff), BF16), pltpu.VMEM((d, d_ff), BF16),
                        pltpu.VMEM((d_ff, d), BF16), pltpu.VMEM((d, N_PROJ * QK_W), BF16),
                        pltpu.VMEM((2, STAGE_ROWS, N_PROJ * QK_W), F32),
                        pltpu.SemaphoreType.DMA((2,))],
        compiler_params=params,
        name="front",
    )(x, bias_cols, row(ffn1_norm[0]), ffn1_w_gate[0], ffn1_w_up[0], ffn1_w_down[0],
      row(mix_norm[0]), w_in[0], gmat, q_gain, k_gain,
      conv_w[0].astype(F32), row(conv_norm[0]), *later)
    x1, q_t, k_n, v_t, c_n, w2_gate, w2_up, w2_down, w_out_bf = outs

    blk = ATT_BLOCK
    attn = pl.pallas_call(
        _attn_kernel,
        grid=(b,),
        in_specs=[pl.BlockSpec(memory_space=pltpu.SMEM),
                  pl.BlockSpec((1, QK_W, s), lambda bi: (bi, 0, 0)),
                  pl.BlockSpec((1, s, N_HEADS * K_AUG), lambda bi: (bi, 0, 0)),
                  pl.BlockSpec((1, N_HEADS * V_AUG, s), lambda bi: (bi, 0, 0)),
                  _resident((1, V_DIM))],
        out_specs=pl.BlockSpec((1, s, ATTN_W), lambda bi: (bi, 0, 0)),
        out_shape=jax.ShapeDtypeStruct((b, s, ATTN_W), BF16),
        scratch_shapes=[pltpu.VMEM((2, N_HEADS, K_AUG, 4 * blk), BF16),
                        pltpu.VMEM((blk, 2 * blk), F32),
                        pltpu.VMEM((N_HEADS, V_AUG, 4 * blk), F32),
                        pltpu.VMEM((2, N_HEADS, blk, 4 * blk), F32)],
        compiler_params=pltpu.CompilerParams(dimension_semantics=("arbitrary",),
                                             vmem_limit_bytes=VMEM_LIMIT),
        name="diff_attn",
    )(lam, q_t, k_n, v_t, sub_gain)

    tb = BACK_TILE
    tok_b = lambda w: pl.BlockSpec((1, tb, w), lambda bi, ti: (bi, ti, 0))
    out = pl.pallas_call(
        _back_kernel,
        grid=(b, s // tb),
        in_specs=[tok_b(d), tok_b(ATTN_W), tok_b(CONV_W), _resident((ATTN_W + CONV_W, d)),
                  _resident((1, d)), _resident((d, d_ff)), _resident((d, d_ff)),
                  _resident((d_ff, d)), _resident((1, d))],
        out_specs=tok_b(d),
        out_shape=jax.ShapeDtypeStruct((b, s, d), F32),
        scratch_shapes=[pltpu.VMEM((tb, d), F32),
                        pltpu.VMEM((tb, d), BF16),
                        pltpu.VMEM((tb, d_ff), BF16)],
        compiler_params=params,
        name="back",
    )(x1, attn, c_n, w_out_bf, row(ffn2_norm[0]), w2_gate, w2_up, w2_down, row(final_norm[0]))
    return out
```

```python
import math

import jax
import jax.numpy as jnp
import numpy as np
from jax import lax
from jax.experimental import pallas as pl
from jax.experimental.pallas import tpu as pltpu

F32 = jnp.float32
BF16 = jnp.bfloat16

N_HEADS = 4
QK_DIM = 64
V_DIM = 128
HEAD_COLS = 2 * QK_DIM
K_AUG = 2 * HEAD_COLS
N_BIAS_COLS = 3
BF16_ROWS = 16
V_PAD = BF16_ROWS
V_AUG = V_DIM + V_PAD
ATTN_W = N_HEADS * V_DIM
CONV_W = 512
QK_W = N_HEADS * HEAD_COLS
N_PROJ = 6
CONV_GROUP = 64
MXU_W = 256
GMAT_W = MXU_W
CONV_K = 3
NORM_EPS = 1e-6
FFN_RES_W = 0.5
LAM_INIT = 0.8 - 0.6 * math.exp(-0.3 * 0)
NEG_BIG = -1e30

FFN_CHUNK = 256
TOKEN_TILE = 512
BACK_TILE = 1024
SUB_TILE = 256
BACK_SUB = 512
NORM_ROWS = 32
ATT_BLOCK = 256
CARRY_ROWS = 8
STAGE_ROWS = 256
VMEM_LIMIT = 60000 * 1024


def _rms(x, gain):
    ms = jnp.mean(x * x, axis=-1, keepdims=True)
    return x * lax.rsqrt(ms + NORM_EPS) * gain


def _group_rms_scale(t, gmat_ref, group):
    t2 = (t * t).astype(BF16)
    w = gmat_ref.shape[0]
    ss = jnp.concatenate(
        [jnp.dot(t2[:, c:c + w], gmat_ref[...], preferred_element_type=F32)
         for c in range(0, t.shape[1], w)], axis=1)
    return lax.rsqrt(ss * (1.0 / group) + NORM_EPS)


def _norm_jobs(src_ref, gain_ref, dst_ref, r0, sub, dtype):
    def job(g):
        rows = slice(r0 + g * NORM_ROWS, r0 + (g + 1) * NORM_ROWS)
        dst_ref[rows, :] = _rms(src_ref[rows, :], gain_ref[...]).astype(dtype)
    return [lambda g=g: job(g) for g in range(sub // NORM_ROWS)]


def _run(jobs):
    for job in jobs:
        job()


def _gate_up(h_buf, wg_ref, wu_ref, a_buf, r0, sub, side_jobs):
    rows = slice(r0, r0 + sub)
    side_jobs = list(side_jobs)
    n_chunks = wg_ref.shape[1] // FFN_CHUNK
    done = 0
    for c in range(n_chunks):
        cols = slice(c * FFN_CHUNK, (c + 1) * FFN_CHUNK)
        h = h_buf[rows, :]
        g = jnp.dot(h, wg_ref[:, cols], preferred_element_type=F32)
        u = jnp.dot(h, wu_ref[:, cols], preferred_element_type=F32)
        a_buf[rows, cols] = (g * jax.nn.sigmoid(g) * u).astype(BF16)
        upto = -(-(c + 1) * len(side_jobs) // n_chunks)
        _run(side_jobs[done:upto])
        done = upto


def _front_kernel(x_ref, bias_ref, g1_ref, wg_hbm, wu_hbm, wd_hbm, g2_ref, win_hbm,
                  gmat_ref, qg_ref, kg_ref, cw_ref, cg_ref,
                  later_w0, later_w1, later_w2, later_w3,
                  x1_ref, qt_ref, k_ref, vt_ref, c_ref,
                  later_o0, later_o1, later_o2, later_o3,
                  h_buf, a_buf, h2_buf, u_buf, gb_buf,
                  wg_ref, wu_ref, wd_ref, win_ref, stage, stage_sem):
    tm = x_ref.shape[1]

    first_step = (pl.program_id(0) == 0) & (pl.program_id(1) == 0)
    stage_rows = stage.shape[1]
    chunks = [(src, dst, r0) for src, dst in ((wg_hbm, wg_ref), (wu_hbm, wu_ref),
                                              (win_hbm, win_ref), (wd_hbm, wd_ref))
              for r0 in range(0, src.shape[0], stage_rows)]
    n_early = 2 * (wg_hbm.shape[0] // stage_rows)

    def copy(i):
        src, _, r0 = chunks[i]
        return pltpu.make_async_copy(src.at[pl.ds(r0, stage_rows), :],
                                     stage.at[i % 2, :, pl.ds(0, src.shape[1])],
                                     stage_sem.at[i % 2])

    def stage_job(i):
        src, dst, r0 = chunks[i]
        if i + 1 < len(chunks):
            copy(i + 1).start()
        copy(i).wait()
        dst[r0:r0 + stage_rows, :] = stage[i % 2, :, 0:src.shape[1]].astype(BF16)

    @pl.when(first_step)
    def _():
        copy(0).start()
        for i in range(n_early):
            stage_job(i)
    x_rows, x1_rows = x_ref.at[0], x1_ref.at[0]
    sub_tiles = list(range(0, tm, SUB_TILE))

    @pl.when(pl.program_id(1) == 0)
    def _():
        u_buf[0:CARRY_ROWS, :] = jnp.zeros((CARRY_ROWS, CONV_W), F32)

    def down(r0):
        rows = slice(r0, r0 + SUB_TILE)
        x1_ref[0, rows, :] = x_ref[0, rows, :] + FFN_RES_W * jnp.dot(
            a_buf[rows, :], wd_ref[...], preferred_element_type=F32)

    def proj(r0, j):
        return jnp.dot(h2_buf[r0:r0 + SUB_TILE, :], win_ref[:, j * QK_W:(j + 1) * QK_W],
                       preferred_element_type=F32)

    def q_job(r0):
        q = proj(r0, 0)
        q = q * _group_rms_scale(q, gmat_ref, QK_DIM) * qg_ref[...]
        qt_ref[0, :, r0:r0 + SUB_TILE] = q.T.astype(BF16)

    def k_job(r0):
        rows = slice(r0, r0 + SUB_TILE)
        k = proj(r0, 1)
        k = (k * _group_rms_scale(k, gmat_ref, QK_DIM) * kg_ref[...]).astype(BF16)
        for hd in range(N_HEADS):
            k_ref[0, rows, hd * K_AUG:hd * K_AUG + HEAD_COLS] = (
                k[:, hd * HEAD_COLS:(hd + 1) * HEAD_COLS])
            k_ref[0, rows, hd * K_AUG + HEAD_COLS:(hd + 1) * K_AUG] = (
                bias_ref[rows, hd * HEAD_COLS:(hd + 1) * HEAD_COLS])

    def v_job(r0):
        rows = slice(r0, r0 + SUB_TILE)
        vt = proj(r0, 2).T.astype(BF16)
        ones_rows = jnp.where(lax.broadcasted_iota(jnp.int32, (V_PAD, SUB_TILE), 0) == 0, 1.0, 0.0)
        for hd in range(N_HEADS):
            vt_ref[0, hd * V_AUG:hd * V_AUG + V_DIM, rows] = vt[hd * V_DIM:(hd + 1) * V_DIM, :]
            vt_ref[0, hd * V_AUG + V_DIM:(hd + 1) * V_AUG, rows] = ones_rows.astype(BF16)

    def conv_in_job(r0):
        gb_buf[r0:r0 + SUB_TILE, :] = proj(r0, 3)
        u0 = CARRY_ROWS + r0
        u_buf[u0:u0 + SUB_TILE, :] = proj(r0, 4) * proj(r0, 5)

    def conv_out_job(r0):
        u0 = CARRY_ROWS + r0
        y = (cw_ref[0:1, :] * u_buf[u0 - 2:u0 - 2 + SUB_TILE, :]
             + cw_ref[1:2, :] * u_buf[u0 - 1:u0 - 1 + SUB_TILE, :]
             + cw_ref[2:3, :] * u_buf[u0:u0 + SUB_TILE, :])
        c = gb_buf[r0:r0 + SUB_TILE, :] * y
        c = c * _group_rms_scale(c, gmat_ref, CONV_GROUP) * cg_ref[...]
        c_ref[0, r0:r0 + SUB_TILE, :] = c.astype(BF16)

    def mixer_jobs(r0):
        return [lambda: q_job(r0), lambda: k_job(r0), lambda: v_job(r0),
                lambda: conv_in_job(r0), lambda: conv_out_job(r0)]

    def cast_job(w_ref, o_ref):
        o_ref[...] = w_ref[...].astype(BF16)

    def schedule(first_jobs):
        _run(_norm_jobs(x_rows, g1_ref, h_buf, sub_tiles[0], SUB_TILE, BF16))
        pending = first_jobs + [
            lambda: cast_job(later_w0, later_o0), lambda: cast_job(later_w1, later_o1),
            lambda: cast_job(later_w2, later_o2), lambda: cast_job(later_w3, later_o3)]
        for i, r0 in enumerate(sub_tiles):
            nxt = sub_tiles[i + 1] if i + 1 < len(sub_tiles) else None
            side = ((_norm_jobs(x_rows, g1_ref, h_buf, nxt, SUB_TILE, BF16)
                     if nxt is not None else []) + pending)
            _gate_up(h_buf, wg_ref, wu_ref, a_buf, r0, SUB_TILE, side)
            down(r0)
            pending = _norm_jobs(x1_rows, g2_ref, h2_buf, r0, SUB_TILE, BF16) + mixer_jobs(r0)
        _run(pending)
        u_buf[0:CARRY_ROWS, :] = u_buf[tm:tm + CARRY_ROWS, :]

    @pl.when(first_step)
    def _():
        schedule([lambda i=i: stage_job(i) for i in range(n_early, len(chunks))])

    @pl.when(jnp.logical_not(first_step))
    def _():
        schedule([])


def _attn_kernel(lam_ref, qt_ref, k_ref, vt_ref, sg_ref, o_ref,
                 rhs_buf, mask_buf, acc_buf, s_buf):
    blk = ATT_BLOCK
    half = 2 * blk
    seq = k_ref.shape[1]
    lam = lam_ref[0]
    heads = range(N_HEADS)
    all_cols, first, second = slice(0, 2 * half), slice(0, half), slice(half, 2 * half)

    row = lax.broadcasted_iota(jnp.int32, (blk, half), 0)
    col = lax.broadcasted_iota(jnp.int32, (blk, half), 1)
    col = jnp.where(col >= blk, col - blk, col)
    mask_buf[...] = jnp.where(row <= col, 0.0, NEG_BIG)

    rhs_buf[...] = jnp.zeros(rhs_buf.shape, BF16)
    for par in range(2):
        for h in heads:
            rhs_buf[par, h, HEAD_COLS:HEAD_COLS + BF16_ROWS, :] = jnp.where(
                lax.broadcasted_iota(jnp.int32, (BF16_ROWS, 2 * half), 0) < N_BIAS_COLS, 1.0, 0.0
            ).astype(BF16)

    def block_start(kb):
        return kb * blk if isinstance(kb, int) else pl.multiple_of(kb * blk, blk)

    def build_rhs(qb, par, h):
        q0 = block_start(2 * qb)
        r0 = h * HEAD_COLS
        for part in range(2):
            qs = pl.ds(q0 + part * blk, blk)
            c0 = part * half
            rhs_buf[par, h, 0:QK_DIM, c0:c0 + blk] = qt_ref[0, r0:r0 + QK_DIM, qs]
            rhs_buf[par, h, QK_DIM:HEAD_COLS, c0 + blk:c0 + half] = (
                qt_ref[0, r0 + QK_DIM:r0 + HEAD_COLS, qs])

    def scores(kb, slot, par, h, cols, want_max=True):
        s = jnp.dot(k_ref[0, pl.ds(block_start(kb), blk), h * K_AUG:(h + 1) * K_AUG],
                    rhs_buf[par, h, :, cols], preferred_element_type=F32)
        s_buf[slot, h, :, cols] = s
        return jnp.max(s, axis=0, keepdims=True) if want_max else None

    def masked_probs(slot, h, cols, m):
        hb = blk // 2
        live = [slice(hb, blk), slice(blk + hb, 2 * blk)]
        t_top = s_buf[slot, h, 0:hb, cols] + mask_buf[0:hb, :]
        t_bot = [s_buf[slot, h, hb:blk, cols.start + c.start:cols.start + c.stop]
                 + mask_buf[hb:blk, c] for c in live]
        dead_max = jnp.full((1, hb), NEG_BIG, F32)
        bot_max = jnp.concatenate([dead_max, jnp.max(t_bot[0], axis=0, keepdims=True),
                                   dead_max, jnp.max(t_bot[1], axis=0, keepdims=True)], axis=1)
        m_new = jnp.maximum(m, jnp.maximum(jnp.max(t_top, axis=0, keepdims=True), bot_max))
        dead_p = jnp.zeros((hb, hb), BF16)
        p_bot = jnp.concatenate(
            [dead_p, jnp.exp2(t_bot[0] - m_new[:, live[0]]).astype(BF16),
             dead_p, jnp.exp2(t_bot[1] - m_new[:, live[1]]).astype(BF16)], axis=1)
        p = jnp.concatenate([jnp.exp2(t_top - m_new).astype(BF16), p_bot], axis=0)
        return m_new, p

    def softmax_pv(kb, slot, h, cols, m, tile_max, masked):
        if masked:
            m_new, p = masked_probs(slot, h, cols, m)
        else:
            m_new = jnp.maximum(m, tile_max)
            p = jnp.exp2(s_buf[slot, h, :, cols] - m_new).astype(BF16)
        alpha = jnp.exp2(m - m_new)
        pv = jnp.dot(vt_ref[0, h * V_AUG:(h + 1) * V_AUG, pl.ds(block_start(kb), blk)], p,
                     preferred_element_type=F32)
        acc_buf[h, :, cols] = acc_buf[h, :, cols] * alpha + pv
        return m_new

    def finalize(qb, h):
        inv_l = 1.0 / acc_buf[h, V_DIM:V_DIM + 1, :]
        for part in range(2):
            c0 = part * half
            a = (acc_buf[h, 0:V_DIM, c0:c0 + blk] * inv_l[:, c0:c0 + blk]
                 - acc_buf[h, 0:V_DIM, c0 + blk:c0 + half]
                 * (lam * inv_l[:, c0 + blk:c0 + half]))
            ms_a = jnp.mean(a * a, axis=0, keepdims=True)
            a = a * lax.rsqrt(ms_a + NORM_EPS)
            o_ref[0, pl.ds(block_start(2 * qb + part), blk), h * V_DIM:(h + 1) * V_DIM] = (
                (a.T * sg_ref[...]).astype(BF16))

    def q_block(qb, tms, odd):
        par = int(odd)
        acc_buf[...] = jnp.zeros(acc_buf.shape, F32)
        ms = (jnp.full((1, 2 * half), NEG_BIG, F32),) * N_HEADS

        def full_step(kb, slot, carry):
            ms, tms = carry
            new_ms, new_tms = [], []
            for h in heads:
                new_tms.append(scores(kb + 1, 1 - slot, par, h, all_cols))
                new_ms.append(softmax_pv(kb, slot, h, all_cols, ms[h], tms[h], masked=False))
            return tuple(new_ms), tuple(new_tms)

        def pair(kp, c):
            return full_step(2 * kp + 1, 1, full_step(2 * kp, 0, c))

        ms, tms = lax.fori_loop(0, qb // 2, lambda kq, c: pair(2 * kq + 1, pair(2 * kq, c)),
                                (ms, tms))
        if odd:
            ms, tms = pair(qb - 1, (ms, tms))

        for h in heads:
            scores(2 * qb + 1, 1, par, h, second, want_max=False)
            softmax_pv(2 * qb, 0, h, first, ms[h][:, first], None, masked=True)
            m_b = softmax_pv(2 * qb, 0, h, second, ms[h][:, second], tms[h][:, second],
                             masked=False)
            softmax_pv(2 * qb + 1, 1, h, second, m_b, None, masked=True)

        nxt = jnp.minimum(qb + 1, n_qb - 1)
        new_tms = []
        for h in heads:
            build_rhs(nxt, 1 - par, h)
            new_tms.append(scores(0, 0, 1 - par, h, all_cols))
            finalize(qb, h)
        return tuple(new_tms)

    n_qb = seq // (2 * blk)
    for h in heads:
        build_rhs(0, 0, h)
    lax.fori_loop(0, n_qb // 2,
                  lambda j, tms: q_block(2 * j + 1, q_block(2 * j, tms, odd=False), odd=True),
                  tuple(scores(0, 0, 0, h, all_cols) for h in heads))


def _back_kernel(x1_ref, a_ref, c_ref, wo_ref, g3_ref, wg_ref, wu_ref, wd_ref, gf_ref,
                 out_ref, x2_buf, h_buf, a_buf):
    tm = x1_ref.shape[1]
    out_rows = out_ref.at[0]
    sub_tiles = list(range(0, tm, BACK_SUB))

    def out_proj(r0):
        rows = slice(r0, r0 + BACK_SUB)
        x2_buf[rows, :] = (
            x1_ref[0, rows, :]
            + jnp.dot(a_ref[0, rows, :], wo_ref[0:ATTN_W, :], preferred_element_type=F32)
            + jnp.dot(c_ref[0, rows, :], wo_ref[ATTN_W:ATTN_W + CONV_W, :],
                      preferred_element_type=F32))

    def down(r0):
        rows = slice(r0, r0 + BACK_SUB)
        out_ref[0, rows, :] = x2_buf[rows, :] + FFN_RES_W * jnp.dot(
            a_buf[rows, :], wd_ref[...], preferred_element_type=F32)

    out_proj(sub_tiles[0])
    _run(_norm_jobs(x2_buf, g3_ref, h_buf, sub_tiles[0], BACK_SUB, BF16))
    pending = []
    for i, r0 in enumerate(sub_tiles):
        nxt = sub_tiles[i + 1] if i + 1 < len(sub_tiles) else None
        side = list(pending)
        if nxt is not None:
            side = ([lambda nxt=nxt: out_proj(nxt)] + side
                    + _norm_jobs(x2_buf, g3_ref, h_buf, nxt, BACK_SUB, BF16))
        _gate_up(h_buf, wg_ref, wu_ref, a_buf, r0, BACK_SUB, side)
        down(r0)
        pending = _norm_jobs(out_rows, gf_ref, out_rows, r0, BACK_SUB, F32)
    _run(pending)


def _alibi_columns(seq):
    slopes = np.exp2(-8.0 * np.arange(1, N_HEADS + 1) / N_HEADS) * math.log2(math.e)
    rest = (slopes[None, :] * np.arange(seq)[:, None]).astype(np.float32)
    table = np.zeros((seq, N_HEADS, HEAD_COLS), np.float32)
    for c in range(N_BIAS_COLS):
        piece = (rest.view(np.uint32) & np.uint32(0xFFFF0000)).view(np.float32)
        table[:, :, c] = piece
        rest = rest - piece
    return table.reshape(seq, N_HEADS * HEAD_COLS)


def _resident(shape):
    return pl.BlockSpec(shape, lambda *_: (0,) * len(shape), pipeline_mode=pl.Buffered(1))


def kernel(x, ffn1_norm, ffn1_w_gate, ffn1_w_up, ffn1_w_down, mix_norm, w_in, q_norm, k_norm,
           lambda_q1, lambda_k1, lambda_q2, lambda_k2, attn_subln, conv_w, conv_norm, w_out,
           ffn2_norm, ffn2_w_gate, ffn2_w_up, ffn2_w_down, final_norm):
    b, s, d = x.shape
    d_ff = ffn1_w_gate.shape[-1]
    tm = TOKEN_TILE
    nt = s // tm
    assert s % tm == 0 and tm % SUB_TILE == 0 and d_ff % FFN_CHUNK == 0
    assert s % BACK_TILE == 0 and BACK_TILE % BACK_SUB == 0
    assert s % (4 * ATT_BLOCK) == 0
    assert ffn1_w_gate.shape[0] == 1, "single layer"

    row = lambda v: v.reshape(1, -1).astype(F32)
    n_groups = QK_W // QK_DIM
    gid = jnp.arange(GMAT_W) // QK_DIM
    gmat = (gid[:, None] == gid[None, :]).astype(BF16)
    log2e = math.log2(math.e)
    q_gain = jnp.tile(q_norm[0].astype(F32), n_groups).reshape(1, QK_W) * (QK_DIM ** -0.5 * log2e)
    k_gain = jnp.tile(k_norm[0].astype(F32), n_groups).reshape(1, QK_W)
    bias_cols = jnp.asarray(_alibi_columns(s), dtype=BF16)
    sub_gain = row(attn_subln[0]) * (1.0 - LAM_INIT)
    lam = (jnp.exp(jnp.sum(lambda_q1[0].astype(F32) * lambda_k1[0].astype(F32)))
           - jnp.exp(jnp.sum(lambda_q2[0].astype(F32) * lambda_k2[0].astype(F32)))
           + LAM_INIT).reshape(1)

    tok = lambda w: pl.BlockSpec((1, tm, w), lambda bi, ti: (bi, ti, 0))
    tok_t = lambda r: pl.BlockSpec((1, r, tm), lambda bi, ti: (bi, 0, ti))
    params = pltpu.CompilerParams(dimension_semantics=("arbitrary", "arbitrary"),
                                  vmem_limit_bytes=VMEM_LIMIT)

    later = [ffn2_w_gate[0], ffn2_w_up[0], ffn2_w_down[0], w_out[0]]
    n_steps = b * nt

    def cast_spec(w):
        n_rows = w.shape[0]
        rows = next(r for r in range(BF16_ROWS, n_rows + 1, BF16_ROWS)
                    if n_rows % r == 0 and n_rows // r <= n_steps)
        last = n_rows // rows - 1
        return pl.BlockSpec((rows, w.shape[1]),
                            lambda bi, ti: (jnp.minimum(bi * nt + ti, last), 0))

    hbm = pl.BlockSpec(memory_space=pl.ANY)
    assert d % STAGE_ROWS == 0 and d_ff % STAGE_ROWS == 0 and d_ff <= N_PROJ * QK_W
    outs = pl.pallas_call(
        _front_kernel,
        grid=(b, nt),
        in_specs=[tok(d), pl.BlockSpec((tm, N_HEADS * HEAD_COLS), lambda bi, ti: (ti, 0)),
                  _resident((1, d)), hbm, hbm, hbm, _resident((1, d)), hbm,
                  _resident((GMAT_W, GMAT_W)), _resident((1, QK_W)), _resident((1, QK_W)),
                  _resident((CONV_K, CONV_W)), _resident((1, CONV_W))]
                 + [cast_spec(w) for w in later],
        out_specs=[tok(d), tok_t(QK_W), tok(N_HEADS * K_AUG), tok_t(N_HEADS * V_AUG), tok(CONV_W)]
                  + [cast_spec(w) for w in later],
        out_shape=[jax.ShapeDtypeStruct((b, s, d), F32),
                   jax.ShapeDtypeStruct((b, QK_W, s), BF16),
                   jax.ShapeDtypeStruct((b, s, N_HEADS * K_AUG), BF16),
                   jax.ShapeDtypeStruct((b, N_HEADS * V_AUG, s), BF16),
                   jax.ShapeDtypeStruct((b, s, CONV_W), BF16)]
                  + [jax.ShapeDtypeStruct(w.shape, BF16) for w in later],
        scratch_shapes=[pltpu.VMEM((tm, d), BF16),
                        pltpu.VMEM((tm, d_ff), BF16),
                        pltpu.VMEM((tm, d), BF16),
                        pltpu.VMEM((tm + CARRY_ROWS, CONV_W), F32),
                        pltpu.VMEM((tm, CONV_W), F32),
                        pltpu.VMEM((d, d_ff), BF16), pltpu.VMEM((d, d_ff), BF16),
                        pltpu.VMEM((d_ff, d), BF16), pltpu.VMEM((d, N_PROJ * QK_W), BF16),
                        pltpu.VMEM((2, STAGE_ROWS, N_PROJ * QK_W), F32),
                        pltpu.SemaphoreType.DMA((2,))],
        compiler_params=params,
        name="front",
    )(x, bias_cols, row(ffn1_norm[0]), ffn1_w_gate[0], ffn1_w_up[0], ffn1_w_down[0],
      row(mix_norm[0]), w_in[0], gmat, q_gain, k_gain,
      conv_w[0].astype(F32), row(conv_norm[0]), *later)
    x1, q_t, k_n, v_t, c_n, w2_gate, w2_up, w2_down, w_out_bf = outs

    blk = ATT_BLOCK
    attn = pl.pallas_call(
        _attn_kernel,
        grid=(b,),
        in_specs=[pl.BlockSpec(memory_space=pltpu.SMEM),
                  pl.BlockSpec((1, QK_W, s), lambda bi: (bi, 0, 0)),
                  pl.BlockSpec((1, s, N_HEADS * K_AUG), lambda bi: (bi, 0, 0)),
                  pl.BlockSpec((1, N_HEADS * V_AUG, s), lambda bi: (bi, 0, 0)),
                  _resident((1, V_DIM))],
        out_specs=pl.BlockSpec((1, s, ATTN_W), lambda bi: (bi, 0, 0)),
        out_shape=jax.ShapeDtypeStruct((b, s, ATTN_W), BF16),
        scratch_shapes=[pltpu.VMEM((2, N_HEADS, K_AUG, 4 * blk), BF16),
                        pltpu.VMEM((blk, 2 * blk), F32),
                        pltpu.VMEM((N_HEADS, V_AUG, 4 * blk), F32),
                        pltpu.VMEM((2, N_HEADS, blk, 4 * blk), F32)],
        compiler_params=pltpu.CompilerParams(dimension_semantics=("arbitrary",),
                                             vmem_limit_bytes=VMEM_LIMIT),
        name="diff_attn",
    )(lam, q_t, k_n, v_t, sub_gain)

    tb = BACK_TILE
    tok_b = lambda w: pl.BlockSpec((1, tb, w), lambda bi, ti: (bi, ti, 0))
    out = pl.pallas_call(
        _back_kernel,
        grid=(b, s // tb),
        in_specs=[tok_b(d), tok_b(ATTN_W), tok_b(CONV_W), _resident((ATTN_W + CONV_W, d)),
                  _resident((1, d)), _resident((d, d_ff)), _resident((d, d_ff)),
                  _resident((d_ff, d)), _resident((1, d))],
        out_specs=tok_b(d),
        out_shape=jax.ShapeDtypeStruct((b, s, d), F32),
        scratch_shapes=[pltpu.VMEM((tb, d), F32),
                        pltpu.VMEM((tb, d), BF16),
                        pltpu.VMEM((tb, d_ff), BF16)],
        compiler_params=params,
        name="back",
    )(x1, attn, c_n, w_out_bf, row(ffn2_norm[0]), w2_gate, w2_up, w2_down, row(final_norm[0]))
    return out
```

```python
import math

import jax
import jax.numpy as jnp
import numpy as np
from jax import lax
from jax.experimental import pallas as pl
from jax.experimental.pallas import tpu as pltpu

F32 = jnp.float32
BF16 = jnp.bfloat16

N_HEADS = 4
QK_DIM = 64
V_DIM = 128
HEAD_COLS = 2 * QK_DIM
K_AUG = 2 * HEAD_COLS
N_BIAS_COLS = 3
BF16_ROWS = 16
V_PAD = BF16_ROWS
V_AUG = V_DIM + V_PAD
ATTN_W = N_HEADS * V_DIM
CONV_W = 512
QK_W = N_HEADS * HEAD_COLS
N_PROJ = 6
CONV_GROUP = 64
MXU_W = 256
GMAT_W = MXU_W
CONV_K = 3
NORM_EPS = 1e-6
FFN_RES_W = 0.5
LAM_INIT = 0.8 - 0.6 * math.exp(-0.3 * 0)
NEG_BIG = -1e30

FFN_CHUNK = 256
TOKEN_TILE = 512
BACK_TILE = 1024
SUB_TILE = 256
BACK_SUB = 512
NORM_ROWS = 32
ATT_BLOCK = 256
CARRY_ROWS = 8
STAGE_ROWS = 256
VMEM_LIMIT = 60000 * 1024


def _rms(x, gain):
    ms = jnp.mean(x * x, axis=-1, keepdims=True)
    return x * lax.rsqrt(ms + NORM_EPS) * gain


def _group_rms_scale(t, gmat_ref, group):
    t2 = (t * t).astype(BF16)
    w = gmat_ref.shape[0]
    ss = jnp.concatenate(
        [jnp.dot(t2[:, c:c + w], gmat_ref[...], preferred_element_type=F32)
         for c in range(0, t.shape[1], w)], axis=1)
    return lax.rsqrt(ss * (1.0 / group) + NORM_EPS)


def _norm_jobs(src_ref, gain_ref, dst_ref, r0, sub, dtype):
    def job(g):
        rows = slice(r0 + g * NORM_ROWS, r0 + (g + 1) * NORM_ROWS)
        dst_ref[rows, :] = _rms(src_ref[rows, :], gain_ref[...]).astype(dtype)
    return [lambda g=g: job(g) for g in range(sub // NORM_ROWS)]


def _run(jobs):
    for job in jobs:
        job()


def _gate_up(h_buf, wg_ref, wu_ref, a_buf, r0, sub, side_jobs):
    rows = slice(r0, r0 + sub)
    side_jobs = list(side_jobs)
    n_chunks = wg_ref.shape[1] // FFN_CHUNK
    done = 0
    for c in range(n_chunks):
        cols = slice(c * FFN_CHUNK, (c + 1) * FFN_CHUNK)
        h = h_buf[rows, :]
        g = jnp.dot(h, wg_ref[:, cols], preferred_element_type=F32)
        u = jnp.dot(h, wu_ref[:, cols], preferred_element_type=F32)
        a_buf[rows, cols] = (g * jax.nn.sigmoid(g) * u).astype(BF16)
        upto = -(-(c + 1) * len(side_jobs) // n_chunks)
        _run(side_jobs[done:upto])
        done = upto


def _front_kernel(x_ref, bias_ref, g1_ref, wg_hbm, wu_hbm, wd_hbm, g2_ref, win_hbm,
                  gmat_ref, qg_ref, kg_ref, cw_ref, cg_ref,
                  later_w0, later_w1, later_w2, later_w3,
                  x1_ref, qt_ref, k_ref, vt_ref, c_ref,
                  later_o0, later_o1, later_o2, later_o3,
                  h_buf, a_buf, h2_buf, u_buf, gb_buf,
                  wg_ref, wu_ref, wd_ref, win_ref, stage, stage_sem):
    tm = x_ref.shape[1]

    @pl.when((pl.program_id(0) == 0) & (pl.program_id(1) == 0))
    def _():
        rows = stage.shape[1]
        chunks = [(src, dst, r0) for src, dst in ((wg_hbm, wg_ref), (wu_hbm, wu_ref),
                                                  (win_hbm, win_ref), (wd_hbm, wd_ref))
                  for r0 in range(0, src.shape[0], rows)]

        def copy(i):
            src, _, r0 = chunks[i]
            return pltpu.make_async_copy(src.at[pl.ds(r0, rows), :],
                                         stage.at[i % 2, :, pl.ds(0, src.shape[1])],
                                         stage_sem.at[i % 2])

        copy(0).start()
        for i, (src, dst, r0) in enumerate(chunks):
            if i + 1 < len(chunks):
                copy(i + 1).start(priority=(i + 1) % 2)
            copy(i).wait()
            dst[r0:r0 + rows, :] = stage[i % 2, :, 0:src.shape[1]].astype(BF16)
    x_rows, x1_rows = x_ref.at[0], x1_ref.at[0]
    sub_tiles = list(range(0, tm, SUB_TILE))

    @pl.when(pl.program_id(1) == 0)
    def _():
        u_buf[0:CARRY_ROWS, :] = jnp.zeros((CARRY_ROWS, CONV_W), F32)

    def down(r0):
        rows = slice(r0, r0 + SUB_TILE)
        x1_ref[0, rows, :] = x_ref[0, rows, :] + FFN_RES_W * jnp.dot(
            a_buf[rows, :], wd_ref[...], preferred_element_type=F32)

    def proj(r0, j):
        return jnp.dot(h2_buf[r0:r0 + SUB_TILE, :], win_ref[:, j * QK_W:(j + 1) * QK_W],
                       preferred_element_type=F32)

    def q_job(r0):
        q = proj(r0, 0)
        q = q * _group_rms_scale(q, gmat_ref, QK_DIM) * qg_ref[...]
        qt_ref[0, :, r0:r0 + SUB_TILE] = q.T.astype(BF16)

    def k_job(r0):
        rows = slice(r0, r0 + SUB_TILE)
        k = proj(r0, 1)
        k = (k * _group_rms_scale(k, gmat_ref, QK_DIM) * kg_ref[...]).astype(BF16)
        for hd in range(N_HEADS):
            k_ref[0, rows, hd * K_AUG:hd * K_AUG + HEAD_COLS] = (
                k[:, hd * HEAD_COLS:(hd + 1) * HEAD_COLS])
            k_ref[0, rows, hd * K_AUG + HEAD_COLS:(hd + 1) * K_AUG] = (
                bias_ref[rows, hd * HEAD_COLS:(hd + 1) * HEAD_COLS])

    def v_job(r0):
        rows = slice(r0, r0 + SUB_TILE)
        vt = proj(r0, 2).T.astype(BF16)
        ones_rows = jnp.where(lax.broadcasted_iota(jnp.int32, (V_PAD, SUB_TILE), 0) == 0, 1.0, 0.0)
        for hd in range(N_HEADS):
            vt_ref[0, hd * V_AUG:hd * V_AUG + V_DIM, rows] = vt[hd * V_DIM:(hd + 1) * V_DIM, :]
            vt_ref[0, hd * V_AUG + V_DIM:(hd + 1) * V_AUG, rows] = ones_rows.astype(BF16)

    def conv_in_job(r0):
        gb_buf[r0:r0 + SUB_TILE, :] = proj(r0, 3)
        u0 = CARRY_ROWS + r0
        u_buf[u0:u0 + SUB_TILE, :] = proj(r0, 4) * proj(r0, 5)

    def conv_out_job(r0):
        u0 = CARRY_ROWS + r0
        y = (cw_ref[0:1, :] * u_buf[u0 - 2:u0 - 2 + SUB_TILE, :]
             + cw_ref[1:2, :] * u_buf[u0 - 1:u0 - 1 + SUB_TILE, :]
             + cw_ref[2:3, :] * u_buf[u0:u0 + SUB_TILE, :])
        c = gb_buf[r0:r0 + SUB_TILE, :] * y
        c = c * _group_rms_scale(c, gmat_ref, CONV_GROUP) * cg_ref[...]
        c_ref[0, r0:r0 + SUB_TILE, :] = c.astype(BF16)

    def mixer_jobs(r0):
        return [lambda: q_job(r0), lambda: k_job(r0), lambda: v_job(r0),
                lambda: conv_in_job(r0), lambda: conv_out_job(r0)]

    def cast_job(w_ref, o_ref):
        o_ref[...] = w_ref[...].astype(BF16)

    _run(_norm_jobs(x_rows, g1_ref, h_buf, sub_tiles[0], SUB_TILE, BF16))
    pending = [lambda: cast_job(later_w0, later_o0), lambda: cast_job(later_w1, later_o1),
               lambda: cast_job(later_w2, later_o2), lambda: cast_job(later_w3, later_o3)]
    for i, r0 in enumerate(sub_tiles):
        nxt = sub_tiles[i + 1] if i + 1 < len(sub_tiles) else None
        side = ((_norm_jobs(x_rows, g1_ref, h_buf, nxt, SUB_TILE, BF16) if nxt is not None else [])
                + pending)
        _gate_up(h_buf, wg_ref, wu_ref, a_buf, r0, SUB_TILE, side)
        down(r0)
        pending = _norm_jobs(x1_rows, g2_ref, h2_buf, r0, SUB_TILE, BF16) + mixer_jobs(r0)
    _run(pending)

    u_buf[0:CARRY_ROWS, :] = u_buf[tm:tm + CARRY_ROWS, :]


def _attn_kernel(lam_ref, qt_ref, k_ref, vt_ref, sg_ref, o_ref,
                 rhs_buf, mask_buf, acc_buf, s_buf):
    blk = ATT_BLOCK
    half = 2 * blk
    seq = k_ref.shape[1]
    lam = lam_ref[0]
    heads = range(N_HEADS)
    all_cols, first, second = slice(0, 2 * half), slice(0, half), slice(half, 2 * half)

    row = lax.broadcasted_iota(jnp.int32, (blk, half), 0)
    col = lax.broadcasted_iota(jnp.int32, (blk, half), 1)
    col = jnp.where(col >= blk, col - blk, col)
    mask_buf[...] = jnp.where(row <= col, 0.0, NEG_BIG)

    rhs_buf[...] = jnp.zeros(rhs_buf.shape, BF16)
    for par in range(2):
        for h in heads:
            rhs_buf[par, h, HEAD_COLS:HEAD_COLS + BF16_ROWS, :] = jnp.where(
                lax.broadcasted_iota(jnp.int32, (BF16_ROWS, 2 * half), 0) < N_BIAS_COLS, 1.0, 0.0
            ).astype(BF16)

    def block_start(kb):
        return kb * blk if isinstance(kb, int) else pl.multiple_of(kb * blk, blk)

    def build_rhs(qb, par, h):
        q0 = block_start(2 * qb)
        r0 = h * HEAD_COLS
        for part in range(2):
            qs = pl.ds(q0 + part * blk, blk)
            c0 = part * half
            rhs_buf[par, h, 0:QK_DIM, c0:c0 + blk] = qt_ref[0, r0:r0 + QK_DIM, qs]
            rhs_buf[par, h, QK_DIM:HEAD_COLS, c0 + blk:c0 + half] = (
                qt_ref[0, r0 + QK_DIM:r0 + HEAD_COLS, qs])

    def scores(kb, slot, par, h, cols, want_max=True):
        s = jnp.dot(k_ref[0, pl.ds(block_start(kb), blk), h * K_AUG:(h + 1) * K_AUG],
                    rhs_buf[par, h, :, cols], preferred_element_type=F32)
        s_buf[slot, h, :, cols] = s
        return jnp.max(s, axis=0, keepdims=True) if want_max else None

    def masked_probs(slot, h, cols, m):
        hb = blk // 2
        live = [slice(hb, blk), slice(blk + hb, 2 * blk)]
        t_top = s_buf[slot, h, 0:hb, cols] + mask_buf[0:hb, :]
        t_bot = [s_buf[slot, h, hb:blk, cols.start + c.start:cols.start + c.stop]
                 + mask_buf[hb:blk, c] for c in live]
        dead_max = jnp.full((1, hb), NEG_BIG, F32)
        bot_max = jnp.concatenate([dead_max, jnp.max(t_bot[0], axis=0, keepdims=True),
                                   dead_max, jnp.max(t_bot[1], axis=0, keepdims=True)], axis=1)
        m_new = jnp.maximum(m, jnp.maximum(jnp.max(t_top, axis=0, keepdims=True), bot_max))
        dead_p = jnp.zeros((hb, hb), BF16)
        p_bot = jnp.concatenate(
            [dead_p, jnp.exp2(t_bot[0] - m_new[:, live[0]]).astype(BF16),
             dead_p, jnp.exp2(t_bot[1] - m_new[:, live[1]]).astype(BF16)], axis=1)
        p = jnp.concatenate([jnp.exp2(t_top - m_new).astype(BF16), p_bot], axis=0)
        return m_new, p

    def softmax_pv(kb, slot, h, cols, m, tile_max, masked):
        if masked:
            m_new, p = masked_probs(slot, h, cols, m)
        else:
            m_new = jnp.maximum(m, tile_max)
            p = jnp.exp2(s_buf[slot, h, :, cols] - m_new).astype(BF16)
        alpha = jnp.exp2(m - m_new)
        pv = jnp.dot(vt_ref[0, h * V_AUG:(h + 1) * V_AUG, pl.ds(block_start(kb), blk)], p,
                     preferred_element_type=F32)
        acc_buf[h, :, cols] = acc_buf[h, :, cols] * alpha + pv
        return m_new

    def finalize(qb, h):
        inv_l = 1.0 / acc_buf[h, V_DIM:V_DIM + 1, :]
        for part in range(2):
            c0 = part * half
            a = (acc_buf[h, 0:V_DIM, c0:c0 + blk] * inv_l[:, c0:c0 + blk]
                 - acc_buf[h, 0:V_DIM, c0 + blk:c0 + half]
                 * (lam * inv_l[:, c0 + blk:c0 + half]))
            ms_a = jnp.mean(a * a, axis=0, keepdims=True)
            a = a * lax.rsqrt(ms_a + NORM_EPS)
            o_ref[0, pl.ds(block_start(2 * qb + part), blk), h * V_DIM:(h + 1) * V_DIM] = (
                (a.T * sg_ref[...]).astype(BF16))

    def q_block(qb, tms, odd):
        par = int(odd)
        acc_buf[...] = jnp.zeros(acc_buf.shape, F32)
        ms = (jnp.full((1, 2 * half), NEG_BIG, F32),) * N_HEADS

        def full_step(kb, slot, carry):
            ms, tms = carry
            new_ms, new_tms = [], []
            for h in heads:
                new_tms.append(scores(kb + 1, 1 - slot, par, h, all_cols))
                new_ms.append(softmax_pv(kb, slot, h, all_cols, ms[h], tms[h], masked=False))
            return tuple(new_ms), tuple(new_tms)

        def pair(kp, c):
            return full_step(2 * kp + 1, 1, full_step(2 * kp, 0, c))

        ms, tms = lax.fori_loop(0, qb // 2, lambda kq, c: pair(2 * kq + 1, pair(2 * kq, c)),
                                (ms, tms))
        if odd:
            ms, tms = pair(qb - 1, (ms, tms))

        for h in heads:
            scores(2 * qb + 1, 1, par, h, second, want_max=False)
            softmax_pv(2 * qb, 0, h, first, ms[h][:, first], None, masked=True)
            m_b = softmax_pv(2 * qb, 0, h, second, ms[h][:, second], tms[h][:, second],
                             masked=False)
            softmax_pv(2 * qb + 1, 1, h, second, m_b, None, masked=True)

        nxt = jnp.minimum(qb + 1, n_qb - 1)
        new_tms = []
        for h in heads:
            build_rhs(nxt, 1 - par, h)
            new_tms.append(scores(0, 0, 1 - par, h, all_cols))
            finalize(qb, h)
        return tuple(new_tms)

    n_qb = seq // (2 * blk)
    for h in heads:
        build_rhs(0, 0, h)
    lax.fori_loop(0, n_qb // 2,
                  lambda j, tms: q_block(2 * j + 1, q_block(2 * j, tms, odd=False), odd=True),
                  tuple(scores(0, 0, 0, h, all_cols) for h in heads))


def _back_kernel(x1_ref, a_ref, c_ref, wo_ref, g3_ref, wg_ref, wu_ref, wd_ref, gf_ref,
                 out_ref, x2_buf, h_buf, a_buf):
    tm = x1_ref.shape[1]
    out_rows = out_ref.at[0]
    sub_tiles = list(range(0, tm, BACK_SUB))

    def out_proj(r0):
        rows = slice(r0, r0 + BACK_SUB)
        x2_buf[rows, :] = (
            x1_ref[0, rows, :]
            + jnp.dot(a_ref[0, rows, :], wo_ref[0:ATTN_W, :], preferred_element_type=F32)
            + jnp.dot(c_ref[0, rows, :], wo_ref[ATTN_W:ATTN_W + CONV_W, :],
                      preferred_element_type=F32))

    def down(r0):
        rows = slice(r0, r0 + BACK_SUB)
        out_ref[0, rows, :] = x2_buf[rows, :] + FFN_RES_W * jnp.dot(
            a_buf[rows, :], wd_ref[...], preferred_element_type=F32)

    out_proj(sub_tiles[0])
    _run(_norm_jobs(x2_buf, g3_ref, h_buf, sub_tiles[0], BACK_SUB, BF16))
    pending = []
    for i, r0 in enumerate(sub_tiles):
        nxt = sub_tiles[i + 1] if i + 1 < len(sub_tiles) else None
        side = list(pending)
        if nxt is not None:
            side = ([lambda nxt=nxt: out_proj(nxt)] + side
                    + _norm_jobs(x2_buf, g3_ref, h_buf, nxt, BACK_SUB, BF16))
        _gate_up(h_buf, wg_ref, wu_ref, a_buf, r0, BACK_SUB, side)
        down(r0)
        pending = _norm_jobs(out_rows, gf_ref, out_rows, r0, BACK_SUB, F32)
    _run(pending)


def _alibi_columns(seq):
    slopes = np.exp2(-8.0 * np.arange(1, N_HEADS + 1) / N_HEADS) * math.log2(math.e)
    rest = (slopes[None, :] * np.arange(seq)[:, None]).astype(np.float32)
    table = np.zeros((seq, N_HEADS, HEAD_COLS), np.float32)
    for c in range(N_BIAS_COLS):
        piece = (rest.view(np.uint32) & np.uint32(0xFFFF0000)).view(np.float32)
        table[:, :, c] = piece
        rest = rest - piece
    return table.reshape(seq, N_HEADS * HEAD_COLS)


def _resident(shape):
    return pl.BlockSpec(shape, lambda *_: (0,) * len(shape), pipeline_mode=pl.Buffered(1))


def kernel(x, ffn1_norm, ffn1_w_gate, ffn1_w_up, ffn1_w_down, mix_norm, w_in, q_norm, k_norm,
           lambda_q1, lambda_k1, lambda_q2, lambda_k2, attn_subln, conv_w, conv_norm, w_out,
           ffn2_norm, ffn2_w_gate, ffn2_w_up, ffn2_w_down, final_norm):
    b, s, d = x.shape
    d_ff = ffn1_w_gate.shape[-1]
    tm = TOKEN_TILE
    nt = s // tm
    assert s % tm == 0 and tm % SUB_TILE == 0 and d_ff % FFN_CHUNK == 0
    assert s % BACK_TILE == 0 and BACK_TILE % BACK_SUB == 0
    assert s % (4 * ATT_BLOCK) == 0
    assert ffn1_w_gate.shape[0] == 1, "single layer"

    row = lambda v: v.reshape(1, -1).astype(F32)
    n_groups = QK_W // QK_DIM
    gid = jnp.arange(GMAT_W) // QK_DIM
    gmat = (gid[:, None] == gid[None, :]).astype(BF16)
    log2e = math.log2(math.e)
    q_gain = jnp.tile(q_norm[0].astype(F32), n_groups).reshape(1, QK_W) * (QK_DIM ** -0.5 * log2e)
    k_gain = jnp.tile(k_norm[0].astype(F32), n_groups).reshape(1, QK_W)
    bias_cols = jnp.asarray(_alibi_columns(s), dtype=BF16)
    sub_gain = row(attn_subln[0]) * (1.0 - LAM_INIT)
    lam = (jnp.exp(jnp.sum(lambda_q1[0].astype(F32) * lambda_k1[0].astype(F32)))
           - jnp.exp(jnp.sum(lambda_q2[0].astype(F32) * lambda_k2[0].astype(F32)))
           + LAM_INIT).reshape(1)

    tok = lambda w: pl.BlockSpec((1, tm, w), lambda bi, ti: (bi, ti, 0))
    tok_t = lambda r: pl.BlockSpec((1, r, tm), lambda bi, ti: (bi, 0, ti))
    params = pltpu.CompilerParams(dimension_semantics=("arbitrary", "arbitrary"),
                                  vmem_limit_bytes=VMEM_LIMIT)

    later = [ffn2_w_gate[0], ffn2_w_up[0], ffn2_w_down[0], w_out[0]]
    n_steps = b * nt

    def cast_spec(w):
        n_rows = w.shape[0]
        rows = next(r for r in range(BF16_ROWS, n_rows + 1, BF16_ROWS)
                    if n_rows % r == 0 and n_rows // r <= n_steps)
        last = n_rows // rows - 1
        return pl.BlockSpec((rows, w.shape[1]),
                            lambda bi, ti: (jnp.minimum(bi * nt + ti, last), 0))

    hbm = pl.BlockSpec(memory_space=pl.ANY)
    assert d % STAGE_ROWS == 0 and d_ff % STAGE_ROWS == 0 and d_ff <= N_PROJ * QK_W
    outs = pl.pallas_call(
        _front_kernel,
        grid=(b, nt),
        in_specs=[tok(d), pl.BlockSpec((tm, N_HEADS * HEAD_COLS), lambda bi, ti: (ti, 0)),
                  _resident((1, d)), hbm, hbm, hbm, _resident((1, d)), hbm,
                  _resident((GMAT_W, GMAT_W)), _resident((1, QK_W)), _resident((1, QK_W)),
                  _resident((CONV_K, CONV_W)), _resident((1, CONV_W))]
                 + [cast_spec(w) for w in later],
        out_specs=[tok(d), tok_t(QK_W), tok(N_HEADS * K_AUG), tok_t(N_HEADS * V_AUG), tok(CONV_W)]
                  + [cast_spec(w) for w in later],
        out_shape=[jax.ShapeDtypeStruct((b, s, d), F32),
                   jax.ShapeDtypeStruct((b, QK_W, s), BF16),
                   jax.ShapeDtypeStruct((b, s, N_HEADS * K_AUG), BF16),
                   jax.ShapeDtypeStruct((b, N_HEADS * V_AUG, s), BF16),
                   jax.ShapeDtypeStruct((b, s, CONV_W), BF16)]
                  + [jax.ShapeDtypeStruct(w.shape, BF16) for w in later],
        scratch_shapes=[pltpu.VMEM((tm, d), BF16),
                        pltpu.VMEM((tm, d_ff), BF16),
                        pltpu.VMEM((tm, d), BF16),
                        pltpu.VMEM((tm + CARRY_ROWS, CONV_W), F32),
                        pltpu.VMEM((tm, CONV_W), F32),
                        pltpu.VMEM((d, d_ff), BF16), pltpu.VMEM((d, d_ff), BF16),
                        pltpu.VMEM((d_ff, d), BF16), pltpu.VMEM((d, N_PROJ * QK_W), BF16),
                        pltpu.VMEM((2, STAGE_ROWS, N_PROJ * QK_W), F32),
                        pltpu.SemaphoreType.DMA((2,))],
        compiler_params=params,
        name="front",
    )(x, bias_cols, row(ffn1_norm[0]), ffn1_w_gate[0], ffn1_w_up[0], ffn1_w_down[0],
      row(mix_norm[0]), w_in[0], gmat, q_gain, k_gain,
      conv_w[0].astype(F32), row(conv_norm[0]), *later)
    x1, q_t, k_n, v_t, c_n, w2_gate, w2_up, w2_down, w_out_bf = outs

    blk = ATT_BLOCK
    attn = pl.pallas_call(
        _attn_kernel,
        grid=(b,),
        in_specs=[pl.BlockSpec(memory_space=pltpu.SMEM),
                  pl.BlockSpec((1, QK_W, s), lambda bi: (bi, 0, 0)),
                  pl.BlockSpec((1, s, N_HEADS * K_AUG), lambda bi: (bi, 0, 0)),
                  pl.BlockSpec((1, N_HEADS * V_AUG, s), lambda bi: (bi, 0, 0)),
                  _resident((1, V_DIM))],
        out_specs=pl.BlockSpec((1, s, ATTN_W), lambda bi: (bi, 0, 0)),
        out_shape=jax.ShapeDtypeStruct((b, s, ATTN_W), BF16),
        scratch_shapes=[pltpu.VMEM((2, N_HEADS, K_AUG, 4 * blk), BF16),
                        pltpu.VMEM((blk, 2 * blk), F32),
                        pltpu.VMEM((N_HEADS, V_AUG, 4 * blk), F32),
                        pltpu.VMEM((2, N_HEADS, blk, 4 * blk), F32)],
        compiler_params=pltpu.CompilerParams(dimension_semantics=("arbitrary",),
                                             vmem_limit_bytes=VMEM_LIMIT),
        name="diff_attn",
    )(lam, q_t, k_n, v_t, sub_gain)

    tb = BACK_TILE
    tok_b = lambda w: pl.BlockSpec((1, tb, w), lambda bi, ti: (bi, ti, 0))
    out = pl.pallas_call(
        _back_kernel,
        grid=(b, s // tb),
        in_specs=[tok_b(d), tok_b(ATTN_W), tok_b(CONV_W), _resident((ATTN_W + CONV_W, d)),
                  _resident((1, d)), _resident((d, d_ff)), _resident((d, d_ff)),
                  _resident((d_ff, d)), _resident((1, d))],
        out_specs=tok_b(d),
        out_shape=jax.ShapeDtypeStruct((b, s, d), F32),
        scratch_shapes=[pltpu.VMEM((tb, d), F32),
                        pltpu.VMEM((tb, d), BF16),
                        pltpu.VMEM((tb, d_ff), BF16)],
        compiler_params=params,
        name="back",
    )(x1, attn, c_n, w_out_bf, row(ffn2_norm[0]), w2_gate, w2_up, w2_down, row(final_norm[0]))
    return out
```
